```python
import math
import jax
import jax.numpy as jnp
from jax import lax
import numpy as np

D_MODEL = 1024
BATCH = 8
SEQ = 4096
DEPTH = 2

MIX_WIDTH = D_MODEL
N_MIXERS = 4
GROUP_WIDTH = MIX_WIDTH // N_MIXERS
HEAD_DIM = 64
N_HEADS_GROUP = GROUP_WIDTH // HEAD_DIM
ROPE_THETA = 10000.0
NORM_EPS = 1e-6
Q_BLOCK = 128
NEG_INF = -1e30

NSA_CMP_LEN = 32
NSA_CMP_STRIDE = 16
NSA_CMP_HIDDEN = 2 * HEAD_DIM
NSA_SEL_LEN = 64
NSA_TOP_N = 16
NSA_WINDOW = 512
NSA_FORCE_BONUS = 1e3

GDN_CONV = 4
GDN_CHUNK = 64

MLSTM_QK_DIM = HEAD_DIM // 2
MLSTM_CHUNK = 64
MLSTM_GATE_CAP = 15.0

DIFF_SUB_DIM = HEAD_DIM // 2

MEM_TOKENS = 256
MEM_HEADS = 4
MEM_HEAD_DIM = 64

MOE_GROUPS = 4
MOE_EXPERTS_PER_GROUP = 8
MOE_N_EXPERTS = MOE_GROUPS * MOE_EXPERTS_PER_GROUP
MOE_TOP_K = 2
MOE_FF = D_MODEL // 4
MOE_TOKEN_BLOCK = 128

IN_WIDTHS = (
    GROUP_WIDTH, HEAD_DIM, HEAD_DIM, HEAD_DIM, HEAD_DIM, HEAD_DIM, HEAD_DIM, 3 * N_HEADS_GROUP,
    GROUP_WIDTH, GROUP_WIDTH, GROUP_WIDTH, N_HEADS_GROUP, N_HEADS_GROUP, GROUP_WIDTH,
    N_HEADS_GROUP * MLSTM_QK_DIM, N_HEADS_GROUP * MLSTM_QK_DIM, GROUP_WIDTH, N_HEADS_GROUP, N_HEADS_GROUP, GROUP_WIDTH,
    GROUP_WIDTH, GROUP_WIDTH, GROUP_WIDTH,
)
IN_COLS = sum(IN_WIDTHS)

kernel_name = 'hybrid_nsa_gdn_mlstm_diffattn_hmoe'


def rms_norm(x, g):
    xf = x.astype(jnp.float32)
    y = xf * lax.rsqrt(jnp.mean(xf * xf, axis=-1, keepdims=True) + NORM_EPS)
    return (y * g.astype(jnp.float32)).astype(x.dtype)


def l2_normalize(x):
    return x * lax.rsqrt(jnp.sum(x * x, axis=-1, keepdims=True) + NORM_EPS)


def rope_tables(pos, dim):
    inv_freq = 1.0 / (ROPE_THETA ** (jnp.arange(0, dim, 2, dtype=jnp.float32) / dim))
    ang = pos.astype(jnp.float32)[:, None] * inv_freq[None, :]
    return jnp.cos(ang), jnp.sin(ang)


def apply_rope(x, cos, sin):
    x1, x2 = jnp.split(x.astype(jnp.float32), 2, axis=-1)
    return jnp.concatenate([x1 * cos - x2 * sin, x2 * cos + x1 * sin], axis=-1).astype(x.dtype)


def masked_softmax(s, mask):
    p = jax.nn.softmax(jnp.where(mask, s, NEG_INF), axis=-1)
    return jnp.where(mask, p, 0.0)


def soft_cap(x, cap):
    return cap * jnp.tanh(x / cap)


def causal_depthwise_conv(x, w):
    k_len, ch = w.shape
    return lax.conv_general_dilated(
        x, w[:, None, :].astype(x.dtype), window_strides=(1,), padding=[(k_len - 1, 0)],
        dimension_numbers=('NWC', 'WIO', 'NWC'), feature_group_count=ch)


def nsa_overlap(n_cmp, n_sel):
    c0 = NSA_CMP_STRIDE * np.arange(n_cmp)[:, None]
    s0 = NSA_SEL_LEN * np.arange(n_sel)[None, :]
    ov = np.clip(np.minimum(c0 + NSA_CMP_LEN, s0 + NSA_SEL_LEN) - np.maximum(c0, s0), 0, None)
    return (ov / NSA_CMP_STRIDE).astype(np.float32)


def nsa_mixer(q, kc, vc, ks, vs, kw, vw, gates, qk_gain, cmp_pe, cmp_w1, cmp_w2, cos, sin):
    B, S, _ = q.shape
    H, Dh = N_HEADS_GROUP, HEAD_DIM
    scale = Dh ** -0.5
    nb = S // Q_BLOCK
    n_cmp = (S - NSA_CMP_LEN) // NSA_CMP_STRIDE + 1
    n_sel = S // NSA_SEL_LEN
    n_top = min(NSA_TOP_N, n_sel)

    q = apply_rope(rms_norm(q.reshape(B, S, H, Dh), qk_gain[0]).transpose(0, 2, 1, 3), cos, sin)

    cmp_idx = NSA_CMP_STRIDE * np.arange(n_cmp)[:, None] + np.arange(NSA_CMP_LEN)[None, :]
    cmp_end = NSA_CMP_STRIDE * np.arange(n_cmp) + NSA_CMP_LEN - 1

    def compress(t, j):
        blocks = t[:, cmp_idx] + cmp_pe[j]
        hid = jax.nn.gelu(blocks.reshape(B, n_cmp, -1) @ cmp_w1[j])
        return hid @ cmp_w2[j]

    kc = apply_rope(rms_norm(compress(kc, 0), qk_gain[1]), cos[cmp_end], sin[cmp_end])
    vc = compress(vc, 1)
    ks = apply_rope(rms_norm(ks, qk_gain[2]), cos, sin)
    ks_blocks = ks.reshape(B, n_sel, NSA_SEL_LEN, Dh)
    vs_blocks = vs.reshape(B, n_sel, NSA_SEL_LEN, Dh)
    kw = apply_rope(rms_norm(kw, qk_gain[3]), cos, sin)
    kw_pad = jnp.pad(kw, ((0, 0), (NSA_WINDOW, 0), (0, 0)))
    vw_pad = jnp.pad(vw, ((0, 0), (NSA_WINDOW, 0), (0, 0)))
    gates = jax.nn.sigmoid(gates.reshape(B, S, H, 3).transpose(0, 2, 1, 3))
    overlap = jnp.asarray(nsa_overlap(n_cmp, n_sel))
    sel_ids = jnp.arange(n_sel)
    win_offs = jnp.arange(Q_BLOCK + NSA_WINDOW)

    def block(i):
        t0 = i * Q_BLOCK
        t = t0 + jnp.arange(Q_BLOCK)
        qb = lax.dynamic_slice_in_dim(q, t0, Q_BLOCK, axis=2)
        gb = lax.dynamic_slice_in_dim(gates, t0, Q_BLOCK, axis=2)
        s = jnp.einsum('bhqd,bcd->bhqc', qb, kc).astype(jnp.float32) * scale
        p_c = masked_softmax(s, cmp_end[None, :] <= t[:, None])
        o_c = jnp.einsum('bhqc,bcd->bhqd', p_c.astype(vc.dtype), vc)
        imp = jnp.einsum('bhqc,cn->bqn', p_c, overlap)
        cur = t // NSA_SEL_LEN
        valid = sel_ids[None, :] <= cur[:, None]
        forced = (sel_ids[None, :] == 0) | (sel_ids[None, :] == cur[:, None]) | (sel_ids[None, :] == cur[:, None] - 1)
        score = jnp.where(valid, imp + jnp.where(forced, NSA_FORCE_BONUS, 0.0), NEG_INF)
        _, idx = lax.top_k(score, n_top)
        kg = jax.vmap(lambda kb, ix: kb[ix])(ks_blocks, idx)
        vg = jax.vmap(lambda vb, ix: vb[ix])(vs_blocks, idx)
        s = jnp.einsum('bhqd,bqnrd->bhqnr', qb, kg).astype(jnp.float32) * scale
        kpos = idx[..., None] * NSA_SEL_LEN + jnp.arange(NSA_SEL_LEN)
        sel_mask = (kpos <= t[None, :, None, None]).reshape(B, 1, Q_BLOCK, -1)
        p_s = masked_softmax(s.reshape(B, H, Q_BLOCK, -1), sel_mask)
        o_s = jnp.einsum('bhqk,bqkd->bhqd', p_s.astype(vg.dtype), vg.reshape(B, Q_BLOCK, -1, Dh))
        kwb = lax.dynamic_slice_in_dim(kw_pad, t0, Q_BLOCK + NSA_WINDOW, axis=1)
        vwb = lax.dynamic_slice_in_dim(vw_pad, t0, Q_BLOCK + NSA_WINDOW, axis=1)
        spos = t0 - NSA_WINDOW + win_offs
        dist = t[:, None] - spos[None, :]
        w_mask = (dist >= 0) & (dist < NSA_WINDOW) & (spos[None, :] >= 0)
        s = jnp.einsum('bhqd,bkd->bhqk', qb, kwb).astype(jnp.float32) * scale
        p_w = masked_softmax(s, w_mask)
        o_w = jnp.einsum('bhqk,bkd->bhqd', p_w.astype(vwb.dtype), vwb)
        return gb[..., 0:1] * o_c + gb[..., 1:2] * o_s + gb[..., 2:3] * o_w

    out = lax.map(block, jnp.arange(nb))
    return out.transpose(1, 0, 3, 2, 4).reshape(B, S, H * Dh)


def gated_delta_chunked(q, k, v, g, beta):
    B, H, S, Dk = q.shape
    Dv = v.shape[-1]
    C = GDN_CHUNK
    N = S // C
    q = q.reshape(B, H, N, C, Dk)
    k = k.reshape(B, H, N, C, Dk)
    v = v.reshape(B, H, N, C, Dv)
    g = g.reshape(B, H, N, C)
    beta = beta.reshape(B, H, N, C)
    gc = jnp.cumsum(g, axis=-1)
    incl = jnp.tril(jnp.ones((C, C), bool))
    strict = jnp.tril(jnp.ones((C, C), bool), -1)
    decay = jnp.exp(jnp.where(incl, gc[..., :, None] - gc[..., None, :], NEG_INF))
    k_beta = k * beta[..., None]
    a_low = jnp.where(strict, jnp.einsum('bhnid,bhnjd->bhnij', k_beta, k) * decay, 0.0)
    system = a_low + jnp.eye(C, dtype=jnp.float32)
    rhs = jnp.concatenate([v * beta[..., None], k_beta * jnp.exp(gc)[..., None]], axis=-1)
    sol = lax.linalg.triangular_solve(system, rhs, left_side=True, lower=True)
    u, w = sol[..., :Dv], sol[..., Dv:]
    qd = q * jnp.exp(gc)[..., None]
    attn = jnp.einsum('bhnid,bhnjd->bhnij', q, k) * decay
    g_last = gc[..., -1]
    k_end = k * jnp.exp(g_last[..., None] - gc)[..., None]

    def step(state, xs):
        qd_c, w_c, u_c, a_c, ke_c, gl_c = xs
        v_new = u_c - jnp.einsum('bhck,bhkv->bhcv', w_c, state)
        o_c = jnp.einsum('bhck,bhkv->bhcv', qd_c, state) + jnp.einsum('bhcj,bhjv->bhcv', a_c, v_new)
        state = state * jnp.exp(gl_c)[..., None, None] + jnp.einsum('bhck,bhcv->bhkv', ke_c, v_new)
        return state, o_c

    xs = tuple(jnp.moveaxis(t, 2, 0) for t in (qd, w, u, attn, k_end, g_last))
    _, o = lax.scan(step, jnp.zeros((B, H, Dk, Dv), jnp.float32), xs)
    return jnp.moveaxis(o, 0, 2).reshape(B, H, S, Dv)


def gdn_mixer(q, k, v, a, b, z, conv_w, a_log, dt_bias, norm_g):
    B, S, _ = q.shape
    H, Dh = N_HEADS_GROUP, HEAD_DIM
    dtype = q.dtype
    qkv = jax.nn.silu(causal_depthwise_conv(jnp.concatenate([q, k, v], axis=-1), conv_w))
    q, k, v = jnp.split(qkv, 3, axis=-1)
    heads = lambda t: t.reshape(B, S, H, Dh).transpose(0, 2, 1, 3).astype(jnp.float32)
    q = l2_normalize(heads(q)) * Dh ** -0.5
    k = l2_normalize(heads(k))
    v = heads(v)
    beta = jax.nn.sigmoid(b.astype(jnp.float32)).transpose(0, 2, 1)
    g = -jnp.exp(a_log.astype(jnp.float32)) * jax.nn.softplus(a.astype(jnp.float32) + dt_bias.astype(jnp.float32))
    o = gated_delta_chunked(q, k, v, g.transpose(0, 2, 1), beta)
    o = rms_norm(o.transpose(0, 2, 1, 3), norm_g) * jax.nn.silu(z.astype(jnp.float32).reshape(B, S, H, Dh))
    return o.reshape(B, S, H * Dh).astype(dtype)


def mlstm_chunked(q, k, v, i_g, log_f):
    B, H, S, Dk = q.shape
    Dv = v.shape[-1]
    C = MLSTM_CHUNK
    N = S // C
    q = q.reshape(B, H, N, C, Dk)
    k = k.reshape(B, H, N, C, Dk)
    v = v.reshape(B, H, N, C, Dv)
    i_g = i_g.reshape(B, H, N, C)
    b = jnp.cumsum(log_f.reshape(B, H, N, C), axis=-1)
    causal = jnp.tril(jnp.ones((C, C), bool))
    d_log = jnp.where(causal, b[..., :, None] - b[..., None, :] + i_g[..., None, :], NEG_INF)
    a_end = b[..., -1:] - b + i_g
    m_loc = jnp.max(a_end, axis=-1)
    w_end = jnp.exp(a_end - m_loc[..., None])
    c_loc = jnp.einsum('bhncd,bhnce,bhnc->bhnde', k, v, w_end)
    n_loc = jnp.einsum('bhncd,bhnc->bhnd', k, w_end)
    b_last = b[..., -1]

    def step(carry, xs):
        c_st, n_st, m_st = carry
        c_l, n_l, m_l, b_l = xs
        m_new = jnp.maximum(b_l + m_st, m_l)
        s_prev = jnp.exp(b_l + m_st - m_new)
        s_loc = jnp.exp(m_l - m_new)
        c_new = s_prev[..., None, None] * c_st + s_loc[..., None, None] * c_l
        n_new = s_prev[..., None] * n_st + s_loc[..., None] * n_l
        return (c_new, n_new, m_new), (c_st, n_st, m_st)

    init = (jnp.zeros((B, H, Dk, Dv), jnp.float32), jnp.zeros((B, H, Dk), jnp.float32), jnp.zeros((B, H), jnp.float32))
    xs = tuple(jnp.moveaxis(t, 2, 0) for t in (c_loc, n_loc, m_loc, b_last))
    _, (c_in, n_in, m_in) = lax.scan(step, init, xs)
    c_in = jnp.moveaxis(c_in, 0, 2)
    n_in = jnp.moveaxis(n_in, 0, 2)
    m_in = jnp.moveaxis(m_in, 0, 2)
    inter_log = b + m_in[..., None]
    m_t = jnp.maximum(inter_log, jnp.max(d_log, axis=-1))
    w_inter = jnp.exp(inter_log - m_t)
    w_intra = jnp.exp(d_log - m_t[..., None]) * jnp.einsum('bhncd,bhnsd->bhncs', q, k)
    num = w_inter[..., None] * jnp.einsum('bhncd,bhnde->bhnce', q, c_in) + jnp.einsum('bhncs,bhnse->bhnce', w_intra, v)
    den = w_inter * jnp.einsum('bhncd,bhnd->bhnc', q, n_in) + jnp.sum(w_intra, axis=-1)
    h = num / jnp.maximum(jnp.abs(den), jnp.exp(-m_t))[..., None]
    return h.reshape(B, H, S, Dv)


def mlstm_mixer(q, k, v, i_pre, f_pre, o_pre, i_bias, f_bias, norm_g):
    B, S, _ = q.shape
    H = N_HEADS_GROUP
    dtype = q.dtype
    heads = lambda t, d: t.reshape(B, S, H, d).transpose(0, 2, 1, 3).astype(jnp.float32)
    q = heads(q, MLSTM_QK_DIM)
    k = heads(k, MLSTM_QK_DIM) * MLSTM_QK_DIM ** -0.5
    v = heads(v, HEAD_DIM)
    i_g = soft_cap(i_pre.astype(jnp.float32) + i_bias.astype(jnp.float32), MLSTM_GATE_CAP).transpose(0, 2, 1)
    log_f = jax.nn.log_sigmoid(soft_cap(f_pre.astype(jnp.float32) + f_bias.astype(jnp.float32), MLSTM_GATE_CAP)).transpose(0, 2, 1)
    h = mlstm_chunked(q, k, v, i_g, log_f)
    h = rms_norm(h.transpose(0, 2, 1, 3), norm_g) * jax.nn.sigmoid(o_pre.astype(jnp.float32).reshape(B, S, H, HEAD_DIM))
    return h.reshape(B, S, H * HEAD_DIM).astype(dtype)


def diff_mixer(q, k, v, qk_gain, lam, norm_g, lambda_init, cos, sin):
    B, S, _ = q.shape
    H, d = N_HEADS_GROUP, DIFF_SUB_DIM
    nb = S // Q_BLOCK
    q = apply_rope(rms_norm(q.reshape(B, S, H, 2, d).transpose(0, 2, 3, 1, 4), qk_gain[0]), cos, sin)
    k = apply_rope(rms_norm(k.reshape(B, S, H, 2, d).transpose(0, 2, 3, 1, 4), qk_gain[1]), cos, sin)
    v = v.reshape(B, S, H, HEAD_DIM).transpose(0, 2, 1, 3)
    lam = lam.astype(jnp.float32)
    lam_full = jnp.exp(jnp.sum(lam[0] * lam[1])) - jnp.exp(jnp.sum(lam[2] * lam[3])) + lambda_init
    scale = d ** -0.5
    kpos = jnp.arange(S)

    def block(i):
        t0 = i * Q_BLOCK
        qb = lax.dynamic_slice_in_dim(q, t0, Q_BLOCK, axis=3)
        s = jnp.einsum('bhcqd,bhckd->bhcqk', qb, k).astype(jnp.float32) * scale
        p = masked_softmax(s, kpos[None, :] <= (t0 + jnp.arange(Q_BLOCK))[:, None])
        a = p[:, :, 0] - lam_full * p[:, :, 1]
        return jnp.einsum('bhqk,bhkd->bhqd', a.astype(v.dtype), v)

    o = lax.map(block, jnp.arange(nb))
    o = o.transpose(1, 0, 3, 2, 4).reshape(B, S, H, HEAD_DIM)
    o = rms_norm(o, norm_g) * (1.0 - lambda_init)
    return o.reshape(B, S, H * HEAD_DIM)


def memory_cross_attention(hn, memn, wq, wkv, qk_gain, wo):
    B, S, _ = hn.shape
    M = memn.shape[1]
    q = (hn @ wq).reshape(B, S, MEM_HEADS, MEM_HEAD_DIM).transpose(0, 2, 1, 3)
    k, v = jnp.split(memn @ wkv, 2, axis=-1)
    k = k.reshape(B, M, MEM_HEADS, MEM_HEAD_DIM).transpose(0, 2, 1, 3)
    v = v.reshape(B, M, MEM_HEADS, MEM_HEAD_DIM).transpose(0, 2, 1, 3)
    q = rms_norm(q, qk_gain[0])
    k = rms_norm(k, qk_gain[1])
    s = jnp.einsum('bhsd,bhmd->bhsm', q, k).astype(jnp.float32) * MEM_HEAD_DIM ** -0.5
    p = jax.nn.softmax(s, axis=-1)
    o = jnp.einsum('bhsm,bhmd->bhsd', p.astype(v.dtype), v)
    return o.transpose(0, 2, 1, 3).reshape(B, S, MEM_HEADS * MEM_HEAD_DIM) @ wo


def hierarchical_moe(hn, w_group, b_group, w_expert, b_expert, w1, w3, w2):
    B, S, D = hn.shape
    xt = hn.reshape(-1, D)
    T = xt.shape[0]
    grp_prob = jax.nn.softmax((xt @ w_group).astype(jnp.float32) + b_group.astype(jnp.float32), axis=-1)
    grp_p, grp_idx = lax.top_k(grp_prob, 1)
    exp_logits = ((xt @ w_expert).astype(jnp.float32) + b_expert.astype(jnp.float32)).reshape(T, MOE_GROUPS, MOE_EXPERTS_PER_GROUP)
    exp_logits = jnp.take_along_axis(exp_logits, grp_idx[:, :, None], axis=1)[:, 0]
    top_p, top_idx = lax.top_k(jax.nn.softmax(exp_logits, axis=-1), MOE_TOP_K)
    top_w = top_p / jnp.sum(top_p, axis=-1, keepdims=True) * grp_p
    w_in_group = jnp.sum(jax.nn.one_hot(top_idx, MOE_EXPERTS_PER_GROUP, dtype=jnp.float32) * top_w[..., None], axis=1)
    combine = (jax.nn.one_hot(grp_idx[:, 0], MOE_GROUPS, dtype=jnp.float32)[:, :, None] * w_in_group[:, None, :])
    combine = combine.reshape(T, MOE_N_EXPERTS).astype(xt.dtype)
    nblk = T // MOE_TOKEN_BLOCK

    def block(args):
        xb, cb = args
        hg = jnp.einsum('td,edf->tef', xb, w1)
        hu = jnp.einsum('td,edf->tef', xb, w3)
        act = jax.nn.silu(hg) * hu * cb[:, :, None]
        return jnp.einsum('tef,efd->td', act, w2)

    y = lax.map(block, (xt.reshape(nblk, MOE_TOKEN_BLOCK, D), combine.reshape(nblk, MOE_TOKEN_BLOCK, MOE_N_EXPERTS)))
    return y.reshape(B, S, D)


def setup_inputs(seed: int = 0) -> dict:
    key = jax.random.key(seed)
    keys = iter(jax.random.split(key, 64))
    L, H, D = DEPTH, N_HEADS_GROUP, D_MODEL

    def normal(shape, scale):
        return jax.random.normal(next(keys), shape, jnp.float32) * scale

    def gain(shape):
        return 1.0 + normal(shape, 0.02)

    dt = jnp.exp(jax.random.uniform(next(keys), (L, H), jnp.float32, math.log(1e-3), math.log(1e-1)))
    return {
        'x': normal((BATCH, SEQ, D), 1.0),
        'mem': normal((BATCH, MEM_TOKENS, D), 1.0),
        'ln_mix': gain((L, D)),
        'w_in': normal((L, D, IN_COLS), D ** -0.5),
        'w_out': normal((L, MIX_WIDTH, D), MIX_WIDTH ** -0.5),
        'nsa_qk_gain': gain((L, 4, HEAD_DIM)),
        'nsa_cmp_pe': normal((L, 2, NSA_CMP_LEN, HEAD_DIM), 0.02),
        'nsa_cmp_w1': normal((L, 2, NSA_CMP_LEN * HEAD_DIM, NSA_CMP_HIDDEN), (NSA_CMP_LEN * HEAD_DIM) ** -0.5),
        'nsa_cmp_w2': normal((L, 2, NSA_CMP_HIDDEN, HEAD_DIM), NSA_CMP_HIDDEN ** -0.5),
        'gdn_conv': normal((L, GDN_CONV, 3 * GROUP_WIDTH), GDN_CONV ** -0.5),
        'gdn_a_log': jnp.log(jax.random.uniform(next(keys), (L, H), jnp.float32, 1.0, 16.0)),
        'gdn_dt_bias': dt + jnp.log(-jnp.expm1(-dt)),
        'gdn_norm': gain((L, HEAD_DIM)),
        'mlstm_i_bias': normal((L, H), 0.1),
        'mlstm_f_bias': jnp.linspace(3.0, 6.0, H, dtype=jnp.float32)[None, :] + normal((L, H), 0.1),
        'mlstm_norm': gain((L, HEAD_DIM)),
        'diff_qk_gain': gain((L, 2, DIFF_SUB_DIM)),
        'diff_lambda': normal((L, 4, DIFF_SUB_DIM), 0.1),
        'diff_norm': gain((L, HEAD_DIM)),
        'ln_xattn': gain((L, D)),
        'ln_mem': gain((L, D)),
        'xattn_wq': normal((L, D, MEM_HEADS * MEM_HEAD_DIM), D ** -0.5),
        'xattn_wkv': normal((L, D, 2 * MEM_HEADS * MEM_HEAD_DIM), D ** -0.5),
        'xattn_qk_gain': gain((L, 2, MEM_HEAD_DIM)),
        'xattn_wo': normal((L, MEM_HEADS * MEM_HEAD_DIM, D), (MEM_HEADS * MEM_HEAD_DIM) ** -0.5),
        'ln_moe': gain((L, D)),
        'moe_w_group': normal((L, D, MOE_GROUPS), D ** -0.5),
        'moe_b_group': normal((L, MOE_GROUPS), 0.01),
        'moe_w_expert': normal((L, D, MOE_N_EXPERTS), D ** -0.5),
        'moe_b_expert': normal((L, MOE_N_EXPERTS), 0.01),
        'moe_w1': normal((L, MOE_N_EXPERTS, D, MOE_FF), D ** -0.5),
        'moe_w3': normal((L, MOE_N_EXPERTS, D, MOE_FF), D ** -0.5),
        'moe_w2': normal((L, MOE_N_EXPERTS, MOE_FF, D), MOE_FF ** -0.5),
    }


def reference(x, mem, ln_mix, w_in, w_out, nsa_qk_gain, nsa_cmp_pe, nsa_cmp_w1, nsa_cmp_w2,
              gdn_conv, gdn_a_log, gdn_dt_bias, gdn_norm, mlstm_i_bias, mlstm_f_bias, mlstm_norm,
              diff_qk_gain, diff_lambda, diff_norm, ln_xattn, ln_mem, xattn_wq, xattn_wkv,
              xattn_qk_gain, xattn_wo, ln_moe, moe_w_group, moe_b_group, moe_w_expert,
              moe_b_expert, moe_w1, moe_w3, moe_w2):
    S = x.shape[1]
    pos = jnp.arange(S)
    cos_a, sin_a = rope_tables(pos, HEAD_DIM)
    cos_d, sin_d = rope_tables(pos, DIFF_SUB_DIM)
    split_at = np.cumsum(IN_WIDTHS)[:-1].tolist()
    h = x
    for l in range(DEPTH):
        hn = rms_norm(h, ln_mix[l])
        parts = jnp.split(hn @ w_in[l], split_at, axis=-1)
        y_a = nsa_mixer(*parts[0:8], nsa_qk_gain[l], nsa_cmp_pe[l], nsa_cmp_w1[l], nsa_cmp_w2[l], cos_a, sin_a)
        y_b = gdn_mixer(*parts[8:14], gdn_conv[l], gdn_a_log[l], gdn_dt_bias[l], gdn_norm[l])
        y_c = mlstm_mixer(*parts[14:20], mlstm_i_bias[l], mlstm_f_bias[l], mlstm_norm[l])
        lambda_init = 0.8 - 0.6 * math.exp(-0.3 * l)
        y_d = diff_mixer(*parts[20:23], diff_qk_gain[l], diff_lambda[l], diff_norm[l], lambda_init, cos_d, sin_d)
        h = h + jnp.concatenate([y_a, y_b, y_c, y_d], axis=-1) @ w_out[l]
        h = h + memory_cross_attention(rms_norm(h, ln_xattn[l]), rms_norm(mem, ln_mem[l]),
                                       xattn_wq[l], xattn_wkv[l], xattn_qk_gain[l], xattn_wo[l])
        h = h + hierarchical_moe(rms_norm(h, ln_moe[l]), moe_w_group[l], moe_b_group[l], moe_w_expert[l],
                                 moe_b_expert[l], moe_w1[l], moe_w3[l], moe_w2[l])
    return h
```

```python
import functools
import math

import numpy as np
import jax
import jax.numpy as jnp
from jax import lax
from jax.experimental import pallas as pl
from jax.experimental.pallas import tpu as pltpu

F32 = jnp.float32
BF16 = jnp.bfloat16
HI = lax.Precision.HIGHEST

D_MODEL = 1024
HEAD_DIM = 64
N_HEADS = 4
GROUP_WIDTH = 256
ROPE_THETA = 10000.0
EPS = 1e-6
NEG = -1e30

NSA_CMP_LEN = 32
NSA_CMP_STRIDE = 16
NSA_SEL_LEN = 64
NSA_TOP_N = 16
NSA_WINDOW = 512
NSA_FORCE_BONUS = 1e3
CHUNK = 64
MLSTM_QK = 32
GATE_CAP = 15.0
DIFF_SUB = 32
MOE_GROUPS = 4
MOE_PER_GROUP = 8
MOE_EXPERTS = 32
MOE_FF = 256
LANES = 128
VMEM_LIMIT = 48 * 1024 * 1024

_A0, _B0, _C0, _D0 = 0, 652, 1684, 2460
_IN_COLS = 3228
_GROUPS = (
    ("a_q", ((_A0, 256),), F32),
    ("a_kv", ((_A0 + 384, 64), (_A0 + 512, 64), (_A0 + 448, 64), (_A0 + 576, 64), (_A0 + 256, 64), (_A0 + 320, 64)), F32),
    ("b_qkv", ((_B0, 768),), F32),
    ("b_z", ((_B0 + 776, 256),), F32),
    ("c_qk", ((_C0, 256),), F32),
    ("c_v", ((_C0 + 256, 256),), F32),
    ("c_o", ((_C0 + 520, 256),), F32),
    ("d_qk", ((_D0, 512),), F32),
    ("d_v", ((_D0 + 512, 256),), BF16),
    ("small", ((_A0 + 640, 12), (_B0 + 768, 4), (_B0 + 772, 4), (_C0 + 512, 4), (_C0 + 516, 4)), F32),
)
_LANE_GDN_A, _LANE_GDN_B, _LANE_I, _LANE_F = 12, 16, 20, 24


def _dot(a, b, prec=None):
    return jnp.dot(a, b, preferred_element_type=F32, precision=prec)


def _dot_nt(a, b, prec=None):
    return lax.dot_general(a, b, (((1,), (1,)), ((), ())), preferred_element_type=F32, precision=prec)


def _dot_tn(a, b, prec=None):
    return lax.dot_general(a, b, (((0,), (0,)), ((), ())), preferred_element_type=F32, precision=prec)


def _params(*sem):
    return pltpu.CompilerParams(dimension_semantics=sem, vmem_limit_bytes=VMEM_LIMIT)


def _group_matrix(width, gsz, mean):
    g = np.kron(np.eye(width // gsz), np.ones((gsz, gsz)))
    return jnp.asarray(g / gsz if mean else g, F32)


def _expand_matrix(src_lane0, n, out_per, stride=1):
    e = np.zeros((LANES, n * out_per), np.float32)
    for h in range(n):
        e[src_lane0 + stride * h, h * out_per:(h + 1) * out_per] = 1.0
    return jnp.asarray(e)


def _row128(vals, lane0):
    return jnp.zeros((1, LANES), F32).at[0, lane0:lane0 + vals.shape[0]].set(vals.astype(F32))


def _rope_tables(seq, dim):
    inv = 1.0 / (ROPE_THETA ** (jnp.arange(0, dim, 2, dtype=F32) / dim))
    ang = jnp.arange(seq).astype(F32)[:, None] * inv[None, :]
    cos, sin = jnp.cos(ang), jnp.sin(ang)
    cosd = jnp.concatenate([cos, cos], axis=-1)
    sind = jnp.concatenate([-sin, sin], axis=-1)
    rep = LANES // dim
    return jnp.tile(cosd, (1, rep)), jnp.tile(sind, (1, rep))


def _rope128(x, cos, sin_signed, half):
    left = pltpu.roll(x, LANES - half, 1)
    right = pltpu.roll(x, half, 1)
    lane = lax.broadcasted_iota(jnp.int32, x.shape, 1)
    first = (lane & (2 * half - 1)) < half
    return x * cos + jnp.where(first, left, right) * sin_signed


def _rep_lanes(x, width):
    return x if width == LANES else jnp.concatenate([x] * (width // LANES), axis=1)


def _softmax_rows(s, mask):
    m = jnp.max(s, axis=-1, keepdims=True)
    e = jnp.exp(s - m)
    return jnp.where(mask, e / jnp.sum(e, axis=-1, keepdims=True), 0.0)


def _inproj_kernel(x_ref, g_ref, w_ref, *outs, widths):
    x = x_ref[...]
    ms = jnp.mean(x * x, axis=-1, keepdims=True)
    xn = (x * lax.rsqrt(ms + EPS) * g_ref[...]).astype(BF16)
    off = 0
    for o, wd in zip(outs, widths):
        o[...] = _dot(xn, w_ref[:, off:off + wd]).astype(o.dtype)
        off += wd


def _permute_w_in(w):
    cols, widths = [], []
    for _, parts, _ in _GROUPS:
        width = sum(p[1] for p in parts)
        padded = -(-width // LANES) * LANES
        for s, n in parts:
            cols.append(w[:, s:s + n])
        if padded > width:
            cols.append(jnp.zeros((w.shape[0], padded - width), w.dtype))
        widths.append(padded)
    return jnp.concatenate(cols, axis=1).astype(BF16), tuple(widths)


def _inproj(h2d, gain, w_in, tm=256):
    T = h2d.shape[0]
    wp, widths = _permute_w_in(w_in)
    out_shape = [jax.ShapeDtypeStruct((T, wd), g[2]) for wd, g in zip(widths, _GROUPS)]
    return pl.pallas_call(
        functools.partial(_inproj_kernel, widths=widths),
        out_shape=out_shape,
        grid=(T // tm,),
        in_specs=[pl.BlockSpec((tm, D_MODEL), lambda i: (i, 0)),
                  pl.BlockSpec((1, D_MODEL), lambda i: (0, 0)),
                  pl.BlockSpec((D_MODEL, sum(widths)), lambda i: (0, 0))],
        out_specs=[pl.BlockSpec((tm, wd), lambda i: (i, 0)) for wd in widths],
        compiler_params=_params("parallel"),
        name="inproj",
    )(h2d, gain.reshape(1, D_MODEL), wp)


def _normrope_kernel(x_ref, gain_ref, gm_ref, cos_ref, sin_ref, o_ref, *, half, nchunk):
    for c in range(nchunk):
        cs = slice(c * LANES, (c + 1) * LANES)
        x = x_ref[:, cs]
        ms = _dot(x * x, gm_ref[...], HI)
        xn = x * lax.rsqrt(ms + EPS) * gain_ref[:, cs]
        o_ref[:, cs] = _rope128(xn, cos_ref[...], sin_ref[...], half).astype(o_ref.dtype)


def _normrope(x, gain_row, gsz, cos, sin, width, B, S, tm=512):
    tm = min(tm, S)
    ns = S // tm
    return pl.pallas_call(
        functools.partial(_normrope_kernel, half=gsz // 2, nchunk=width // LANES),
        out_shape=jax.ShapeDtypeStruct((B * S, width), BF16),
        grid=(B, ns),
        in_specs=[pl.BlockSpec((tm, width), lambda b, i: (b * ns + i, 0)),
                  pl.BlockSpec((1, width), lambda b, i: (0, 0)),
                  pl.BlockSpec((LANES, LANES), lambda b, i: (0, 0)),
                  pl.BlockSpec((tm, LANES), lambda b, i: (i, 0)),
                  pl.BlockSpec((tm, LANES), lambda b, i: (i, 0))],
        out_specs=pl.BlockSpec((tm, width), lambda b, i: (b * ns + i, 0)),
        compiler_params=_params("parallel", "parallel"),
        name="normrope",
    )(x, gain_row, _group_matrix(LANES, gsz, True), cos, sin)


def _nsa_cmp_kernel(kr_ref, vr_ref, pe_ref, w1_ref, w2_ref, gain_ref, cos_ref, sin_ref, kc_ref, vc_ref, *, nc):
    half_in = NSA_CMP_STRIDE * HEAD_DIM

    def compress(r, j):
        a = _dot(r, w1_ref[j, :half_in, :], HI)
        b = _dot(r, w1_ref[j, half_in:, :], HI)
        pe = jnp.broadcast_to(pe_ref[j], (8, 2 * half_in))
        c = _dot(pe, w1_ref[j], HI)[0:1]
        hid = jax.nn.gelu(a + pltpu.roll(b, nc - 1, 0) + c)
        return _dot(hid, w2_ref[j], HI)

    kc = compress(kr_ref[0], 0)
    kc = kc * lax.rsqrt(jnp.mean(kc * kc, axis=-1, keepdims=True) + EPS) * gain_ref[...]
    x1, x2 = kc[:, :HEAD_DIM // 2], kc[:, HEAD_DIM // 2:]
    cos, sin = cos_ref[...], sin_ref[...]
    kc_ref[0] = jnp.concatenate([x1 * cos - x2 * sin, x2 * cos + x1 * sin], axis=1)
    vc_ref[0] = compress(vr_ref[0], 1)


def _nsa_compress(kraw, vraw, pe, w1, w2, gain, cos_c, sin_c, B, nc):
    wide = NSA_CMP_STRIDE * HEAD_DIM
    return pl.pallas_call(
        functools.partial(_nsa_cmp_kernel, nc=nc),
        out_shape=[jax.ShapeDtypeStruct((B, nc, HEAD_DIM), F32)] * 2,
        grid=(B,),
        in_specs=[pl.BlockSpec((1, nc, wide), lambda b: (b, 0, 0)),
                  pl.BlockSpec((1, nc, wide), lambda b: (b, 0, 0)),
                  pl.BlockSpec((2, 1, 2 * wide), lambda b: (0, 0, 0)),
                  pl.BlockSpec((2, 2 * wide, 2 * HEAD_DIM), lambda b: (0, 0, 0)),
                  pl.BlockSpec((2, 2 * HEAD_DIM, HEAD_DIM), lambda b: (0, 0, 0)),
                  pl.BlockSpec((1, HEAD_DIM), lambda b: (0, 0)),
                  pl.BlockSpec((nc, HEAD_DIM // 2), lambda b: (0, 0)),
                  pl.BlockSpec((nc, HEAD_DIM // 2), lambda b: (0, 0))],
        out_specs=[pl.BlockSpec((1, nc, HEAD_DIM), lambda b: (b, 0, 0))] * 2,
        compiler_params=_params("parallel"),
        name="nsa_compress",
    )(kraw, vraw, pe.reshape(2, 1, 2 * wide), w1, w2, gain.reshape(1, HEAD_DIM), cos_c, sin_c)


def _nsa1_kernel(q_ref, sm_ref, kc_ref, vc_ref, kn_ref, vv_ref, cos_ref, sin_ref, gq_ref, gm_ref, ovl_ref,
                 eg0_ref, eg2_ref, qn_ref, part_ref, sel_ref, *, tq, nc, n_sel):
    t0 = pl.program_id(1) * tq
    scale = HEAD_DIM ** -0.5
    chunks = []
    for c in range(GROUP_WIDTH // LANES):
        cs = slice(c * LANES, (c + 1) * LANES)
        x = q_ref[:, cs]
        ms = _dot(x * x, gm_ref[...], HI)
        chunks.append(_rope128(x * lax.rsqrt(ms + EPS) * gq_ref[:, cs], cos_ref[...], sin_ref[...], HEAD_DIM // 2))
    qn = jnp.concatenate(chunks, axis=1)
    qn_ref[...] = qn.astype(BF16)
    qb = qn.astype(BF16)
    sig = jax.nn.sigmoid(sm_ref[...])
    g0x = _dot(sig, eg0_ref[...], HI)
    g2x = _dot(sig, eg2_ref[...], HI)
    tpos = t0 + lax.broadcasted_iota(jnp.int32, (tq, 1), 0)

    kc = kc_ref[0]
    vc = vc_ref[0].astype(BF16)
    cidx = lax.broadcasted_iota(jnp.int32, (1, nc), 1)
    cmask = ((NSA_CMP_STRIDE * cidx + NSA_CMP_LEN - 1) <= tpos) & (cidx < nc - 1)
    psum = jnp.zeros((tq, nc), F32)
    o_cmp = []
    for h in range(N_HEADS):
        hs = slice(h * HEAD_DIM, (h + 1) * HEAD_DIM)
        s = jnp.where(cmask, _dot_nt(qn[:, hs], kc, HI) * scale, NEG)
        p = _softmax_rows(s, cmask)
        o_cmp.append(_dot(p.astype(BF16), vc))
        psum = psum + p

    imp = _dot(psum, ovl_ref[...], HI)
    j = lax.broadcasted_iota(jnp.int32, (tq, LANES), 1)
    cur = tpos >> 6
    valid = j <= cur
    forced = (j == 0) | (j == cur) | (j == cur - 1)
    score = jnp.where(valid, imp + jnp.where(forced, NSA_FORCE_BONUS, 0.0), NEG)
    st = score.T
    jj = lax.broadcasted_iota(jnp.int32, (LANES, 1), 0)
    cnt = jnp.zeros((LANES, tq), F32)
    for i in range(n_sel):
        row = st[i:i + 1, :]
        tie = jnp.where(jj > i, 1.0, 0.0)
        cnt = cnt + jnp.where(row > st, 1.0, jnp.where(row == st, tie, 0.0))
    sel_t = jnp.where((cnt < min(NSA_TOP_N, n_sel)) & (st > 0.5 * NEG), 1.0, 0.0)
    sel_ref[...] = sel_t.T.astype(BF16)

    band = tq + NSA_WINDOW
    start = pl.multiple_of(jnp.maximum(t0 - NSA_WINDOW, 0), LANES)
    kw = kn_ref[pl.ds(start, band), HEAD_DIM:2 * HEAD_DIM]
    vw = vv_ref[pl.ds(start, band), HEAD_DIM:2 * HEAD_DIM]
    dist = tpos - (start + lax.broadcasted_iota(jnp.int32, (1, band), 1))
    wmask = (dist >= 0) & (dist < NSA_WINDOW)
    o_win = []
    for h in range(N_HEADS):
        hs = slice(h * HEAD_DIM, (h + 1) * HEAD_DIM)
        s = jnp.where(wmask, _dot_nt(qb[:, hs], kw) * scale, NEG)
        p = _softmax_rows(s, wmask)
        o_win.append(_dot(p.astype(BF16), vw))
    part_ref[...] = g0x * jnp.concatenate(o_cmp, axis=1) + g2x * jnp.concatenate(o_win, axis=1)


def _nsa_overlap(nc, n_sel):
    c0 = NSA_CMP_STRIDE * np.arange(nc)[:, None]
    s0 = NSA_SEL_LEN * np.arange(n_sel)[None, :]
    ov = np.clip(np.minimum(c0 + NSA_CMP_LEN, s0 + NSA_SEL_LEN) - np.maximum(c0, s0), 0, None) / NSA_CMP_STRIDE
    ov[nc - 1:] = 0.0
    out = np.zeros((nc, LANES), np.float32)
    out[:, :n_sel] = ov
    return jnp.asarray(out)


def _nsa1(a_q, small, kc, vc, kn, vv, cos, sin, gq_row, B, S, tq=128):
    nq = S // tq
    nc = S // NSA_CMP_STRIDE
    n_sel = S // NSA_SEL_LEN
    row = lambda b, i: (b * nq + i, 0)
    return pl.pallas_call(
        functools.partial(_nsa1_kernel, tq=tq, nc=nc, n_sel=n_sel),
        out_shape=[jax.ShapeDtypeStruct((B * S, GROUP_WIDTH), BF16),
                   jax.ShapeDtypeStruct((B * S, GROUP_WIDTH), F32),
                   jax.ShapeDtypeStruct((B * S, LANES), BF16)],
        grid=(B, nq),
        in_specs=[pl.BlockSpec((tq, GROUP_WIDTH), row),
                  pl.BlockSpec((tq, LANES), row),
                  pl.BlockSpec((1, nc, HEAD_DIM), lambda b, i: (b, 0, 0)),
                  pl.BlockSpec((1, nc, HEAD_DIM), lambda b, i: (b, 0, 0)),
                  pl.BlockSpec((S, LANES), lambda b, i: (b, 0)),
                  pl.BlockSpec((S, LANES), lambda b, i: (b, 0)),
                  pl.BlockSpec((tq, LANES), lambda b, i: (i, 0)),
                  pl.BlockSpec((tq, LANES), lambda b, i: (i, 0)),
                  pl.BlockSpec((1, GROUP_WIDTH), lambda b, i: (0, 0)),
                  pl.BlockSpec((LANES, LANES), lambda b, i: (0, 0)),
                  pl.BlockSpec((nc, LANES), lambda b, i: (0, 0)),
                  pl.BlockSpec((LANES, GROUP_WIDTH), lambda b, i: (0, 0)),
                  pl.BlockSpec((LANES, GROUP_WIDTH), lambda b, i: (0, 0))],
        out_specs=[pl.BlockSpec((tq, GROUP_WIDTH), row),
                   pl.BlockSpec((tq, GROUP_WIDTH), row),
                   pl.BlockSpec((tq, LANES), row)],
        compiler_params=_params("parallel", "parallel"),
        name="nsa_cmp_win_select",
    )(a_q, small, kc, vc, kn, vv, cos, sin, gq_row, _group_matrix(LANES, HEAD_DIM, True),
      _nsa_overlap(nc, n_sel), _expand_matrix(0, N_HEADS, HEAD_DIM, 3), _expand_matrix(2, N_HEADS, HEAD_DIM, 3))


def _nsa2_kernel(qn_ref, sel_ref, e_ref, kn_ref, vv_ref, part_ref, sm_ref, eg1_ref, o_ref, m_sc, l_sc, acc_sc,
                 *, tq, tk, nk):
    qi = pl.program_id(1)
    ki = pl.program_id(2)
    scale = HEAD_DIM ** -0.5

    @pl.when(ki == 0)
    def _():
        m_sc[...] = jnp.full(m_sc.shape, NEG, F32)
        l_sc[...] = jnp.zeros(l_sc.shape, F32)
        acc_sc[...] = jnp.zeros(acc_sc.shape, F32)

    @pl.when(ki * tk <= qi * tq + tq - 1)
    def _():
        km = _dot(sel_ref[...], e_ref[...])
        tpos = qi * tq + lax.broadcasted_iota(jnp.int32, (tq, 1), 0)
        kpos = ki * tk + lax.broadcasted_iota(jnp.int32, (1, tk), 1)
        mask = (km > 0.5) & (kpos <= tpos)
        ks = kn_ref[:, 0:HEAD_DIM]
        vs = vv_ref[:, 0:HEAD_DIM]
        for h in range(N_HEADS):
            hs = slice(h * HEAD_DIM, (h + 1) * HEAD_DIM)
            s = jnp.where(mask, _dot_nt(qn_ref[:, hs], ks) * scale, NEG)
            m_prev = m_sc[h]
            m_new = jnp.maximum(m_prev, jnp.max(s, axis=-1, keepdims=True))
            alpha = jnp.exp(m_prev - m_new)
            p = jnp.exp(s - _rep_lanes(m_new, tk))
            l_sc[h] = alpha * l_sc[h] + jnp.sum(p, axis=-1, keepdims=True)
            acc_sc[h] = alpha[:, :HEAD_DIM] * acc_sc[h] + _dot(p.astype(BF16), vs)
            m_sc[h] = m_new

    @pl.when(ki == nk - 1)
    def _():
        g1x = _dot(jax.nn.sigmoid(sm_ref[...]), eg1_ref[...], HI)
        o = jnp.concatenate([acc_sc[h] / l_sc[h][:, :HEAD_DIM] for h in range(N_HEADS)], axis=1)
        o_ref[...] = (part_ref[...] + g1x * o).astype(o_ref.dtype)


def _nsa2(qn, sel, kn, vv, part, small, B, S, tq=128, tk=512):
    tk = min(tk, S)
    nq, nk = S // tq, S // tk
    n_sel = S // NSA_SEL_LEN
    e = np.zeros((LANES, S), np.float32)
    e[np.arange(S) // NSA_SEL_LEN, np.arange(S)] = 1.0
    row = lambda b, i, k: (b * nq + i, 0)
    kcl = lambda b, i, k: (b * nk + jnp.minimum(k, (i * tq + tq - 1) // tk), 0)
    return pl.pallas_call(
        functools.partial(_nsa2_kernel, tq=tq, tk=tk, nk=nk),
        out_shape=jax.ShapeDtypeStruct((B * S, GROUP_WIDTH), BF16),
        grid=(B, nq, nk),
        in_specs=[pl.BlockSpec((tq, GROUP_WIDTH), row),
                  pl.BlockSpec((tq, LANES), row),
                  pl.BlockSpec((LANES, tk), lambda b, i, k: (0, jnp.minimum(k, (i * tq + tq - 1) // tk))),
                  pl.BlockSpec((tk, LANES), kcl),
                  pl.BlockSpec((tk, LANES), kcl),
                  pl.BlockSpec((tq, GROUP_WIDTH), row),
                  pl.BlockSpec((tq, LANES), row),
                  pl.BlockSpec((LANES, GROUP_WIDTH), lambda b, i, k: (0, 0))],
        out_specs=pl.BlockSpec((tq, GROUP_WIDTH), row),
        scratch_shapes=[pltpu.VMEM((N_HEADS, tq, LANES), F32),
                        pltpu.VMEM((N_HEADS, tq, LANES), F32),
                        pltpu.VMEM((N_HEADS, tq, HEAD_DIM), F32)],
        compiler_params=_params("parallel", "parallel", "arbitrary"),
        name="nsa_selected",
    )(qn, sel, jnp.asarray(e, BF16), kn, vv, part, small, _expand_matrix(1, N_HEADS, HEAD_DIM, 3))


def _diff_kernel(q_ref, k_ref, v_ref, lam_ref, gain_ref, o_ref, m_sc, l_sc, acc_sc, *, tq, tk, nk, lambda_init):
    qi = pl.program_id(1)
    ki = pl.program_id(2)
    scale = DIFF_SUB ** -0.5

    @pl.when(ki == 0)
    def _():
        m_sc[...] = jnp.full(m_sc.shape, NEG, F32)
        l_sc[...] = jnp.zeros(l_sc.shape, F32)
        acc_sc[...] = jnp.zeros(acc_sc.shape, F32)

    @pl.when(ki * tk <= qi * tq + tq - 1)
    def _():
        tpos = qi * tq + lax.broadcasted_iota(jnp.int32, (tq, 1), 0)
        kpos = ki * tk + lax.broadcasted_iota(jnp.int32, (1, tk), 1)
        mask = kpos <= tpos
        for h in range(N_HEADS):
            vh = v_ref[:, h * HEAD_DIM:(h + 1) * HEAD_DIM]
            for c in range(2):
                idx = 2 * h + c
                cs = slice(idx * DIFF_SUB, (idx + 1) * DIFF_SUB)
                s = jnp.where(mask, _dot_nt(q_ref[:, cs], k_ref[:, cs]) * scale, NEG)
                m_prev = m_sc[idx]
                m_new = jnp.maximum(m_prev, jnp.max(s, axis=-1, keepdims=True))
                alpha = jnp.exp(m_prev - m_new)
                p = jnp.exp(s - _rep_lanes(m_new, tk))
                l_sc[idx] = alpha * l_sc[idx] + jnp.sum(p, axis=-1, keepdims=True)
                acc_sc[idx] = alpha[:, :HEAD_DIM] * acc_sc[idx] + _dot(p.astype(BF16), vh)
                m_sc[idx] = m_new

    @pl.when(ki == nk - 1)
    def _():
        lm = lam_ref[...]
        lam = (jnp.exp(jnp.sum(lm[0:1] * lm[1:2], axis=-1, keepdims=True))
               - jnp.exp(jnp.sum(lm[2:3] * lm[3:4], axis=-1, keepdims=True)) + lambda_init)
        outs = []
        for h in range(N_HEADS):
            o = (acc_sc[2 * h] / l_sc[2 * h][:, :HEAD_DIM]
                 - lam * (acc_sc[2 * h + 1] / l_sc[2 * h + 1][:, :HEAD_DIM]))
            o = o * lax.rsqrt(jnp.mean(o * o, axis=-1, keepdims=True) + EPS)
            outs.append(o * gain_ref[...] * (1.0 - lambda_init))
        o_ref[...] = jnp.concatenate(outs, axis=1).astype(o_ref.dtype)


def _diff_attention(qk, v, lam, norm_g, lambda_init, B, S, tq=256, tk=512):
    tq, tk = min(tq, S), min(tk, S)
    nq, nk = S // tq, S // tk
    lam_pad = jnp.zeros((4, LANES), F32).at[:, :DIFF_SUB].set(lam.astype(F32))
    kcl = lambda col: (lambda b, i, k: (b * nk + jnp.minimum(k, (i * tq + tq - 1) // tk), col))
    return pl.pallas_call(
        functools.partial(_diff_kernel, tq=tq, tk=tk, nk=nk, lambda_init=lambda_init),
        out_shape=jax.ShapeDtypeStruct((B * S, GROUP_WIDTH), BF16),
        grid=(B, nq, nk),
        in_specs=[pl.BlockSpec((tq, GROUP_WIDTH), lambda b, i, k: (b * nq + i, 0)),
                  pl.BlockSpec((tk, GROUP_WIDTH), kcl(1)),
                  pl.BlockSpec((tk, GROUP_WIDTH), kcl(0)),
                  pl.BlockSpec((4, LANES), lambda b, i, k: (0, 0)),
                  pl.BlockSpec((1, HEAD_DIM), lambda b, i, k: (0, 0))],
        out_specs=pl.BlockSpec((tq, GROUP_WIDTH), lambda b, i, k: (b * nq + i, 0)),
        scratch_shapes=[pltpu.VMEM((2 * N_HEADS, tq, LANES), F32),
                        pltpu.VMEM((2 * N_HEADS, tq, LANES), F32),
                        pltpu.VMEM((2 * N_HEADS, tq, HEAD_DIM), F32)],
        compiler_params=_params("parallel", "parallel", "arbitrary"),
        name="diff_attention",
    )(qk, qk, v, lam_pad, norm_g.reshape(1, HEAD_DIM).astype(F32))


def _unit_lower_inverse(a, eye):
    p = -a
    x = eye + p
    for _ in range(5):
        p = _dot(p, p, HI)
        x = x + _dot(x, p, HI)
    return x


def _gdn_kernel(x_ref, z_ref, sm_ref, cw_ref, alog_ref, dtb_ref, ng_ref, gs_ref, ea_ref, eb_ref, tri_ref, sel0_ref,
                o_ref, xs_sc, st_sc, *, tt):
    kconv = cw_ref.shape[0]

    @pl.when(pl.program_id(1) == 0)
    def _():
        xs_sc[0:8, :] = jnp.zeros((8, xs_sc.shape[1]), F32)
        st_sc[...] = jnp.zeros(st_sc.shape, F32)

    xs_sc[8:8 + tt, :] = x_ref[...]
    conv = cw_ref[0:1, :] * xs_sc[pl.ds(8 - (kconv - 1), tt), :]
    for j in range(1, kconv):
        conv = conv + cw_ref[j:j + 1, :] * xs_sc[pl.ds(8 - (kconv - 1) + j, tt), :]
    xs_sc[0:8, :] = x_ref[tt - 8:tt, :]
    qkv = jax.nn.silu(conv)

    def l2n(a):
        out = []
        for c in range(GROUP_WIDTH // LANES):
            xc = a[:, c * LANES:(c + 1) * LANES]
            out.append(xc * lax.rsqrt(_dot(xc * xc, gs_ref[...], HI) + EPS))
        return jnp.concatenate(out, axis=1)

    q = l2n(qkv[:, 0:GROUP_WIDTH]) * HEAD_DIM ** -0.5
    k = l2n(qkv[:, GROUP_WIDTH:2 * GROUP_WIDTH])
    v = qkv[:, 2 * GROUP_WIDTH:]
    sm = sm_ref[...]
    g_all = -jnp.exp(alog_ref[...]) * jax.nn.softplus(sm + dtb_ref[...])
    gx = _dot(g_all, ea_ref[...], HI)
    bx = _dot(jax.nn.sigmoid(sm), eb_ref[...], HI)
    ii = lax.broadcasted_iota(jnp.int32, (CHUNK, CHUNK), 0)
    jj = lax.broadcasted_iota(jnp.int32, (CHUNK, CHUNK), 1)
    eye = jnp.where(ii == jj, 1.0, 0.0)
    outs = []
    for c in range(tt // CHUNK):
        r = slice(c * CHUNK, (c + 1) * CHUNK)
        gcx = _dot(tri_ref[...], gx[r], HI)
        egc = jnp.exp(gcx)
        glast = gcx[CHUNK - 1:CHUNK, :]
        eglast = jnp.exp(glast)
        kc, qc, bc = k[r], q[r], bx[r]
        kb = kc * bc
        vb = v[r] * bc
        kbe = kb * egc
        qd = qc * egc
        kend = kc * jnp.exp(glast - gcx)
        heads = []
        for h in range(N_HEADS):
            hs = slice(h * HEAD_DIM, (h + 1) * HEAD_DIM)
            cg = gcx[:, hs]
            rg = _dot_nt(sel0_ref[...], cg, HI)
            dec = jnp.exp(jnp.where(ii >= jj, cg - rg, NEG))
            a_low = jnp.where(ii > jj, _dot_nt(kb[:, hs], kc[:, hs], HI) * dec, 0.0)
            tm = _unit_lower_inverse(a_low, eye)
            sol = _dot(tm, jnp.concatenate([vb[:, hs], kbe[:, hs]], axis=1), HI)
            u, w = sol[:, :HEAD_DIM], sol[:, HEAD_DIM:]
            attn = _dot_nt(qc[:, hs], kc[:, hs], HI) * dec
            state = st_sc[h]
            v_new = u - _dot(w, state, HI)
            heads.append(_dot(qd[:, hs], state, HI) + _dot(attn, v_new, HI))
            st_sc[h] = state * eglast[:, hs] + _dot_tn(kend[:, hs], v_new, HI)
        outs.append(jnp.concatenate(heads, axis=1))
    o = jnp.concatenate(outs, axis=0)
    normed = []
    for c in range(GROUP_WIDTH // LANES):
        oc = o[:, c * LANES:(c + 1) * LANES]
        normed.append(oc * lax.rsqrt(_dot(oc * oc, gs_ref[...], HI) * (1.0 / HEAD_DIM) + EPS))
    o = jnp.concatenate(normed, axis=1) * ng_ref[...]
    o_ref[...] = (o * jax.nn.silu(z_ref[...])).astype(o_ref.dtype)


def _gdn(b_qkv, b_z, small, conv_w, a_log, dt_bias, norm_g, B, S, tt=256):
    tt = min(tt, S)
    ns = S // tt
    row = lambda b, i: (b * ns + i, 0)
    const = lambda b, i: (0, 0)
    tri = jnp.asarray(np.tril(np.ones((CHUNK, CHUNK), np.float32)))
    sel0 = jnp.asarray(np.eye(CHUNK, dtype=np.float32)[0:1].repeat(CHUNK, axis=0))
    return pl.pallas_call(
        functools.partial(_gdn_kernel, tt=tt),
        out_shape=jax.ShapeDtypeStruct((B * S, GROUP_WIDTH), BF16),
        grid=(B, ns),
        in_specs=[pl.BlockSpec((tt, 3 * GROUP_WIDTH), row),
                  pl.BlockSpec((tt, GROUP_WIDTH), row),
                  pl.BlockSpec((tt, LANES), row),
                  pl.BlockSpec(conv_w.shape, const),
                  pl.BlockSpec((1, LANES), const),
                  pl.BlockSpec((1, LANES), const),
                  pl.BlockSpec((1, GROUP_WIDTH), const),
                  pl.BlockSpec((LANES, LANES), const),
                  pl.BlockSpec((LANES, GROUP_WIDTH), const),
                  pl.BlockSpec((LANES, GROUP_WIDTH), const),
                  pl.BlockSpec((CHUNK, CHUNK), const),
                  pl.BlockSpec((CHUNK, CHUNK), const)],
        out_specs=pl.BlockSpec((tt, GROUP_WIDTH), row),
        scratch_shapes=[pltpu.VMEM((tt + 8, 3 * GROUP_WIDTH), F32),
                        pltpu.VMEM((N_HEADS, HEAD_DIM, HEAD_DIM), F32)],
        compiler_params=_params("parallel", "arbitrary"),
        name="gated_deltanet",
    )(b_qkv, b_z, small, conv_w.astype(F32), _row128(a_log, _LANE_GDN_A), _row128(dt_bias, _LANE_GDN_A),
      jnp.tile(norm_g.astype(F32), N_HEADS).reshape(1, GROUP_WIDTH), _group_matrix(LANES, HEAD_DIM, False),
      _expand_matrix(_LANE_GDN_A, N_HEADS, HEAD_DIM), _expand_matrix(_LANE_GDN_B, N_HEADS, HEAD_DIM), tri, sel0)


def _mlstm_kernel(qk_ref, v_ref, op_ref, sm_ref, ib_ref, fb_ref, ng_ref, gs_ref, ei_ref, ef_ref, tri_ref, sel0_ref,
                  o_ref, c_sc, m_sc, *, tt):
    @pl.when(pl.program_id(1) == 0)
    def _():
        c_sc[...] = jnp.zeros(c_sc.shape, F32)
        m_sc[...] = jnp.zeros(m_sc.shape, F32)

    sm = sm_ref[...]
    ig = GATE_CAP * jnp.tanh((sm + ib_ref[...]) * (1.0 / GATE_CAP))
    lf = jax.nn.log_sigmoid(GATE_CAP * jnp.tanh((sm + fb_ref[...]) * (1.0 / GATE_CAP)))
    ix = _dot(ig, ei_ref[...], HI)
    fx = _dot(lf, ef_ref[...], HI)
    nqk = N_HEADS * MLSTM_QK
    q = qk_ref[:, 0:nqk]
    k = qk_ref[:, nqk:2 * nqk] * MLSTM_QK ** -0.5
    v = v_ref[...]
    ii = lax.broadcasted_iota(jnp.int32, (CHUNK, CHUNK), 0)
    jj = lax.broadcasted_iota(jnp.int32, (CHUNK, CHUNK), 1)
    one_col = jnp.where(jj == 0, 1.0, 0.0)
    outs = []
    for c in range(tt // CHUNK):
        r = slice(c * CHUNK, (c + 1) * CHUNK)
        bx = _dot(tri_ref[...], fx[r], HI)
        ax = ix[r]
        heads = []
        for h in range(N_HEADS):
            hl = slice(h * LANES, (h + 1) * LANES)
            cb = bx[:, hl]
            rowv = cb - ax[:, hl]
            dlog = jnp.where(ii >= jj, cb[:, :CHUNK] - _dot_nt(sel0_ref[...], rowv, HI), NEG)
            blast = cb[CHUNK - 1:CHUNK, :]
            aend = blast - rowv
            mloc = jnp.max(aend, axis=0, keepdims=True)
            wend = jnp.exp(aend - mloc)
            qh = q[r, h * MLSTM_QK:(h + 1) * MLSTM_QK]
            kh = k[r, h * MLSTM_QK:(h + 1) * MLSTM_QK]
            vaug = jnp.concatenate([v[r, h * HEAD_DIM:(h + 1) * HEAD_DIM], one_col], axis=1)
            cloc = _dot_tn(kh * wend[:, :MLSTM_QK], vaug, HI)
            c_in = c_sc[h]
            m_in = m_sc[h][0:1, :]
            inter = cb + m_in
            mt = jnp.maximum(inter, jnp.max(dlog, axis=-1, keepdims=True))
            winter = jnp.exp(inter - mt)
            wintra = jnp.exp(dlog - mt[:, :CHUNK]) * _dot_nt(qh, kh, HI)
            numden = winter * _dot(qh, c_in, HI) + _dot(wintra, vaug, HI)
            den = numden[:, HEAD_DIM:HEAD_DIM + 1]
            heads.append(numden[:, :HEAD_DIM] / jnp.maximum(jnp.abs(den), jnp.exp(-mt[:, 0:1])))
            m_new = jnp.maximum(blast + m_in, mloc)
            c_sc[h] = jnp.exp(blast + m_in - m_new) * c_in + jnp.exp(mloc - m_new) * cloc
            m_sc[h] = jnp.broadcast_to(m_new, (8, LANES))
        outs.append(jnp.concatenate(heads, axis=1))
    o = jnp.concatenate(outs, axis=0)
    normed = []
    for c in range(GROUP_WIDTH // LANES):
        oc = o[:, c * LANES:(c + 1) * LANES]
        normed.append(oc * lax.rsqrt(_dot(oc * oc, gs_ref[...], HI) + EPS))
    o = jnp.concatenate(normed, axis=1) * ng_ref[...]
    o_ref[...] = (o * jax.nn.sigmoid(op_ref[...])).astype(o_ref.dtype)


def _mlstm(c_qk, c_v, c_o, small, i_bias, f_bias, norm_g, B, S, tt=256):
    tt = min(tt, S)
    ns = S // tt
    row = lambda b, i: (b * ns + i, 0)
    const = lambda b, i: (0, 0)
    tri = jnp.asarray(np.tril(np.ones((CHUNK, CHUNK), np.float32)))
    sel0 = jnp.asarray(np.eye(LANES, dtype=np.float32)[0:1].repeat(CHUNK, axis=0))
    return pl.pallas_call(
        functools.partial(_mlstm_kernel, tt=tt),
        out_shape=jax.ShapeDtypeStruct((B * S, GROUP_WIDTH), BF16),
        grid=(B, ns),
        in_specs=[pl.BlockSpec((tt, GROUP_WIDTH), row),
                  pl.BlockSpec((tt, GROUP_WIDTH), row),
                  pl.BlockSpec((tt, GROUP_WIDTH), row),
                  pl.BlockSpec((tt, LANES), row),
                  pl.BlockSpec((1, LANES), const),
                  pl.BlockSpec((1, LANES), const),
                  pl.BlockSpec((1, GROUP_WIDTH), const),
                  pl.BlockSpec((LANES, LANES), const),
                  pl.BlockSpec((LANES, N_HEADS * LANES), const),
                  pl.BlockSpec((LANES, N_HEADS * LANES), const),
                  pl.BlockSpec((CHUNK, CHUNK), const),
                  pl.BlockSpec((CHUNK, LANES), const)],
        out_specs=pl.BlockSpec((tt, GROUP_WIDTH), row),
        scratch_shapes=[pltpu.VMEM((N_HEADS, MLSTM_QK, LANES), F32),
                        pltpu.VMEM((N_HEADS, 8, LANES), F32)],
        compiler_params=_params("parallel", "arbitrary"),
        name="mlstm",
    )(c_qk, c_v, c_o, small, _row128(i_bias, _LANE_I), _row128(f_bias, _LANE_F),
      jnp.tile(norm_g.astype(F32), N_HEADS).reshape(1, GROUP_WIDTH), _group_matrix(LANES, HEAD_DIM, True),
      _expand_matrix(_LANE_I, N_HEADS, LANES), _expand_matrix(_LANE_F, N_HEADS, LANES), tri, sel0)


def _memkv_kernel(mem_ref, ln_ref, wkv_ref, gk_ref, gm_ref, k_ref, v_ref):
    x = mem_ref[0]
    xn = (x * lax.rsqrt(jnp.mean(x * x, axis=-1, keepdims=True) + EPS) * ln_ref[...]).astype(BF16)
    kv = _dot(xn, wkv_ref[...])
    for c in range(GROUP_WIDTH // LANES):
        cs = slice(c * LANES, (c + 1) * LANES)
        kc = kv[:, cs]
        ms = _dot(kc * kc, gm_ref[...], HI)
        k_ref[0, :, cs] = (kc * lax.rsqrt(ms + EPS) * gk_ref[:, cs]).astype(BF16)
    v_ref[0] = kv[:, GROUP_WIDTH:].astype(BF16)


def _memkv(mem, ln_mem, wkv, gain_k):
    B, M, _ = mem.shape
    const = lambda b: (0, 0)
    return pl.pallas_call(
        _memkv_kernel,
        out_shape=[jax.ShapeDtypeStruct((B, M, GROUP_WIDTH), BF16)] * 2,
        grid=(B,),
        in_specs=[pl.BlockSpec((1, M, D_MODEL), lambda b: (b, 0, 0)),
                  pl.BlockSpec((1, D_MODEL), const),
                  pl.BlockSpec((D_MODEL, 2 * GROUP_WIDTH), const),
                  pl.BlockSpec((1, GROUP_WIDTH), const),
                  pl.BlockSpec((LANES, LANES), const)],
        out_specs=[pl.BlockSpec((1, M, GROUP_WIDTH), lambda b: (b, 0, 0))] * 2,
        compiler_params=_params("parallel"),
        name="memory_kv",
    )(mem, ln_mem.reshape(1, D_MODEL), wkv.astype(BF16),
      jnp.tile(gain_k.astype(F32), N_HEADS).reshape(1, GROUP_WIDTH), _group_matrix(LANES, HEAD_DIM, True))


def _out_xattn_kernel(ya_ref, yb_ref, yc_ref, yd_ref, h_ref, wout_ref, lnx_ref, wq_ref, kx_ref, vx_ref, gq_ref, gm_ref,
                      wo_ref, o_ref):
    y = jnp.concatenate([ya_ref[...], yb_ref[...], yc_ref[...], yd_ref[...]], axis=1)
    h1 = h_ref[...] + _dot(y, wout_ref[...])
    hn = (h1 * lax.rsqrt(jnp.mean(h1 * h1, axis=-1, keepdims=True) + EPS) * lnx_ref[...]).astype(BF16)
    q = _dot(hn, wq_ref[...])
    chunks = []
    for c in range(GROUP_WIDTH // LANES):
        cs = slice(c * LANES, (c + 1) * LANES)
        qc = q[:, cs]
        ms = _dot(qc * qc, gm_ref[...], HI)
        chunks.append((qc * lax.rsqrt(ms + EPS) * gq_ref[:, cs]).astype(BF16))
    qn = jnp.concatenate(chunks, axis=1)
    kx, vx = kx_ref[0], vx_ref[0]
    outs = []
    for h in range(N_HEADS):
        hs = slice(h * HEAD_DIM, (h + 1) * HEAD_DIM)
        s = _dot_nt(qn[:, hs], kx[:, hs]) * HEAD_DIM ** -0.5
        e = jnp.exp(s - jnp.max(s, axis=-1, keepdims=True))
        p = e / jnp.sum(e, axis=-1, keepdims=True)
        outs.append(_dot(p.astype(BF16), vx[:, hs]))
    o = jnp.concatenate(outs, axis=1).astype(BF16)
    o_ref[...] = h1 + _dot(o, wo_ref[...])


def _out_xattn(ys, h2d, w_out, ln_x, wq, kx, vx, gain_q, wo, B, S, tm=512):
    tm = min(tm, S)
    ns = S // tm
    M = kx.shape[1]
    row = lambda b, i: (b * ns + i, 0)
    const = lambda b, i: (0, 0)
    return pl.pallas_call(
        _out_xattn_kernel,
        out_shape=jax.ShapeDtypeStruct((B * S, D_MODEL), F32),
        grid=(B, ns),
        in_specs=[pl.BlockSpec((tm, GROUP_WIDTH), row)] * 4 + [
            pl.BlockSpec((tm, D_MODEL), row),
            pl.BlockSpec((D_MODEL, D_MODEL), const),
            pl.BlockSpec((1, D_MODEL), const),
            pl.BlockSpec((D_MODEL, GROUP_WIDTH), const),
            pl.BlockSpec((1, M, GROUP_WIDTH), lambda b, i: (b, 0, 0)),
            pl.BlockSpec((1, M, GROUP_WIDTH), lambda b, i: (b, 0, 0)),
            pl.BlockSpec((1, GROUP_WIDTH), const),
            pl.BlockSpec((LANES, LANES), const),
            pl.BlockSpec((GROUP_WIDTH, D_MODEL), const)],
        out_specs=pl.BlockSpec((tm, D_MODEL), row),
        compiler_params=_params("parallel", "parallel"),
        name="outproj_xattn",
    )(*ys, h2d, w_out.astype(BF16), ln_x.reshape(1, D_MODEL), wq.astype(BF16), kx, vx,
      jnp.tile(gain_q.astype(F32), N_HEADS).reshape(1, GROUP_WIDTH), _group_matrix(LANES, HEAD_DIM, True),
      wo.astype(BF16))


def _moe_kernel(h_ref, ln_ref, wr_ref, br_ref, w1_ref, w3_ref, w2_ref, o_ref, hn_sc, comb_sc, acc_sc):
    e = pl.program_id(1)
    tm = h_ref.shape[0]
    lane = lax.broadcasted_iota(jnp.int32, (tm, LANES), 1)

    @pl.when(e == 0)
    def _():
        x = h_ref[...]
        hn = x * lax.rsqrt(jnp.mean(x * x, axis=-1, keepdims=True) + EPS) * ln_ref[...]
        hn_sc[...] = hn.astype(BF16)
        logits = _dot(hn, wr_ref[...], HI) + br_ref[...]
        lanef = lane.astype(F32)
        big = 1e4
        isg = (lane >= MOE_EXPERTS) & (lane < MOE_EXPERTS + MOE_GROUPS)
        lg = jnp.where(isg, logits, NEG)
        gmax = jnp.max(lg, axis=-1, keepdims=True)
        grp_p = 1.0 / jnp.sum(jnp.exp(lg - gmax), axis=-1, keepdims=True)
        gidx = jnp.min(jnp.where(lg == gmax, lanef, big), axis=-1, keepdims=True) - MOE_EXPERTS
        ing = (lane < MOE_EXPERTS) & ((lane >> 3).astype(F32) == gidx)
        le = jnp.where(ing, logits, NEG)
        m1 = jnp.max(le, axis=-1, keepdims=True)
        z = jnp.sum(jnp.where(ing, jnp.exp(le - m1), 0.0), axis=-1, keepdims=True)
        i1 = jnp.min(jnp.where(le == m1, lanef, big), axis=-1, keepdims=True)
        oh1 = lanef == i1
        le2 = jnp.where(oh1, NEG, le)
        m2 = jnp.max(le2, axis=-1, keepdims=True)
        i2 = jnp.min(jnp.where((le2 == m2) & ing, jnp.where(oh1, big, lanef), big), axis=-1, keepdims=True)
        oh2 = lanef == i2
        p1 = 1.0 / z
        p2 = jnp.exp(m2 - m1) / z
        tot = p1 + p2
        comb_sc[...] = jnp.where(oh1, p1 / tot * grp_p, 0.0) + jnp.where(oh2, p2 / tot * grp_p, 0.0)
        acc_sc[...] = jnp.zeros(acc_sc.shape, F32)

    x = hn_sc[...]
    hg = _dot(x, w1_ref[0])
    hu = _dot(x, w3_ref[0])
    ce = jnp.sum(jnp.where(lane == e, comb_sc[...], 0.0), axis=-1, keepdims=True)
    act = (jax.nn.silu(hg) * hu * ce).astype(BF16)
    acc_sc[...] += _dot(act, w2_ref[0])

    @pl.when(e == MOE_EXPERTS - 1)
    def _():
        o_ref[...] = h_ref[...] + acc_sc[...]


def _moe(h2d, ln, w_group, b_group, w_expert, b_expert, w1, w3, w2, tm=1024):
    T = h2d.shape[0]
    tm = min(tm, T)
    pad = LANES - MOE_EXPERTS - MOE_GROUPS
    wr = jnp.concatenate([w_expert, w_group, jnp.zeros((D_MODEL, pad), F32)], axis=1)
    br = jnp.concatenate([b_expert, b_group, jnp.zeros((pad,), F32)]).reshape(1, LANES)
    const = lambda i, e: (0, 0)
    return pl.pallas_call(
        _moe_kernel,
        out_shape=jax.ShapeDtypeStruct((T, D_MODEL), F32),
        grid=(T // tm, MOE_EXPERTS),
        in_specs=[pl.BlockSpec((tm, D_MODEL), lambda i, e: (i, 0)),
                  pl.BlockSpec((1, D_MODEL), const),
                  pl.BlockSpec((D_MODEL, LANES), const),
                  pl.BlockSpec((1, LANES), const),
                  pl.BlockSpec((1, D_MODEL, MOE_FF), lambda i, e: (e, 0, 0)),
                  pl.BlockSpec((1, D_MODEL, MOE_FF), lambda i, e: (e, 0, 0)),
                  pl.BlockSpec((1, MOE_FF, D_MODEL), lambda i, e: (e, 0, 0))],
        out_specs=pl.BlockSpec((tm, D_MODEL), lambda i, e: (i, 0)),
        scratch_shapes=[pltpu.VMEM((tm, D_MODEL), BF16),
                        pltpu.VMEM((tm, LANES), F32),
                        pltpu.VMEM((tm, D_MODEL), F32)],
        compiler_params=_params("parallel", "arbitrary"),
        name="hier_moe",
    )(h2d, ln.reshape(1, D_MODEL), wr, br, w1.astype(BF16), w3.astype(BF16), w2.astype(BF16))


def _nsa_mixer(a_q, a_kv, small, qk_gain, cmp_pe, cmp_w1, cmp_w2, cos, sin, B, S):
    nc = S // NSA_CMP_STRIDE
    wide = NSA_CMP_STRIDE * HEAD_DIM
    gain_k = jnp.concatenate([qk_gain[2], qk_gain[3]]).reshape(1, LANES).astype(F32)
    kn = _normrope(a_kv, gain_k, HEAD_DIM, cos, sin, LANES, B, S)
    vv = a_kv[:, LANES:2 * LANES].astype(BF16)
    kraw = a_kv[:, 2 * LANES:2 * LANES + HEAD_DIM].reshape(B, nc, wide)
    vraw = a_kv[:, 2 * LANES + HEAD_DIM:].reshape(B, nc, wide)
    cend = np.minimum(NSA_CMP_STRIDE * np.arange(nc) + NSA_CMP_LEN - 1, S - 1)
    half = HEAD_DIM // 2
    kc, vc = _nsa_compress(kraw, vraw, cmp_pe.reshape(2, 2 * wide), cmp_w1, cmp_w2, qk_gain[1],
                           cos[cend, :half], sin[cend, half:2 * half], B, nc)
    gq_row = jnp.tile(qk_gain[0].astype(F32), N_HEADS).reshape(1, GROUP_WIDTH)
    qn, part, sel = _nsa1(a_q, small, kc, vc, kn, vv, cos, sin, gq_row, B, S)
    return _nsa2(qn, sel, kn, vv, part, small, B, S)


def kernel(x, mem, ln_mix, w_in, w_out, nsa_qk_gain, nsa_cmp_pe, nsa_cmp_w1, nsa_cmp_w2, gdn_conv, gdn_a_log, gdn_dt_bias, gdn_norm, mlstm_i_bias, mlstm_f_bias, mlstm_norm, diff_qk_gain, diff_lambda, diff_norm, ln_xattn, ln_mem, xattn_wq, xattn_wkv, xattn_qk_gain, xattn_wo, ln_moe, moe_w_group, moe_b_group, moe_w_expert, moe_b_expert, moe_w1, moe_w3, moe_w2):
    B, S, D = x.shape
    depth = w_in.shape[0]
    cos_a, sin_a = _rope_tables(S, HEAD_DIM)
    cos_d, sin_d = _rope_tables(S, DIFF_SUB)
    h = x.reshape(B * S, D)
    for l in range(depth):
        a_q, a_kv, b_qkv, b_z, c_qk, c_v, c_o, d_qk, d_v, small = _inproj(h, ln_mix[l], w_in[l])
        y_a = _nsa_mixer(a_q, a_kv, small, nsa_qk_gain[l], nsa_cmp_pe[l], nsa_cmp_w1[l], nsa_cmp_w2[l],
                         cos_a, sin_a, B, S)
        y_b = _gdn(b_qkv, b_z, small, gdn_conv[l], gdn_a_log[l], gdn_dt_bias[l], gdn_norm[l], B, S)
        y_c = _mlstm(c_qk, c_v, c_o, small, mlstm_i_bias[l], mlstm_f_bias[l], mlstm_norm[l], B, S)
        lambda_init = 0.8 - 0.6 * math.exp(-0.3 * l)
        gain_d = jnp.concatenate([jnp.tile(diff_qk_gain[l, 0], 2 * N_HEADS),
                                  jnp.tile(diff_qk_gain[l, 1], 2 * N_HEADS)]).reshape(1, 2 * GROUP_WIDTH).astype(F32)
        dqk = _normrope(d_qk, gain_d, DIFF_SUB, cos_d, sin_d, 2 * GROUP_WIDTH, B, S)
        y_d = _diff_attention(dqk, d_v, diff_lambda[l], diff_norm[l], lambda_init, B, S)
        kx, vx = _memkv(mem, ln_mem[l], xattn_wkv[l], xattn_qk_gain[l, 1])
        h = _out_xattn((y_a, y_b, y_c, y_d), h, w_out[l], ln_xattn[l], xattn_wq[l], kx, vx,
                       xattn_qk_gain[l, 0], xattn_wo[l], B, S)
        h = _moe(h, ln_moe[l], moe_w_group[l], moe_b_group[l], moe_w_expert[l], moe_b_expert[l],
                 moe_w1[l], moe_w3[l], moe_w2[l])
    return h.reshape(B, S, D)
```

```python
import functools
import math

import numpy as np
import jax
import jax.numpy as jnp
from jax import lax
from jax.experimental import pallas as pl
from jax.experimental.pallas import tpu as pltpu

F32 = jnp.float32
BF16 = jnp.bfloat16
HI = lax.Precision.HIGHEST

D_MODEL = 1024
HEAD_DIM = 64
N_HEADS = 4
GROUP_WIDTH = 256
ROPE_THETA = 10000.0
EPS = 1e-6
NEG = -1e30

NSA_CMP_LEN = 32
NSA_CMP_STRIDE = 16
NSA_SEL_LEN = 64
NSA_TOP_N = 16
NSA_WINDOW = 512
NSA_FORCE_BONUS = 1e3
CHUNK = 64
MLSTM_QK = 32
GATE_CAP = 15.0
DIFF_SUB = 32
MOE_GROUPS = 4
MOE_PER_GROUP = 8
MOE_EXPERTS = 32
MOE_FF = 256
LANES = 128
VMEM_LIMIT = 48 * 1024 * 1024

_A0, _B0, _C0, _D0 = 0, 652, 1684, 2460
_GROUPS = (
    ("a_q", ((_A0, 256),), F32),
    ("a_kv", ((_A0 + 384, 64), (_A0 + 512, 64), (_A0 + 448, 64), (None, 64), (_A0 + 576, 64), (None, 64),
              (_A0 + 256, 64), (_A0 + 320, 64)), F32),
    ("b_qkv", ((_B0, 768),), F32),
    ("b_z", ((_B0 + 776, 256),), F32),
    ("c_qk", ((_C0, 256),), F32),
    ("c_v", ((_C0 + 256, 256),), F32),
    ("c_o", ((_C0 + 520, 256),), F32),
    ("d_qk", ((_D0, 512),), F32),
    ("d_v", tuple(p for h in range(N_HEADS) for p in ((_D0 + 512 + 64 * h, 64), (None, 64))), BF16),
    ("small", ((_A0 + 640, 12), (_B0 + 768, 4), (_B0 + 772, 4), (_C0 + 512, 4), (_C0 + 516, 4), (None, 100)), F32),
)
_LANE_GDN_A, _LANE_GDN_B, _LANE_I, _LANE_F = 12, 16, 20, 24


def _dot(a, b, prec=None):
    return jnp.dot(a, b, preferred_element_type=F32, precision=prec)


def _dot_nt(a, b, prec=None):
    return lax.dot_general(a, b, (((1,), (1,)), ((), ())), preferred_element_type=F32, precision=prec)


def _dot_tn(a, b, prec=None):
    return lax.dot_general(a, b, (((0,), (0,)), ((), ())), preferred_element_type=F32, precision=prec)


def _bdot(a, b):
    return _dot(a.astype(BF16), b.astype(BF16))


def _bdot_nt(a, b):
    return _dot_nt(a.astype(BF16), b.astype(BF16))


def _bdot_tn(a, b):
    return _dot_tn(a.astype(BF16), b.astype(BF16))


def _split3(a):
    hi = a.astype(BF16)
    r = a - hi.astype(F32)
    mid = r.astype(BF16)
    return hi, mid, (r - mid.astype(F32)).astype(BF16)


def _dot_c(a, c):
    hi, mid, lo = _split3(a)
    return _dot(hi, c) + _dot(mid, c) + _dot(lo, c)


def _c_dot(c, b):
    hi, mid, lo = _split3(b)
    return _dot(c, hi) + _dot(c, mid) + _dot(c, lo)


def _dot_nt_x3(a, b):
    ah, bh = a.astype(BF16), b.astype(BF16)
    al, bl = (a - ah.astype(F32)).astype(BF16), (b - bh.astype(F32)).astype(BF16)
    return _dot_nt(ah, bh) + _dot_nt(ah, bl) + _dot_nt(al, bh)


def _params(*sem):
    return pltpu.CompilerParams(dimension_semantics=sem, vmem_limit_bytes=VMEM_LIMIT)


def _group_matrix(width, gsz, mean):
    g = np.kron(np.eye(width // gsz), np.ones((gsz, gsz)))
    return jnp.asarray(g / gsz if mean else g, BF16)


def _expand_matrix(src_lane0, n, out_per, stride=1):
    e = np.zeros((LANES, n * out_per), np.float32)
    for h in range(n):
        e[src_lane0 + stride * h, h * out_per:(h + 1) * out_per] = 1.0
    return jnp.asarray(e, BF16)


def _row128(vals, lane0):
    return jnp.zeros((1, LANES), F32).at[0, lane0:lane0 + vals.shape[0]].set(vals.astype(F32))


def _ones_col_row():
    return jnp.zeros((1, LANES), BF16).at[0, HEAD_DIM].set(1.0)


def _rope_tables(pos, dim):
    inv = 1.0 / (ROPE_THETA ** (jnp.arange(0, dim, 2, dtype=F32) / dim))
    ang = jnp.asarray(pos).astype(F32)[:, None] * inv[None, :]
    cos, sin = jnp.cos(ang), jnp.sin(ang)
    cosd = jnp.concatenate([cos, cos], axis=-1)
    sind = jnp.concatenate([-sin, sin], axis=-1)
    rep = LANES // dim
    return jnp.tile(cosd, (1, rep)), jnp.tile(sind, (1, rep))


def _rope128(x, cos, sin_signed, half):
    left = pltpu.roll(x, LANES - half, 1)
    right = pltpu.roll(x, half, 1)
    lane = lax.broadcasted_iota(jnp.int32, x.shape, 1)
    first = (lane & (2 * half - 1)) < half
    return x * cos + jnp.where(first, left, right) * sin_signed


def _rep_lanes(x, width):
    return x if width == LANES else jnp.concatenate([x] * (width // LANES), axis=1)


def _softmax_rows(s, mask):
    m = jnp.max(s, axis=-1, keepdims=True)
    e = jnp.exp(s - m)
    return jnp.where(mask, e / jnp.sum(e, axis=-1, keepdims=True), 0.0)


def _group_rms(x, gm):
    return x * lax.rsqrt(_dot_c(x * x, gm) + EPS)


def _flash_update(s, vaug, m_ref, acc_ref, idx, tk):
    m_prev = m_ref[idx]
    m_new = jnp.maximum(m_prev, jnp.max(s, axis=-1, keepdims=True))
    p = jnp.exp(s - _rep_lanes(m_new, tk)).astype(BF16)
    acc_ref[idx] = jnp.exp(m_prev - m_new) * acc_ref[idx] + _dot(p, vaug)
    m_ref[idx] = m_new


def _flash_result(acc_ref, idx):
    acc = acc_ref[idx]
    return acc[:, :HEAD_DIM] / acc[:, HEAD_DIM:HEAD_DIM + 1]


def _inproj_kernel(x_ref, g_ref, w_ref, *outs, widths):
    x = x_ref[...]
    ms = jnp.mean(x * x, axis=-1, keepdims=True)
    xn = (x * lax.rsqrt(ms + EPS) * g_ref[...]).astype(BF16)
    off = 0
    for o, wd in zip(outs, widths):
        o[...] = _dot(xn, w_ref[:, off:off + wd]).astype(o.dtype)
        off += wd


def _permute_w_in(w):
    cols, widths = [], []
    for _, parts, _ in _GROUPS:
        for s, n in parts:
            cols.append(jnp.zeros((w.shape[0], n), w.dtype) if s is None else w[:, s:s + n])
        widths.append(sum(n for _, n in parts))
    return jnp.concatenate(cols, axis=1).astype(BF16), tuple(widths)


def _inproj(h2d, gain, w_in, tm=256):
    T = h2d.shape[0]
    wp, widths = _permute_w_in(w_in)
    out_shape = [jax.ShapeDtypeStruct((T, wd), g[2]) for wd, g in zip(widths, _GROUPS)]
    return pl.pallas_call(
        functools.partial(_inproj_kernel, widths=widths),
        out_shape=out_shape,
        grid=(T // tm,),
        in_specs=[pl.BlockSpec((tm, D_MODEL), lambda i: (i, 0)),
                  pl.BlockSpec((1, D_MODEL), lambda i: (0, 0)),
                  pl.BlockSpec((D_MODEL, sum(widths)), lambda i: (0, 0))],
        out_specs=[pl.BlockSpec((tm, wd), lambda i: (i, 0)) for wd in widths],
        compiler_params=_params("parallel"),
        name="inproj",
    )(h2d, gain.reshape(1, D_MODEL), wp)


def _normrope_kernel(x_ref, gain_ref, gm_ref, cos_ref, sin_ref, o_ref, *, half, nchunk):
    for c in range(nchunk):
        cs = slice(c * LANES, (c + 1) * LANES)
        xn = _group_rms(x_ref[:, cs], gm_ref[...]) * gain_ref[:, cs]
        o_ref[:, cs] = _rope128(xn, cos_ref[...], sin_ref[...], half).astype(o_ref.dtype)


def _normrope(x, gain_row, gsz, cos, sin, width, B, S, tm=512):
    tm = min(tm, S)
    ns = S // tm
    return pl.pallas_call(
        functools.partial(_normrope_kernel, half=gsz // 2, nchunk=width // LANES),
        out_shape=jax.ShapeDtypeStruct((B * S, width), BF16),
        grid=(B, ns),
        in_specs=[pl.BlockSpec((tm, width), lambda b, i: (b * ns + i, 0)),
                  pl.BlockSpec((1, width), lambda b, i: (0, 0)),
                  pl.BlockSpec((LANES, LANES), lambda b, i: (0, 0)),
                  pl.BlockSpec((tm, LANES), lambda b, i: (i, 0)),
                  pl.BlockSpec((tm, LANES), lambda b, i: (i, 0))],
        out_specs=pl.BlockSpec((tm, width), lambda b, i: (b * ns + i, 0)),
        compiler_params=_params("parallel", "parallel"),
        name="normrope",
    )(x, gain_row, _group_matrix(LANES, gsz, True), cos, sin)


def _nsa_cmp_kernel(kr_ref, vr_ref, pe_ref, w1_ref, w2_ref, gain_ref, cos_ref, sin_ref, kc_ref, vc_ref, *, nc):
    half_in = NSA_CMP_STRIDE * HEAD_DIM

    def compress(r, j):
        a = _dot(r, w1_ref[j, :half_in, :], HI)
        b = _dot(r, w1_ref[j, half_in:, :], HI)
        pe = jnp.broadcast_to(pe_ref[j], (8, 2 * half_in))
        c = _dot(pe, w1_ref[j], HI)[0:1]
        hid = jax.nn.gelu(a + pltpu.roll(b, nc - 1, 0) + c)
        return _dot(hid, w2_ref[j], HI)

    kc = compress(kr_ref[0], 0)
    kc = kc * lax.rsqrt(jnp.mean(kc * kc, axis=-1, keepdims=True) + EPS) * gain_ref[...]
    x1, x2 = kc[:, :HEAD_DIM // 2], kc[:, HEAD_DIM // 2:]
    cos, sin = cos_ref[...], sin_ref[...]
    kc_ref[0] = jnp.concatenate([x1 * cos - x2 * sin, x2 * cos + x1 * sin], axis=1)
    vc_ref[0] = compress(vr_ref[0], 1)


def _nsa_compress(kraw, vraw, pe, w1, w2, gain, cos_c, sin_c, B, nc):
    wide = NSA_CMP_STRIDE * HEAD_DIM
    return pl.pallas_call(
        functools.partial(_nsa_cmp_kernel, nc=nc),
        out_shape=[jax.ShapeDtypeStruct((B, nc, HEAD_DIM), F32)] * 2,
        grid=(B,),
        in_specs=[pl.BlockSpec((1, nc, wide), lambda b: (b, 0, 0)),
                  pl.BlockSpec((1, nc, wide), lambda b: (b, 0, 0)),
                  pl.BlockSpec((2, 1, 2 * wide), lambda b: (0, 0, 0)),
                  pl.BlockSpec((2, 2 * wide, 2 * HEAD_DIM), lambda b: (0, 0, 0)),
                  pl.BlockSpec((2, 2 * HEAD_DIM, HEAD_DIM), lambda b: (0, 0, 0)),
                  pl.BlockSpec((1, HEAD_DIM), lambda b: (0, 0)),
                  pl.BlockSpec((nc, HEAD_DIM // 2), lambda b: (0, 0)),
                  pl.BlockSpec((nc, HEAD_DIM // 2), lambda b: (0, 0))],
        out_specs=[pl.BlockSpec((1, nc, HEAD_DIM), lambda b: (b, 0, 0))] * 2,
        compiler_params=_params("parallel"),
        name="nsa_compress",
    )(kraw, vraw, pe.reshape(2, 1, 2 * wide), w1, w2, gain.reshape(1, HEAD_DIM), cos_c, sin_c)


def _nsa1_kernel(q_ref, sm_ref, kc_ref, vc_ref, kn_ref, vw_ref, cos_ref, sin_ref, gq_ref, gm_ref, ovl_ref,
                 eg0_ref, eg2_ref, qn_ref, part_ref, sel_ref, *, tq, nc, n_sel):
    t0 = pl.program_id(1) * tq
    chunks = []
    for c in range(GROUP_WIDTH // LANES):
        cs = slice(c * LANES, (c + 1) * LANES)
        xn = _group_rms(q_ref[:, cs], gm_ref[...]) * gq_ref[:, cs]
        chunks.append(_rope128(xn, cos_ref[...], sin_ref[...], HEAD_DIM // 2))
    qs = jnp.concatenate(chunks, axis=1) * HEAD_DIM ** -0.5
    qb = qs.astype(BF16)
    qn_ref[...] = qb
    sig = jax.nn.sigmoid(sm_ref[...])
    g0x = _dot_c(sig, eg0_ref[...])
    g2x = _dot_c(sig, eg2_ref[...])
    tpos = t0 + lax.broadcasted_iota(jnp.int32, (tq, 1), 0)

    kc = kc_ref[0]
    vc = vc_ref[0].astype(BF16)
    cidx = lax.broadcasted_iota(jnp.int32, (1, nc), 1)
    cmask = ((NSA_CMP_STRIDE * cidx + NSA_CMP_LEN - 1) <= tpos) & (cidx < nc - 1)
    psum = jnp.zeros((tq, nc), F32)
    o_cmp = []
    for h in range(N_HEADS):
        hs = slice(h * HEAD_DIM, (h + 1) * HEAD_DIM)
        p = _softmax_rows(jnp.where(cmask, _dot_nt_x3(qs[:, hs], kc), NEG), cmask)
        o_cmp.append(_dot(p.astype(BF16), vc))
        psum = psum + p

    imp = _dot_c(psum, ovl_ref[...])
    j = lax.broadcasted_iota(jnp.int32, (tq, LANES), 1)
    cur = tpos >> 6
    valid = j <= cur
    forced = (j == 0) | (j == cur) | (j == cur - 1)
    score = jnp.where(valid, imp + jnp.where(forced, NSA_FORCE_BONUS, 0.0), NEG)
    st = score.T
    jj = lax.broadcasted_iota(jnp.int32, (LANES, 1), 0)
    cnt = jnp.zeros((LANES, tq), F32)
    for i in range(n_sel):
        row = st[i:i + 1, :]
        tie = jnp.where(jj > i, 1.0, 0.0)
        cnt = cnt + jnp.where(row > st, 1.0, jnp.where(row == st, tie, 0.0))
    sel_t = jnp.where((cnt < min(NSA_TOP_N, n_sel)) & (st > 0.5 * NEG), 1.0, 0.0)
    sel_ref[...] = sel_t.T.astype(BF16)

    band = tq + NSA_WINDOW
    start = pl.multiple_of(jnp.maximum(t0 - NSA_WINDOW, 0), LANES)
    kw = kn_ref[pl.ds(start, band), HEAD_DIM:2 * HEAD_DIM]
    vw = vw_ref[pl.ds(start, band), 0:HEAD_DIM].astype(BF16)
    dist = tpos - (start + lax.broadcasted_iota(jnp.int32, (1, band), 1))
    wmask = (dist >= 0) & (dist < NSA_WINDOW)
    o_win = []
    for h in range(N_HEADS):
        hs = slice(h * HEAD_DIM, (h + 1) * HEAD_DIM)
        p = _softmax_rows(jnp.where(wmask, _dot_nt(qb[:, hs], kw), NEG), wmask)
        o_win.append(_dot(p.astype(BF16), vw))
    part_ref[...] = g0x * jnp.concatenate(o_cmp, axis=1) + g2x * jnp.concatenate(o_win, axis=1)


def _nsa_overlap(nc, n_sel):
    c0 = NSA_CMP_STRIDE * np.arange(nc)[:, None]
    s0 = NSA_SEL_LEN * np.arange(n_sel)[None, :]
    ov = np.clip(np.minimum(c0 + NSA_CMP_LEN, s0 + NSA_SEL_LEN) - np.maximum(c0, s0), 0, None) / NSA_CMP_STRIDE
    ov[nc - 1:] = 0.0
    out = np.zeros((nc, LANES), np.float32)
    out[:, :n_sel] = ov
    return jnp.asarray(out, BF16)


def _nsa1(a_q, small, kc, vc, kn, a_kv, cos, sin, gq_row, B, S, tq=256):
    tq = min(tq, S)
    nq = S // tq
    nc = S // NSA_CMP_STRIDE
    n_sel = S // NSA_SEL_LEN
    row = lambda b, i: (b * nq + i, 0)
    return pl.pallas_call(
        functools.partial(_nsa1_kernel, tq=tq, nc=nc, n_sel=n_sel),
        out_shape=[jax.ShapeDtypeStruct((B * S, GROUP_WIDTH), BF16),
                   jax.ShapeDtypeStruct((B * S, GROUP_WIDTH), F32),
                   jax.ShapeDtypeStruct((B * S, LANES), BF16)],
        grid=(B, nq),
        in_specs=[pl.BlockSpec((tq, GROUP_WIDTH), row),
                  pl.BlockSpec((tq, LANES), row),
                  pl.BlockSpec((1, nc, HEAD_DIM), lambda b, i: (b, 0, 0)),
                  pl.BlockSpec((1, nc, HEAD_DIM), lambda b, i: (b, 0, 0)),
                  pl.BlockSpec((S, LANES), lambda b, i: (b, 0)),
                  pl.BlockSpec((S, LANES), lambda b, i: (b, 2)),
                  pl.BlockSpec((tq, LANES), lambda b, i: (i, 0)),
                  pl.BlockSpec((tq, LANES), lambda b, i: (i, 0)),
                  pl.BlockSpec((1, GROUP_WIDTH), lambda b, i: (0, 0)),
                  pl.BlockSpec((LANES, LANES), lambda b, i: (0, 0)),
                  pl.BlockSpec((nc, LANES), lambda b, i: (0, 0)),
                  pl.BlockSpec((LANES, GROUP_WIDTH), lambda b, i: (0, 0)),
                  pl.BlockSpec((LANES, GROUP_WIDTH), lambda b, i: (0, 0))],
        out_specs=[pl.BlockSpec((tq, GROUP_WIDTH), row),
                   pl.BlockSpec((tq, GROUP_WIDTH), row),
                   pl.BlockSpec((tq, LANES), row)],
        compiler_params=_params("parallel", "parallel"),
        name="nsa_cmp_win_select",
    )(a_q, small, kc, vc, kn, a_kv, cos, sin, gq_row, _group_matrix(LANES, HEAD_DIM, True),
      _nsa_overlap(nc, n_sel), _expand_matrix(0, N_HEADS, HEAD_DIM, 3), _expand_matrix(2, N_HEADS, HEAD_DIM, 3))


def _nsa2_kernel(qn_ref, sel_ref, e_ref, kn_ref, vs_ref, one_ref, part_ref, sm_ref, eg1_ref, o_ref, m_sc, acc_sc,
                 *, tq, tk, nk):
    qi = pl.program_id(1)
    ki = pl.program_id(2)

    @pl.when(ki == 0)
    def _():
        m_sc[...] = jnp.full(m_sc.shape, NEG, F32)
        acc_sc[...] = jnp.zeros(acc_sc.shape, F32)

    def step(causal):
        mask = _dot(sel_ref[...], e_ref[...]) > 0.5
        if causal:
            tpos = qi * tq + lax.broadcasted_iota(jnp.int32, (tq, 1), 0)
            kpos = ki * tk + lax.broadcasted_iota(jnp.int32, (1, tk), 1)
            mask = mask & (kpos <= tpos)
        ks = kn_ref[:, 0:HEAD_DIM]
        vaug = vs_ref[...].astype(BF16) + one_ref[...]
        for h in range(N_HEADS):
            s = jnp.where(mask, _dot_nt(qn_ref[:, h * HEAD_DIM:(h + 1) * HEAD_DIM], ks), NEG)
            _flash_update(s, vaug, m_sc, acc_sc, h, tk)

    @pl.when(ki * tk + tk - 1 <= qi * tq)
    def _():
        step(False)

    @pl.when((ki * tk + tk - 1 > qi * tq) & (ki * tk <= qi * tq + tq - 1))
    def _():
        step(True)

    @pl.when(ki == nk - 1)
    def _():
        g1x = _dot_c(jax.nn.sigmoid(sm_ref[...]), eg1_ref[...])
        o = jnp.concatenate([_flash_result(acc_sc, h) for h in range(N_HEADS)], axis=1)
        o_ref[...] = (part_ref[...] + g1x * o).astype(o_ref.dtype)


def _nsa2(qn, sel, kn, a_kv, part, small, B, S, tq=256, tk=512):
    tq, tk = min(tq, S), min(tk, S)
    nq, nk = S // tq, S // tk
    e = np.zeros((LANES, S), np.float32)
    e[np.arange(S) // NSA_SEL_LEN, np.arange(S)] = 1.0
    row = lambda b, i, k: (b * nq + i, 0)
    kclamp = lambda i, k: jnp.minimum(k, (i * tq + tq - 1) // tk)
    return pl.pallas_call(
        functools.partial(_nsa2_kernel, tq=tq, tk=tk, nk=nk),
        out_shape=jax.ShapeDtypeStruct((B * S, GROUP_WIDTH), BF16),
        grid=(B, nq, nk),
        in_specs=[pl.BlockSpec((tq, GROUP_WIDTH), row),
                  pl.BlockSpec((tq, LANES), row),
                  pl.BlockSpec((LANES, tk), lambda b, i, k: (0, kclamp(i, k))),
                  pl.BlockSpec((tk, LANES), lambda b, i, k: (b * nk + kclamp(i, k), 0)),
                  pl.BlockSpec((tk, LANES), lambda b, i, k: (b * nk + kclamp(i, k), 1)),
                  pl.BlockSpec((1, LANES), lambda b, i, k: (0, 0)),
                  pl.BlockSpec((tq, GROUP_WIDTH), row),
                  pl.BlockSpec((tq, LANES), row),
                  pl.BlockSpec((LANES, GROUP_WIDTH), lambda b, i, k: (0, 0))],
        out_specs=pl.BlockSpec((tq, GROUP_WIDTH), row),
        scratch_shapes=[pltpu.VMEM((N_HEADS, tq, LANES), F32),
                        pltpu.VMEM((N_HEADS, tq, LANES), F32)],
        compiler_params=_params("parallel", "parallel", "arbitrary"),
        name="nsa_selected",
    )(qn, sel, jnp.asarray(e, BF16), kn, a_kv, _ones_col_row(), part, small, _expand_matrix(1, N_HEADS, HEAD_DIM, 3))


def _diff_kernel(q_ref, k_ref, v_ref, one_ref, lam_ref, gain_ref, o_ref, m_sc, acc_sc, *, tq, tk, nk, lambda_init):
    qi = pl.program_id(1)
    ki = pl.program_id(2)

    @pl.when(ki == 0)
    def _():
        m_sc[...] = jnp.full(m_sc.shape, NEG, F32)
        acc_sc[...] = jnp.zeros(acc_sc.shape, F32)

    def step(causal):
        if causal:
            tpos = qi * tq + lax.broadcasted_iota(jnp.int32, (tq, 1), 0)
            kpos = ki * tk + lax.broadcasted_iota(jnp.int32, (1, tk), 1)
            mask = kpos <= tpos
        for h in range(N_HEADS):
            vaug = v_ref[:, h * LANES:(h + 1) * LANES] + one_ref[...]
            for c in range(2):
                idx = 2 * h + c
                cs = slice(idx * DIFF_SUB, (idx + 1) * DIFF_SUB)
                s = _dot_nt(q_ref[:, cs], k_ref[:, cs])
                if causal:
                    s = jnp.where(mask, s, NEG)
                _flash_update(s, vaug, m_sc, acc_sc, idx, tk)

    @pl.when(ki * tk + tk - 1 <= qi * tq)
    def _():
        step(False)

    @pl.when((ki * tk + tk - 1 > qi * tq) & (ki * tk <= qi * tq + tq - 1))
    def _():
        step(True)

    @pl.when(ki == nk - 1)
    def _():
        lm = lam_ref[...]
        lam = (jnp.exp(jnp.sum(lm[0:1] * lm[1:2], axis=-1, keepdims=True))
               - jnp.exp(jnp.sum(lm[2:3] * lm[3:4], axis=-1, keepdims=True)) + lambda_init)
        outs = []
        for h in range(N_HEADS):
            o = _flash_result(acc_sc, 2 * h) - lam * _flash_result(acc_sc, 2 * h + 1)
            o = o * lax.rsqrt(jnp.mean(o * o, axis=-1, keepdims=True) + EPS)
            outs.append(o * gain_ref[...] * (1.0 - lambda_init))
        o_ref[...] = jnp.concatenate(outs, axis=1).astype(o_ref.dtype)


def _diff_attention(qk, v, lam, norm_g, lambda_init, B, S, tq=512, tk=512):
    tq, tk = min(tq, S), min(tk, S)
    nq, nk = S // tq, S // tk
    lam_pad = jnp.zeros((4, LANES), F32).at[:, :DIFF_SUB].set(lam.astype(F32))
    kclamp = lambda i, k: jnp.minimum(k, (i * tq + tq - 1) // tk)
    const = lambda b, i, k: (0, 0)
    return pl.pallas_call(
        functools.partial(_diff_kernel, tq=tq, tk=tk, nk=nk, lambda_init=lambda_init),
        out_shape=jax.ShapeDtypeStruct((B * S, GROUP_WIDTH), BF16),
        grid=(B, nq, nk),
        in_specs=[pl.BlockSpec((tq, GROUP_WIDTH), lambda b, i, k: (b * nq + i, 0)),
                  pl.BlockSpec((tk, GROUP_WIDTH), lambda b, i, k: (b * nk + kclamp(i, k), 1)),
                  pl.BlockSpec((tk, N_HEADS * LANES), lambda b, i, k: (b * nk + kclamp(i, k), 0)),
                  pl.BlockSpec((1, LANES), const),
                  pl.BlockSpec((4, LANES), const),
                  pl.BlockSpec((1, HEAD_DIM), const)],
        out_specs=pl.BlockSpec((tq, GROUP_WIDTH), lambda b, i, k: (b * nq + i, 0)),
        scratch_shapes=[pltpu.VMEM((2 * N_HEADS, tq, LANES), F32),
                        pltpu.VMEM((2 * N_HEADS, tq, LANES), F32)],
        compiler_params=_params("parallel", "parallel", "arbitrary"),
        name="diff_attention",
    )(qk, qk, v, _ones_col_row(), lam_pad, norm_g.reshape(1, HEAD_DIM).astype(F32))


def _gdn_kernel(x_ref, z_ref, sm_ref, cw_ref, alog_ref, dtb_ref, ng_ref, gs_ref, ea_ref, eb_ref, tri_ref,
                o_ref, xs_sc, st_sc, *, tt):
    kconv = cw_ref.shape[0]

    @pl.when(pl.program_id(1) == 0)
    def _():
        xs_sc[0:8, :] = jnp.zeros((8, xs_sc.shape[1]), F32)
        st_sc[...] = jnp.zeros(st_sc.shape, F32)

    xs_sc[8:8 + tt, :] = x_ref[...]
    conv = cw_ref[0:1, :] * xs_sc[pl.ds(8 - (kconv - 1), tt), :]
    for j in range(1, kconv):
        conv = conv + cw_ref[j:j + 1, :] * xs_sc[pl.ds(8 - (kconv - 1) + j, tt), :]
    xs_sc[0:8, :] = x_ref[tt - 8:tt, :]
    qkv = jax.nn.silu(conv)

    def l2n(a):
        out = []
        for c in range(GROUP_WIDTH // LANES):
            xc = a[:, c * LANES:(c + 1) * LANES]
            out.append(xc * lax.rsqrt(_dot_c(xc * xc, gs_ref[...]) + EPS))
        return jnp.concatenate(out, axis=1)

    q = l2n(qkv[:, 0:GROUP_WIDTH]) * HEAD_DIM ** -0.5
    k = l2n(qkv[:, GROUP_WIDTH:2 * GROUP_WIDTH])
    v = qkv[:, 2 * GROUP_WIDTH:]
    sm = sm_ref[...]
    g_all = -jnp.exp(alog_ref[...]) * jax.nn.softplus(sm + dtb_ref[...])
    gx = _dot_c(g_all, ea_ref[...])
    bx = _dot_c(jax.nn.sigmoid(sm), eb_ref[...])
    ii = lax.broadcasted_iota(jnp.int32, (CHUNK, CHUNK), 0)
    jj = lax.broadcasted_iota(jnp.int32, (CHUNK, CHUNK), 1)
    eye = jnp.where(ii == jj, 1.0, 0.0)
    nchunk = tt // CHUNK
    heads = [slice(h * HEAD_DIM, (h + 1) * HEAD_DIM) for h in range(N_HEADS)]
    pairs, dec, rhs, qd, kend, eglast, qk_nt, kbk_nt = [], [], [], [], [], [], [], []
    for c in range(nchunk):
        r = slice(c * CHUNK, (c + 1) * CHUNK)
        gcx = _c_dot(tri_ref[...], gx[r])
        gct = gcx.T
        egc = jnp.exp(gcx)
        glast = gcx[CHUNK - 1:CHUNK, :]
        kc, qc, bc = k[r].astype(BF16), q[r], bx[r]
        kb = k[r] * bc
        vb = v[r] * bc
        kbe = kb * egc
        qd.append((qc * egc).astype(BF16))
        kend.append((k[r] * jnp.exp(glast - gcx)).astype(BF16))
        eglast.append(jnp.exp(glast))
        qcb, kbb = qc.astype(BF16), kb.astype(BF16)
        for hs in heads:
            pairs.append((c, hs))
            dec.append(jnp.exp(jnp.where(ii >= jj, gcx[:, hs] - gct[hs, :], NEG)))
            rhs.append(jnp.concatenate([vb[:, hs], kbe[:, hs]], axis=1).astype(BF16))
            kbk_nt.append(_dot_nt(kbb[:, hs], kc[:, hs]))
            qk_nt.append(_dot_nt(qcb[:, hs], kc[:, hs]))
    pw = [-jnp.where(ii > jj, a * d, 0.0) for a, d in zip(kbk_nt, dec)]
    inv = [eye + p for p in pw]
    for _ in range(5):
        pw = [_bdot(p, p) for p in pw]
        inv = [x + _bdot(x, p) for x, p in zip(inv, pw)]
    sol = [_bdot(x, b) for x, b in zip(inv, rhs)]
    attn = [(a * d).astype(BF16) for a, d in zip(qk_nt, dec)]
    state = [st_sc[h] for h in range(N_HEADS)]
    outs = []
    for c in range(nchunk):
        idx = [c * N_HEADS + h for h in range(N_HEADS)]
        sb = [s.astype(BF16) for s in state]
        ws = [_dot(sol[i][:, HEAD_DIM:].astype(BF16), sb[h]) for h, i in enumerate(idx)]
        qs = [_dot(qd[c][:, hs], sb[h]) for h, hs in enumerate(heads)]
        v_new = [(sol[i][:, :HEAD_DIM] - ws[h]).astype(BF16) for h, i in enumerate(idx)]
        outs.append(jnp.concatenate([qs[h] + _dot(attn[i], v_new[h]) for h, i in enumerate(idx)], axis=1))
        state = [state[h] * eglast[c][:, hs] + _dot_tn(kend[c][:, hs], v_new[h]) for h, hs in enumerate(heads)]
    for h in range(N_HEADS):
        st_sc[h] = state[h]
    o = jnp.concatenate(outs, axis=0)
    normed = []
    for c in range(GROUP_WIDTH // LANES):
        oc = o[:, c * LANES:(c + 1) * LANES]
        normed.append(oc * lax.rsqrt(_dot_c(oc * oc, gs_ref[...]) * (1.0 / HEAD_DIM) + EPS))
    o = jnp.concatenate(normed, axis=1) * ng_ref[...]
    o_ref[...] = (o * jax.nn.silu(z_ref[...])).astype(o_ref.dtype)


def _gdn(b_qkv, b_z, small, conv_w, a_log, dt_bias, norm_g, B, S, tt=256):
    tt = min(tt, S)
    ns = S // tt
    row = lambda b, i: (b * ns + i, 0)
    const = lambda b, i: (0, 0)
    tri = jnp.asarray(np.tril(np.ones((CHUNK, CHUNK), np.float32)), BF16)
    return pl.pallas_call(
        functools.partial(_gdn_kernel, tt=tt),
        out_shape=jax.ShapeDtypeStruct((B * S, GROUP_WIDTH), BF16),
        grid=(B, ns),
        in_specs=[pl.BlockSpec((tt, 3 * GROUP_WIDTH), row),
                  pl.BlockSpec((tt, GROUP_WIDTH), row),
                  pl.BlockSpec((tt, LANES), row),
                  pl.BlockSpec(conv_w.shape, const),
                  pl.BlockSpec((1, LANES), const),
                  pl.BlockSpec((1, LANES), const),
                  pl.BlockSpec((1, GROUP_WIDTH), const),
                  pl.BlockSpec((LANES, LANES), const),
                  pl.BlockSpec((LANES, GROUP_WIDTH), const),
                  pl.BlockSpec((LANES, GROUP_WIDTH), const),
                  pl.BlockSpec((CHUNK, CHUNK), const)],
        out_specs=pl.BlockSpec((tt, GROUP_WIDTH), row),
        scratch_shapes=[pltpu.VMEM((tt + 8, 3 * GROUP_WIDTH), F32),
                        pltpu.VMEM((N_HEADS, HEAD_DIM, HEAD_DIM), F32)],
        compiler_params=_params("parallel", "arbitrary"),
        name="gated_deltanet",
    )(b_qkv, b_z, small, conv_w.astype(F32), _row128(a_log, _LANE_GDN_A), _row128(dt_bias, _LANE_GDN_A),
      jnp.tile(norm_g.astype(F32), N_HEADS).reshape(1, GROUP_WIDTH), _group_matrix(LANES, HEAD_DIM, False),
      _expand_matrix(_LANE_GDN_A, N_HEADS, HEAD_DIM), _expand_matrix(_LANE_GDN_B, N_HEADS, HEAD_DIM), tri)


def _mlstm_kernel(qk_ref, v_ref, op_ref, sm_ref, ib_ref, fb_ref, ng_ref, gm_ref, ei_ref, ef_ref, tri_ref,
                  o_ref, c_sc, m_sc, *, tt):
    @pl.when(pl.program_id(1) == 0)
    def _():
        c_sc[...] = jnp.zeros(c_sc.shape, F32)
        m_sc[...] = jnp.zeros(m_sc.shape, F32)

    sm = sm_ref[...]
    ig = GATE_CAP * jnp.tanh((sm + ib_ref[...]) * (1.0 / GATE_CAP))
    lf = jax.nn.log_sigmoid(GATE_CAP * jnp.tanh((sm + fb_ref[...]) * (1.0 / GATE_CAP)))
    ix = _dot_c(ig, ei_ref[...])
    fx = _dot_c(lf, ef_ref[...])
    nqk = N_HEADS * MLSTM_QK
    q = qk_ref[:, 0:nqk]
    k = qk_ref[:, nqk:2 * nqk] * MLSTM_QK ** -0.5
    v = v_ref[...]
    ii = lax.broadcasted_iota(jnp.int32, (CHUNK, CHUNK), 0)
    jj = lax.broadcasted_iota(jnp.int32, (CHUNK, CHUNK), 1)
    one_col = jnp.where(jj == 0, 1.0, 0.0)
    nchunk = tt // CHUNK
    cb, dlog, blast, mloc, qh, vaug, cloc, qk_nt = [], [], [], [], [], [], [], []
    for c in range(nchunk):
        r = slice(c * CHUNK, (c + 1) * CHUNK)
        bx = _c_dot(tri_ref[...], fx[r])
        rowv_all = bx - ix[r]
        rowv_t = rowv_all.T
        for h in range(N_HEADS):
            hl = slice(h * LANES, (h + 1) * LANES)
            cb.append(bx[:, hl])
            dlog.append(jnp.where(ii >= jj, bx[:, h * LANES:h * LANES + CHUNK] - rowv_t[h * LANES:h * LANES + CHUNK, :],
                                  NEG))
            blast.append(bx[CHUNK - 1:CHUNK, hl])
            aend = blast[-1] - rowv_all[:, hl]
            mloc.append(jnp.max(aend, axis=0, keepdims=True))
            wend = jnp.exp(aend - mloc[-1])
            qh.append(q[r, h * MLSTM_QK:(h + 1) * MLSTM_QK].astype(BF16))
            kh = k[r, h * MLSTM_QK:(h + 1) * MLSTM_QK]
            vaug.append(jnp.concatenate([v[r, h * HEAD_DIM:(h + 1) * HEAD_DIM], one_col], axis=1).astype(BF16))
            cloc.append(_dot_tn((kh * wend[:, :MLSTM_QK]).astype(BF16), vaug[-1]))
            qk_nt.append(_dot_nt(qh[-1], kh.astype(BF16)))
    c_in, m_in = [], []
    c_st = [c_sc[h] for h in range(N_HEADS)]
    m_st = [m_sc[h][0:1, :] for h in range(N_HEADS)]
    for c in range(nchunk):
        for h in range(N_HEADS):
            i = c * N_HEADS + h
            c_in.append(c_st[h])
            m_in.append(m_st[h])
            m_new = jnp.maximum(blast[i] + m_st[h], mloc[i])
            c_st[h] = jnp.exp(blast[i] + m_st[h] - m_new) * c_st[h] + jnp.exp(mloc[i] - m_new) * cloc[i]
            m_st[h] = m_new
    for h in range(N_HEADS):
        c_sc[h] = c_st[h]
        m_sc[h] = jnp.broadcast_to(m_st[h], (8, LANES))
    inter = [b + m for b, m in zip(cb, m_in)]
    mt = [jnp.maximum(x, jnp.max(d, axis=-1, keepdims=True)) for x, d in zip(inter, dlog)]
    wintra = [(jnp.exp(d - m[:, :CHUNK]) * a).astype(BF16) for d, m, a in zip(dlog, mt, qk_nt)]
    numden = [jnp.exp(x - m) * _dot(qq, cc.astype(BF16)) + _dot(w, va)
              for x, m, qq, cc, w, va in zip(inter, mt, qh, c_in, wintra, vaug)]
    hout = [nd[:, :HEAD_DIM] / jnp.maximum(jnp.abs(nd[:, HEAD_DIM:HEAD_DIM + 1]), jnp.exp(-m[:, 0:1]))
            for nd, m in zip(numden, mt)]
    o = jnp.concatenate([jnp.concatenate(hout[c * N_HEADS:(c + 1) * N_HEADS], axis=1) for c in range(nchunk)], axis=0)
    normed = [_group_rms(o[:, c * LANES:(c + 1) * LANES], gm_ref[...]) for c in range(GROUP_WIDTH // LANES)]
    o = jnp.concatenate(normed, axis=1) * ng_ref[...]
    o_ref[...] = (o * jax.nn.sigmoid(op_ref[...])).astype(o_ref.dtype)


def _mlstm(c_qk, c_v, c_o, small, i_bias, f_bias, norm_g, B, S, tt=256):
    tt = min(tt, S)
    ns = S // tt
    row = lambda b, i: (b * ns + i, 0)
    const = lambda b, i: (0, 0)
    tri = jnp.asarray(np.tril(np.ones((CHUNK, CHUNK), np.float32)), BF16)
    return pl.pallas_call(
        functools.partial(_mlstm_kernel, tt=tt),
        out_shape=jax.ShapeDtypeStruct((B * S, GROUP_WIDTH), BF16),
        grid=(B, ns),
        in_specs=[pl.BlockSpec((tt, GROUP_WIDTH), row),
                  pl.BlockSpec((tt, GROUP_WIDTH), row),
                  pl.BlockSpec((tt, GROUP_WIDTH), row),
                  pl.BlockSpec((tt, LANES), row),
                  pl.BlockSpec((1, LANES), const),
                  pl.BlockSpec((1, LANES), const),
                  pl.BlockSpec((1, GROUP_WIDTH), const),
                  pl.BlockSpec((LANES, LANES), const),
                  pl.BlockSpec((LANES, N_HEADS * LANES), const),
                  pl.BlockSpec((LANES, N_HEADS * LANES), const),
                  pl.BlockSpec((CHUNK, CHUNK), const)],
        out_specs=pl.BlockSpec((tt, GROUP_WIDTH), row),
        scratch_shapes=[pltpu.VMEM((N_HEADS, MLSTM_QK, LANES), F32),
                        pltpu.VMEM((N_HEADS, 8, LANES), F32)],
        compiler_params=_params("parallel", "arbitrary"),
        name="mlstm",
    )(c_qk, c_v, c_o, small, _row128(i_bias, _LANE_I), _row128(f_bias, _LANE_F),
      jnp.tile(norm_g.astype(F32), N_HEADS).reshape(1, GROUP_WIDTH), _group_matrix(LANES, HEAD_DIM, True),
      _expand_matrix(_LANE_I, N_HEADS, LANES), _expand_matrix(_LANE_F, N_HEADS, LANES), tri)


def _memkv_kernel(mem_ref, ln_ref, wkv_ref, gk_ref, gm_ref, k_ref, v_ref):
    x = mem_ref[0]
    xn = (x * lax.rsqrt(jnp.mean(x * x, axis=-1, keepdims=True) + EPS) * ln_ref[...]).astype(BF16)
    kv = _dot(xn, wkv_ref[...])
    for c in range(GROUP_WIDTH // LANES):
        cs = slice(c * LANES, (c + 1) * LANES)
        k_ref[0, :, cs] = (_group_rms(kv[:, cs], gm_ref[...]) * gk_ref[:, cs]).astype(BF16)
    v_ref[0] = kv[:, GROUP_WIDTH:].astype(BF16)


def _memkv(mem, ln_mem, wkv, gain_k):
    B, M, _ = mem.shape
    const = lambda b: (0, 0)
    return pl.pallas_call(
        _memkv_kernel,
        out_shape=[jax.ShapeDtypeStruct((B, M, GROUP_WIDTH), BF16)] * 2,
        grid=(B,),
        in_specs=[pl.BlockSpec((1, M, D_MODEL), lambda b: (b, 0, 0)),
                  pl.BlockSpec((1, D_MODEL), const),
                  pl.BlockSpec((D_MODEL, 2 * GROUP_WIDTH), const),
                  pl.BlockSpec((1, GROUP_WIDTH), const),
                  pl.BlockSpec((LANES, LANES), const)],
        out_specs=[pl.BlockSpec((1, M, GROUP_WIDTH), lambda b: (b, 0, 0))] * 2,
        compiler_params=_params("parallel"),
        name="memory_kv",
    )(mem, ln_mem.reshape(1, D_MODEL), wkv.astype(BF16),
      jnp.tile(gain_k.astype(F32), N_HEADS).reshape(1, GROUP_WIDTH), _group_matrix(LANES, HEAD_DIM, True))


def _out_xattn_kernel(ya_ref, yb_ref, yc_ref, yd_ref, h_ref, wout_ref, lnx_ref, wq_ref, kx_ref, vx_ref, gq_ref, gm_ref,
                      wo_ref, o_ref):
    y = jnp.concatenate([ya_ref[...], yb_ref[...], yc_ref[...], yd_ref[...]], axis=1)
    h1 = h_ref[...] + _dot(y, wout_ref[...])
    hn = (h1 * lax.rsqrt(jnp.mean(h1 * h1, axis=-1, keepdims=True) + EPS) * lnx_ref[...]).astype(BF16)
    q = _dot(hn, wq_ref[...])
    chunks = []
    for c in range(GROUP_WIDTH // LANES):
        cs = slice(c * LANES, (c + 1) * LANES)
        chunks.append((_group_rms(q[:, cs], gm_ref[...]) * gq_ref[:, cs]).astype(BF16))
    qn = jnp.concatenate(chunks, axis=1)
    kx, vx = kx_ref[0], vx_ref[0]
    outs = []
    for h in range(N_HEADS):
        hs = slice(h * HEAD_DIM, (h + 1) * HEAD_DIM)
        s = _dot_nt(qn[:, hs], kx[:, hs]) * HEAD_DIM ** -0.5
        e = jnp.exp(s - jnp.max(s, axis=-1, keepdims=True))
        p = e / jnp.sum(e, axis=-1, keepdims=True)
        outs.append(_dot(p.astype(BF16), vx[:, hs]))
    o = jnp.concatenate(outs, axis=1).astype(BF16)
    o_ref[...] = h1 + _dot(o, wo_ref[...])


def _out_xattn(ys, h2d, w_out, ln_x, wq, kx, vx, gain_q, wo, B, S, tm=512):
    tm = min(tm, S)
    ns = S // tm
    M = kx.shape[1]
    row = lambda b, i: (b * ns + i, 0)
    const = lambda b, i: (0, 0)
    return pl.pallas_call(
        _out_xattn_kernel,
        out_shape=jax.ShapeDtypeStruct((B * S, D_MODEL), F32),
        grid=(B, ns),
        in_specs=[pl.BlockSpec((tm, GROUP_WIDTH), row)] * 4 + [
            pl.BlockSpec((tm, D_MODEL), row),
            pl.BlockSpec((D_MODEL, D_MODEL), const),
            pl.BlockSpec((1, D_MODEL), const),
            pl.BlockSpec((D_MODEL, GROUP_WIDTH), const),
            pl.BlockSpec((1, M, GROUP_WIDTH), lambda b, i: (b, 0, 0)),
            pl.BlockSpec((1, M, GROUP_WIDTH), lambda b, i: (b, 0, 0)),
            pl.BlockSpec((1, GROUP_WIDTH), const),
            pl.BlockSpec((LANES, LANES), const),
            pl.BlockSpec((GROUP_WIDTH, D_MODEL), const)],
        out_specs=pl.BlockSpec((tm, D_MODEL), row),
        compiler_params=_params("parallel", "parallel"),
        name="outproj_xattn",
    )(*ys, h2d, w_out.astype(BF16), ln_x.reshape(1, D_MODEL), wq.astype(BF16), kx, vx,
      jnp.tile(gain_q.astype(F32), N_HEADS).reshape(1, GROUP_WIDTH), _group_matrix(LANES, HEAD_DIM, True),
      wo.astype(BF16))


def _moe_kernel(h_ref, ln_ref, wr_ref, br_ref, w1_ref, w3_ref, w2_ref, o_ref, hn_sc, comb_sc, acc_sc):
    e = pl.program_id(1)
    tm = h_ref.shape[0]
    lane = lax.broadcasted_iota(jnp.int32, (tm, LANES), 1)

    @pl.when(e == 0)
    def _():
        x = h_ref[...]
        hn = x * lax.rsqrt(jnp.mean(x * x, axis=-1, keepdims=True) + EPS) * ln_ref[...]
        hn_sc[...] = hn.astype(BF16)
        logits = _dot(hn, wr_ref[...], HI) + br_ref[...]
        lanef = lane.astype(F32)
        big = 1e4
        isg = (lane >= MOE_EXPERTS) & (lane < MOE_EXPERTS + MOE_GROUPS)
        lg = jnp.where(isg, logits, NEG)
        gmax = jnp.max(lg, axis=-1, keepdims=True)
        grp_p = 1.0 / jnp.sum(jnp.exp(lg - gmax), axis=-1, keepdims=True)
        gidx = jnp.min(jnp.where(lg == gmax, lanef, big), axis=-1, keepdims=True) - MOE_EXPERTS
        ing = (lane < MOE_EXPERTS) & ((lane >> 3).astype(F32) == gidx)
        le = jnp.where(ing, logits, NEG)
        m1 = jnp.max(le, axis=-1, keepdims=True)
        z = jnp.sum(jnp.where(ing, jnp.exp(le - m1), 0.0), axis=-1, keepdims=True)
        i1 = jnp.min(jnp.where(le == m1, lanef, big), axis=-1, keepdims=True)
        oh1 = lanef == i1
        le2 = jnp.where(oh1, NEG, le)
        m2 = jnp.max(le2, axis=-1, keepdims=True)
        i2 = jnp.min(jnp.where((le2 == m2) & ing, jnp.where(oh1, big, lanef), big), axis=-1, keepdims=True)
        oh2 = lanef == i2
        p1 = 1.0 / z
        p2 = jnp.exp(m2 - m1) / z
        tot = p1 + p2
        comb_sc[...] = jnp.where(oh1, p1 / tot * grp_p, 0.0) + jnp.where(oh2, p2 / tot * grp_p, 0.0)
        acc_sc[...] = jnp.zeros(acc_sc.shape, F32)

    x = hn_sc[...]
    hg = _dot(x, w1_ref[0])
    hu = _dot(x, w3_ref[0])
    ce = jnp.sum(jnp.where(lane == e, comb_sc[...], 0.0), axis=-1, keepdims=True)
    act = (jax.nn.silu(hg) * hu * ce).astype(BF16)
    acc_sc[...] += _dot(act, w2_ref[0])

    @pl.when(e == MOE_EXPERTS - 1)
    def _():
        o_ref[...] = h_ref[...] + acc_sc[...]


def _moe(h2d, ln, w_group, b_group, w_expert, b_expert, w1, w3, w2, tm=1024):
    T = h2d.shape[0]
    tm = min(tm, T)
    pad = LANES - MOE_EXPERTS - MOE_GROUPS
    wr = jnp.concatenate([w_expert, w_group, jnp.zeros((D_MODEL, pad), F32)], axis=1)
    br = jnp.concatenate([b_expert, b_group, jnp.zeros((pad,), F32)]).reshape(1, LANES)
    const = lambda i, e: (0, 0)
    return pl.pallas_call(
        _moe_kernel,
        out_shape=jax.ShapeDtypeStruct((T, D_MODEL), F32),
        grid=(T // tm, MOE_EXPERTS),
        in_specs=[pl.BlockSpec((tm, D_MODEL), lambda i, e: (i, 0)),
                  pl.BlockSpec((1, D_MODEL), const),
                  pl.BlockSpec((D_MODEL, LANES), const),
                  pl.BlockSpec((1, LANES), const),
                  pl.BlockSpec((1, D_MODEL, MOE_FF), lambda i, e: (e, 0, 0)),
                  pl.BlockSpec((1, D_MODEL, MOE_FF), lambda i, e: (e, 0, 0)),
                  pl.BlockSpec((1, MOE_FF, D_MODEL), lambda i, e: (e, 0, 0))],
        out_specs=pl.BlockSpec((tm, D_MODEL), lambda i, e: (i, 0)),
        scratch_shapes=[pltpu.VMEM((tm, D_MODEL), BF16),
                        pltpu.VMEM((tm, LANES), F32),
                        pltpu.VMEM((tm, D_MODEL), F32)],
        compiler_params=_params("parallel", "arbitrary"),
        name="hier_moe",
    )(h2d, ln.reshape(1, D_MODEL), wr, br, w1.astype(BF16), w3.astype(BF16), w2.astype(BF16))


def _nsa_mixer(a_q, a_kv, small, qk_gain, cmp_pe, cmp_w1, cmp_w2, cos, sin, B, S):
    nc = S // NSA_CMP_STRIDE
    wide = NSA_CMP_STRIDE * HEAD_DIM
    gain_k = jnp.concatenate([qk_gain[2], qk_gain[3]]).reshape(1, LANES).astype(F32)
    kn = _normrope(a_kv, gain_k, HEAD_DIM, cos, sin, LANES, B, S)
    kraw = a_kv[:, 3 * LANES:3 * LANES + HEAD_DIM].reshape(B, nc, wide)
    vraw = a_kv[:, 3 * LANES + HEAD_DIM:].reshape(B, nc, wide)
    cos_c, sin_c = _rope_tables(NSA_CMP_STRIDE * np.arange(nc) + NSA_CMP_LEN - 1, HEAD_DIM)
    half = HEAD_DIM // 2
    kc, vc = _nsa_compress(kraw, vraw, cmp_pe.reshape(2, 2 * wide), cmp_w1, cmp_w2, qk_gain[1],
                           cos_c[:, :half], sin_c[:, half:2 * half], B, nc)
    gq_row = jnp.tile(qk_gain[0].astype(F32), N_HEADS).reshape(1, GROUP_WIDTH)
    qn, part, sel = _nsa1(a_q, small, kc, vc, kn, a_kv, cos, sin, gq_row, B, S)
    return _nsa2(qn, sel, kn, a_kv, part, small, B, S)


def kernel(x, mem, ln_mix, w_in, w_out, nsa_qk_gain, nsa_cmp_pe, nsa_cmp_w1, nsa_cmp_w2, gdn_conv, gdn_a_log, gdn_dt_bias, gdn_norm, mlstm_i_bias, mlstm_f_bias, mlstm_norm, diff_qk_gain, diff_lambda, diff_norm, ln_xattn, ln_mem, xattn_wq, xattn_wkv, xattn_qk_gain, xattn_wo, ln_moe, moe_w_group, moe_b_group, moe_w_expert, moe_b_expert, moe_w1, moe_w3, moe_w2):
    B, S, D = x.shape
    depth = w_in.shape[0]
    cos_a, sin_a = _rope_tables(np.arange(S), HEAD_DIM)
    cos_d, sin_d = _rope_tables(np.arange(S), DIFF_SUB)
    h = x.reshape(B * S, D)
    for l in range(depth):
        a_q, a_kv, b_qkv, b_z, c_qk, c_v, c_o, d_qk, d_v, small = _inproj(h, ln_mix[l], w_in[l])
        y_a = _nsa_mixer(a_q, a_kv, small, nsa_qk_gain[l], nsa_cmp_pe[l], nsa_cmp_w1[l], nsa_cmp_w2[l],
                         cos_a, sin_a, B, S)
        y_b = _gdn(b_qkv, b_z, small, gdn_conv[l], gdn_a_log[l], gdn_dt_bias[l], gdn_norm[l], B, S)
        y_c = _mlstm(c_qk, c_v, c_o, small, mlstm_i_bias[l], mlstm_f_bias[l], mlstm_norm[l], B, S)
        lambda_init = 0.8 - 0.6 * math.exp(-0.3 * l)
        gain_d = jnp.concatenate([jnp.tile(diff_qk_gain[l, 0], 2 * N_HEADS) * DIFF_SUB ** -0.5,
                                  jnp.tile(diff_qk_gain[l, 1], 2 * N_HEADS)]).reshape(1, 2 * GROUP_WIDTH).astype(F32)
        dqk = _normrope(d_qk, gain_d, DIFF_SUB, cos_d, sin_d, 2 * GROUP_WIDTH, B, S)
        y_d = _diff_attention(dqk, d_v, diff_lambda[l], diff_norm[l], lambda_init, B, S)
        kx, vx = _memkv(mem, ln_mem[l], xattn_wkv[l], xattn_qk_gain[l, 1])
        h = _out_xattn((y_a, y_b, y_c, y_d), h, w_out[l], ln_xattn[l], xattn_wq[l], kx, vx,
                       xattn_qk_gain[l, 0], xattn_wo[l], B, S)
        h = _moe(h, ln_moe[l], moe_w_group[l], moe_b_group[l], moe_w_expert[l], moe_b_expert[l],
                 moe_w1[l], moe_w3[l], moe_w2[l])
    return h.reshape(B, S, D)
```

```python
import functools
import math

import numpy as np
import jax
import jax.numpy as jnp
from jax import lax
from jax.experimental import pallas as pl
from jax.experimental.pallas import tpu as pltpu

F32 = jnp.float32
BF16 = jnp.bfloat16
HI = lax.Precision.HIGHEST

D_MODEL = 1024
HEAD_DIM = 64
N_HEADS = 4
GROUP_WIDTH = 256
ROPE_THETA = 10000.0
EPS = 1e-6
NEG = -1e30

NSA_CMP_LEN = 32
NSA_CMP_STRIDE = 16
NSA_SEL_LEN = 64
NSA_TOP_N = 16
NSA_WINDOW = 512
NSA_FORCE_BONUS = 1e3
CHUNK = 64
MLSTM_QK = 32
GATE_CAP = 15.0
DIFF_SUB = 32
MOE_GROUPS = 4
MOE_PER_GROUP = 8
MOE_EXPERTS = 32
MOE_FF = 256
LANES = 128
VMEM_LIMIT = 48 * 1024 * 1024

_A0, _B0, _C0, _D0 = 0, 652, 1684, 2460
_GROUPS = (
    ("a_q", ((_A0, 256),), F32),
    ("a_kv", ((_A0 + 384, 64), (_A0 + 512, 64), (_A0 + 448, 64), (None, 64), (_A0 + 576, 64), (None, 64),
              (_A0 + 256, 64), (_A0 + 320, 64)), F32),
    ("b_qkv", ((_B0, 768),), F32),
    ("b_z", ((_B0 + 776, 256),), F32),
    ("c_qk", ((_C0, 256),), F32),
    ("c_v", ((_C0 + 256, 256),), F32),
    ("c_o", ((_C0 + 520, 256),), F32),
    ("d_qk", ((_D0, 512),), F32),
    ("d_v", tuple(p for h in range(N_HEADS) for p in ((_D0 + 512 + 64 * h, 64), (None, 64))), BF16),
    ("small", ((_A0 + 640, 12), (_B0 + 768, 4), (_B0 + 772, 4), (_C0 + 512, 4), (_C0 + 516, 4), (None, 100)), F32),
)
_LANE_GDN_A, _LANE_GDN_B, _LANE_I, _LANE_F = 12, 16, 20, 24


def _dot(a, b, prec=None):
    return jnp.dot(a, b, preferred_element_type=F32, precision=prec)


def _dot_nt(a, b, prec=None):
    return lax.dot_general(a, b, (((1,), (1,)), ((), ())), preferred_element_type=F32, precision=prec)


def _dot_tn(a, b, prec=None):
    return lax.dot_general(a, b, (((0,), (0,)), ((), ())), preferred_element_type=F32, precision=prec)


def _bdot(a, b):
    return _dot(a.astype(BF16), b.astype(BF16))


def _bdot_nt(a, b):
    return _dot_nt(a.astype(BF16), b.astype(BF16))


def _bdot_tn(a, b):
    return _dot_tn(a.astype(BF16), b.astype(BF16))


def _split3(a):
    hi = a.astype(BF16)
    r = a - hi.astype(F32)
    mid = r.astype(BF16)
    return hi, mid, (r - mid.astype(F32)).astype(BF16)


def _dot_c(a, c):
    hi, mid, lo = _split3(a)
    return _dot(hi, c) + _dot(mid, c) + _dot(lo, c)


def _c_dot(c, b):
    hi, mid, lo = _split3(b)
    return _dot(c, hi) + _dot(c, mid) + _dot(c, lo)


def _dot_nt_x3(a, b):
    ah, bh = a.astype(BF16), b.astype(BF16)
    al, bl = (a - ah.astype(F32)).astype(BF16), (b - bh.astype(F32)).astype(BF16)
    return _dot_nt(ah, bh) + _dot_nt(ah, bl) + _dot_nt(al, bh)


def _params(*sem):
    return pltpu.CompilerParams(dimension_semantics=sem, vmem_limit_bytes=VMEM_LIMIT)


def _group_matrix(width, gsz, mean):
    g = np.kron(np.eye(width // gsz), np.ones((gsz, gsz)))
    return jnp.asarray(g / gsz if mean else g, BF16)


def _expand_matrix(src_lane0, n, out_per, stride=1):
    e = np.zeros((LANES, n * out_per), np.float32)
    for h in range(n):
        e[src_lane0 + stride * h, h * out_per:(h + 1) * out_per] = 1.0
    return jnp.asarray(e, BF16)


def _row128(vals, lane0):
    return jnp.zeros((1, LANES), F32).at[0, lane0:lane0 + vals.shape[0]].set(vals.astype(F32))


def _ones_col_row():
    return jnp.zeros((1, LANES), BF16).at[0, HEAD_DIM].set(1.0)


def _rope_tables(pos, dim):
    inv = 1.0 / (ROPE_THETA ** (jnp.arange(0, dim, 2, dtype=F32) / dim))
    ang = jnp.asarray(pos).astype(F32)[:, None] * inv[None, :]
    cos, sin = jnp.cos(ang), jnp.sin(ang)
    cosd = jnp.concatenate([cos, cos], axis=-1)
    sind = jnp.concatenate([-sin, sin], axis=-1)
    rep = LANES // dim
    return jnp.tile(cosd, (1, rep)), jnp.tile(sind, (1, rep))


def _rope128(x, cos, sin_signed, half):
    left = pltpu.roll(x, LANES - half, 1)
    right = pltpu.roll(x, half, 1)
    lane = lax.broadcasted_iota(jnp.int32, x.shape, 1)
    first = (lane & (2 * half - 1)) < half
    return x * cos + jnp.where(first, left, right) * sin_signed


def _rep_lanes(x, width):
    return x if width == LANES else jnp.concatenate([x] * (width // LANES), axis=1)


def _softmax_rows(s, mask):
    m = jnp.max(s, axis=-1, keepdims=True)
    e = jnp.exp(s - m)
    return jnp.where(mask, e / jnp.sum(e, axis=-1, keepdims=True), 0.0)


def _group_rms(x, gm):
    return x * lax.rsqrt(_dot_c(x * x, gm) + EPS)


def _flash_update(s, vaug, m_ref, acc_ref, idx, tk):
    m_prev = m_ref[idx]
    m_new = jnp.maximum(m_prev, jnp.max(s, axis=-1, keepdims=True))
    p = jnp.exp((s - _rep_lanes(m_new, tk)).astype(BF16))
    acc_ref[idx] = jnp.exp(m_prev - m_new) * acc_ref[idx] + _dot(p, vaug)
    m_ref[idx] = m_new


def _flash_result(acc_ref, idx):
    acc = acc_ref[idx]
    return acc[:, :HEAD_DIM] / acc[:, HEAD_DIM:HEAD_DIM + 1]


def _inproj_kernel(x_ref, g_ref, w_ref, *outs, widths):
    x = x_ref[...]
    ms = jnp.mean(x * x, axis=-1, keepdims=True)
    xn = (x * lax.rsqrt(ms + EPS) * g_ref[...]).astype(BF16)
    off = 0
    for o, wd in zip(outs, widths):
        o[...] = _dot(xn, w_ref[:, off:off + wd]).astype(o.dtype)
        off += wd


def _permute_w_in(w):
    cols, widths = [], []
    for _, parts, _ in _GROUPS:
        for s, n in parts:
            cols.append(jnp.zeros((w.shape[0], n), w.dtype) if s is None else w[:, s:s + n])
        widths.append(sum(n for _, n in parts))
    return jnp.concatenate(cols, axis=1).astype(BF16), tuple(widths)


def _inproj(h2d, gain, w_in, tm=256):
    T = h2d.shape[0]
    wp, widths = _permute_w_in(w_in)
    out_shape = [jax.ShapeDtypeStruct((T, wd), g[2]) for wd, g in zip(widths, _GROUPS)]
    return pl.pallas_call(
        functools.partial(_inproj_kernel, widths=widths),
        out_shape=out_shape,
        grid=(T // tm,),
        in_specs=[pl.BlockSpec((tm, D_MODEL), lambda i: (i, 0)),
                  pl.BlockSpec((1, D_MODEL), lambda i: (0, 0)),
                  pl.BlockSpec((D_MODEL, sum(widths)), lambda i: (0, 0))],
        out_specs=[pl.BlockSpec((tm, wd), lambda i: (i, 0)) for wd in widths],
        compiler_params=_params("parallel"),
        name="inproj",
    )(h2d, gain.reshape(1, D_MODEL), wp)


def _normrope_kernel(x_ref, gain_ref, gm_ref, cos_ref, sin_ref, o_ref, *, half, nchunk):
    for c in range(nchunk):
        cs = slice(c * LANES, (c + 1) * LANES)
        xn = _group_rms(x_ref[:, cs], gm_ref[...]) * gain_ref[:, cs]
        o_ref[:, cs] = _rope128(xn, cos_ref[...], sin_ref[...], half).astype(o_ref.dtype)


def _normrope(x, gain_row, gsz, cos, sin, width, B, S, tm=512):
    tm = min(tm, S)
    ns = S // tm
    return pl.pallas_call(
        functools.partial(_normrope_kernel, half=gsz // 2, nchunk=width // LANES),
        out_shape=jax.ShapeDtypeStruct((B * S, width), BF16),
        grid=(B, ns),
        in_specs=[pl.BlockSpec((tm, width), lambda b, i: (b * ns + i, 0)),
                  pl.BlockSpec((1, width), lambda b, i: (0, 0)),
                  pl.BlockSpec((LANES, LANES), lambda b, i: (0, 0)),
                  pl.BlockSpec((tm, LANES), lambda b, i: (i, 0)),
                  pl.BlockSpec((tm, LANES), lambda b, i: (i, 0))],
        out_specs=pl.BlockSpec((tm, width), lambda b, i: (b * ns + i, 0)),
        compiler_params=_params("parallel", "parallel"),
        name="normrope",
    )(x, gain_row, _group_matrix(LANES, gsz, True), cos, sin)


def _nsa_cmp_kernel(kv_ref, pe_ref, w1_ref, w2_ref, gain_ref, cos_ref, sin_ref, kc_ref, vc_ref, *, nc):
    half_in = NSA_CMP_STRIDE * HEAD_DIM

    a = [jnp.zeros((nc, 2 * HEAD_DIM), F32) for _ in range(2)]
    b = [jnp.zeros((nc, 2 * HEAD_DIM), F32) for _ in range(2)]
    for l in range(NSA_CMP_STRIDE):
        xl = kv_ref[pl.ds(l, nc, stride=NSA_CMP_STRIDE), :]
        for j in range(2):
            xj = xl[:, j * HEAD_DIM:(j + 1) * HEAD_DIM]
            a[j] = a[j] + _dot(xj, w1_ref[j, l * HEAD_DIM:(l + 1) * HEAD_DIM, :], HI)
            b[j] = b[j] + _dot(xj, w1_ref[j, half_in + l * HEAD_DIM:half_in + (l + 1) * HEAD_DIM, :], HI)

    def finish(j):
        pe = jnp.broadcast_to(pe_ref[j], (8, 2 * half_in))
        c = _dot(pe, w1_ref[j], HI)[0:1]
        hid = jax.nn.gelu(a[j] + pltpu.roll(b[j], nc - 1, 0) + c)
        return _dot(hid, w2_ref[j], HI)

    kc = finish(0)
    kc = kc * lax.rsqrt(jnp.mean(kc * kc, axis=-1, keepdims=True) + EPS) * gain_ref[...]
    x1, x2 = kc[:, :HEAD_DIM // 2], kc[:, HEAD_DIM // 2:]
    cos, sin = cos_ref[...], sin_ref[...]
    kc_ref[0] = jnp.concatenate([x1 * cos - x2 * sin, x2 * cos + x1 * sin], axis=1)
    vc_ref[0] = finish(1)


def _nsa_compress(a_kv, pe, w1, w2, gain, cos_c, sin_c, B, nc):
    wide = NSA_CMP_STRIDE * HEAD_DIM
    return pl.pallas_call(
        functools.partial(_nsa_cmp_kernel, nc=nc),
        out_shape=[jax.ShapeDtypeStruct((B, nc, HEAD_DIM), F32)] * 2,
        grid=(B,),
        in_specs=[pl.BlockSpec((nc * NSA_CMP_STRIDE, LANES), lambda b: (b, 3)),
                  pl.BlockSpec((2, 1, 2 * wide), lambda b: (0, 0, 0)),
                  pl.BlockSpec((2, 2 * wide, 2 * HEAD_DIM), lambda b: (0, 0, 0)),
                  pl.BlockSpec((2, 2 * HEAD_DIM, HEAD_DIM), lambda b: (0, 0, 0)),
                  pl.BlockSpec((1, HEAD_DIM), lambda b: (0, 0)),
                  pl.BlockSpec((nc, HEAD_DIM // 2), lambda b: (0, 0)),
                  pl.BlockSpec((nc, HEAD_DIM // 2), lambda b: (0, 0))],
        out_specs=[pl.BlockSpec((1, nc, HEAD_DIM), lambda b: (b, 0, 0))] * 2,
        compiler_params=_params("parallel"),
        name="nsa_compress",
    )(a_kv, pe.reshape(2, 1, 2 * wide), w1, w2, gain.reshape(1, HEAD_DIM), cos_c, sin_c)


def _nsa1_kernel(q_ref, sm_ref, kc_ref, vc_ref, kn_ref, vw_ref, cos_ref, sin_ref, gq_ref, gm_ref, ovl_ref,
                 eg0_ref, eg2_ref, qn_ref, part_ref, sel_ref, *, tq, nc, n_sel):
    t0 = pl.program_id(1) * tq
    chunks = []
    for c in range(GROUP_WIDTH // LANES):
        cs = slice(c * LANES, (c + 1) * LANES)
        xn = _group_rms(q_ref[:, cs], gm_ref[...]) * gq_ref[:, cs]
        chunks.append(_rope128(xn, cos_ref[...], sin_ref[...], HEAD_DIM // 2))
    qs = jnp.concatenate(chunks, axis=1) * HEAD_DIM ** -0.5
    qb = qs.astype(BF16)
    qn_ref[...] = qb
    sig = jax.nn.sigmoid(sm_ref[...])
    g0x = _dot_c(sig, eg0_ref[...])
    g2x = _dot_c(sig, eg2_ref[...])
    tpos = t0 + lax.broadcasted_iota(jnp.int32, (tq, 1), 0)

    kc = kc_ref[0]
    vc = vc_ref[0].astype(BF16)
    cidx = lax.broadcasted_iota(jnp.int32, (1, nc), 1)
    cmask = ((NSA_CMP_STRIDE * cidx + NSA_CMP_LEN - 1) <= tpos) & (cidx < nc - 1)
    psum = jnp.zeros((tq, nc), F32)
    o_cmp = []
    for h in range(N_HEADS):
        hs = slice(h * HEAD_DIM, (h + 1) * HEAD_DIM)
        p = _softmax_rows(jnp.where(cmask, _dot_nt_x3(qs[:, hs], kc), NEG), cmask)
        o_cmp.append(_dot(p.astype(BF16), vc))
        psum = psum + p

    imp = _dot_c(psum, ovl_ref[...])
    j = lax.broadcasted_iota(jnp.int32, (tq, LANES), 1)
    cur = tpos >> 6
    valid = j <= cur
    forced = (j == 0) | (j == cur) | (j == cur - 1)
    score = jnp.where(valid, imp + jnp.where(forced, NSA_FORCE_BONUS, 0.0), NEG)
    nrow = -(-n_sel // 8) * 8
    st = score.T[0:nrow, :]
    jj = lax.broadcasted_iota(jnp.int32, (nrow, 1), 0)
    cnt = jnp.zeros((nrow, tq), F32)
    for i in range(n_sel):
        row = st[i:i + 1, :]
        tie = jnp.where(jj > i, 1.0, 0.0)
        cnt = cnt + jnp.where(row > st, 1.0, jnp.where(row == st, tie, 0.0))
    sel_t = jnp.where((cnt < min(NSA_TOP_N, n_sel)) & (st > 0.5 * NEG), 1.0, 0.0)
    if nrow < LANES:
        sel_t = jnp.concatenate([sel_t, jnp.zeros((LANES - nrow, tq), F32)], axis=0)
    sel_ref[...] = sel_t.T.astype(BF16)

    band = tq + NSA_WINDOW
    start = pl.multiple_of(jnp.maximum(t0 - NSA_WINDOW, 0), LANES)
    kw = kn_ref[pl.ds(start, band), HEAD_DIM:2 * HEAD_DIM]
    vw = vw_ref[pl.ds(start, band), 0:HEAD_DIM].astype(BF16)
    dist = tpos - (start + lax.broadcasted_iota(jnp.int32, (1, band), 1))
    wmask = (dist >= 0) & (dist < NSA_WINDOW)
    o_win = []
    for h in range(N_HEADS):
        hs = slice(h * HEAD_DIM, (h + 1) * HEAD_DIM)
        p = _softmax_rows(jnp.where(wmask, _dot_nt(qb[:, hs], kw), NEG), wmask)
        o_win.append(_dot(p.astype(BF16), vw))
    part_ref[...] = g0x * jnp.concatenate(o_cmp, axis=1) + g2x * jnp.concatenate(o_win, axis=1)


def _nsa_overlap(nc, n_sel):
    c0 = NSA_CMP_STRIDE * np.arange(nc)[:, None]
    s0 = NSA_SEL_LEN * np.arange(n_sel)[None, :]
    ov = np.clip(np.minimum(c0 + NSA_CMP_LEN, s0 + NSA_SEL_LEN) - np.maximum(c0, s0), 0, None) / NSA_CMP_STRIDE
    ov[nc - 1:] = 0.0
    out = np.zeros((nc, LANES), np.float32)
    out[:, :n_sel] = ov
    return jnp.asarray(out, BF16)


def _nsa1(a_q, small, kc, vc, kn, a_kv, cos, sin, gq_row, B, S, tq=256):
    tq = min(tq, S)
    nq = S // tq
    nc = S // NSA_CMP_STRIDE
    n_sel = S // NSA_SEL_LEN
    row = lambda b, i: (b * nq + i, 0)
    return pl.pallas_call(
        functools.partial(_nsa1_kernel, tq=tq, nc=nc, n_sel=n_sel),
        out_shape=[jax.ShapeDtypeStruct((B * S, GROUP_WIDTH), BF16),
                   jax.ShapeDtypeStruct((B * S, GROUP_WIDTH), F32),
                   jax.ShapeDtypeStruct((B * S, LANES), BF16)],
        grid=(B, nq),
        in_specs=[pl.BlockSpec((tq, GROUP_WIDTH), row),
                  pl.BlockSpec((tq, LANES), row),
                  pl.BlockSpec((1, nc, HEAD_DIM), lambda b, i: (b, 0, 0)),
                  pl.BlockSpec((1, nc, HEAD_DIM), lambda b, i: (b, 0, 0)),
                  pl.BlockSpec((S, LANES), lambda b, i: (b, 0)),
                  pl.BlockSpec((S, LANES), lambda b, i: (b, 2)),
                  pl.BlockSpec((tq, LANES), lambda b, i: (i, 0)),
                  pl.BlockSpec((tq, LANES), lambda b, i: (i, 0)),
                  pl.BlockSpec((1, GROUP_WIDTH), lambda b, i: (0, 0)),
                  pl.BlockSpec((LANES, LANES), lambda b, i: (0, 0)),
                  pl.BlockSpec((nc, LANES), lambda b, i: (0, 0)),
                  pl.BlockSpec((LANES, GROUP_WIDTH), lambda b, i: (0, 0)),
                  pl.BlockSpec((LANES, GROUP_WIDTH), lambda b, i: (0, 0))],
        out_specs=[pl.BlockSpec((tq, GROUP_WIDTH), row),
                   pl.BlockSpec((tq, GROUP_WIDTH), row),
                   pl.BlockSpec((tq, LANES), row)],
        compiler_params=_params("parallel", "parallel"),
        name="nsa_cmp_win_select",
    )(a_q, small, kc, vc, kn, a_kv, cos, sin, gq_row, _group_matrix(LANES, HEAD_DIM, True),
      _nsa_overlap(nc, n_sel), _expand_matrix(0, N_HEADS, HEAD_DIM, 3), _expand_matrix(2, N_HEADS, HEAD_DIM, 3))


def _nsa2_kernel(qn_ref, sel_ref, e_ref, kn_ref, vs_ref, one_ref, part_ref, sm_ref, eg1_ref, o_ref, m_sc, acc_sc,
                 *, tq, tk, nk):
    qi = pl.program_id(1)
    ki = pl.program_id(2)

    @pl.when(ki == 0)
    def _():
        m_sc[...] = jnp.full(m_sc.shape, NEG, F32)
        acc_sc[...] = jnp.zeros(acc_sc.shape, F32)

    def step(causal):
        mask = _dot(sel_ref[...], e_ref[...]) > 0.5
        if causal:
            tpos = qi * tq + lax.broadcasted_iota(jnp.int32, (tq, 1), 0)
            kpos = ki * tk + lax.broadcasted_iota(jnp.int32, (1, tk), 1)
            mask = mask & (kpos <= tpos)
        ks = kn_ref[:, 0:HEAD_DIM]
        vaug = vs_ref[...].astype(BF16) + one_ref[...]
        for h in range(N_HEADS):
            s = jnp.where(mask, _dot_nt(qn_ref[:, h * HEAD_DIM:(h + 1) * HEAD_DIM], ks), NEG)
            _flash_update(s, vaug, m_sc, acc_sc, h, tk)

    @pl.when(ki * tk + tk - 1 <= qi * tq)
    def _():
        step(False)

    @pl.when((ki * tk + tk - 1 > qi * tq) & (ki * tk <= qi * tq + tq - 1))
    def _():
        step(True)

    @pl.when(ki == nk - 1)
    def _():
        g1x = _dot_c(jax.nn.sigmoid(sm_ref[...]), eg1_ref[...])
        o = jnp.concatenate([_flash_result(acc_sc, h) for h in range(N_HEADS)], axis=1)
        o_ref[...] = (part_ref[...] + g1x * o).astype(o_ref.dtype)


def _nsa2(qn, sel, kn, a_kv, part, small, B, S, tq=512, tk=512):
    tq, tk = min(tq, S), min(tk, S)
    nq, nk = S // tq, S // tk
    e = np.zeros((LANES, S), np.float32)
    e[np.arange(S) // NSA_SEL_LEN, np.arange(S)] = 1.0
    row = lambda b, i, k: (b * nq + i, 0)
    kclamp = lambda i, k: jnp.minimum(k, (i * tq + tq - 1) // tk)
    return pl.pallas_call(
        functools.partial(_nsa2_kernel, tq=tq, tk=tk, nk=nk),
        out_shape=jax.ShapeDtypeStruct((B * S, GROUP_WIDTH), BF16),
        grid=(B, nq, nk),
        in_specs=[pl.BlockSpec((tq, GROUP_WIDTH), row),
                  pl.BlockSpec((tq, LANES), row),
                  pl.BlockSpec((LANES, tk), lambda b, i, k: (0, kclamp(i, k))),
                  pl.BlockSpec((tk, LANES), lambda b, i, k: (b * nk + kclamp(i, k), 0)),
                  pl.BlockSpec((tk, LANES), lambda b, i, k: (b * nk + kclamp(i, k), 1)),
                  pl.BlockSpec((1, LANES), lambda b, i, k: (0, 0)),
                  pl.BlockSpec((tq, GROUP_WIDTH), row),
                  pl.BlockSpec((tq, LANES), row),
                  pl.BlockSpec((LANES, GROUP_WIDTH), lambda b, i, k: (0, 0))],
        out_specs=pl.BlockSpec((tq, GROUP_WIDTH), row),
        scratch_shapes=[pltpu.VMEM((N_HEADS, tq, LANES), F32),
                        pltpu.VMEM((N_HEADS, tq, LANES), F32)],
        compiler_params=_params("parallel", "parallel", "arbitrary"),
        name="nsa_selected",
    )(qn, sel, jnp.asarray(e, BF16), kn, a_kv, _ones_col_row(), part, small, _expand_matrix(1, N_HEADS, HEAD_DIM, 3))


def _diff_kernel(q_ref, k_ref, v_ref, one_ref, lam_ref, gain_ref, o_ref, m_sc, acc_sc, *, tq, tk, nk, lambda_init):
    qi = pl.program_id(1)
    ki = pl.program_id(2)

    @pl.when(ki == 0)
    def _():
        m_sc[...] = jnp.full(m_sc.shape, NEG, F32)
        acc_sc[...] = jnp.zeros(acc_sc.shape, F32)

    def step(causal):
        if causal:
            tpos = qi * tq + lax.broadcasted_iota(jnp.int32, (tq, 1), 0)
            kpos = ki * tk + lax.broadcasted_iota(jnp.int32, (1, tk), 1)
            mask = kpos <= tpos
        for h in range(N_HEADS):
            vaug = v_ref[:, h * LANES:(h + 1) * LANES] + one_ref[...]
            for c in range(2):
                idx = 2 * h + c
                cs = slice(idx * DIFF_SUB, (idx + 1) * DIFF_SUB)
                s = _dot_nt(q_ref[:, cs], k_ref[:, cs])
                if causal:
                    s = jnp.where(mask, s, NEG)
                _flash_update(s, vaug, m_sc, acc_sc, idx, tk)

    @pl.when(ki * tk + tk - 1 <= qi * tq)
    def _():
        step(False)

    @pl.when((ki * tk + tk - 1 > qi * tq) & (ki * tk <= qi * tq + tq - 1))
    def _():
        step(True)

    @pl.when(ki == nk - 1)
    def _():
        lm = lam_ref[...]
        lam = (jnp.exp(jnp.sum(lm[0:1] * lm[1:2], axis=-1, keepdims=True))
               - jnp.exp(jnp.sum(lm[2:3] * lm[3:4], axis=-1, keepdims=True)) + lambda_init)
        outs = []
        for h in range(N_HEADS):
            o = _flash_result(acc_sc, 2 * h) - lam * _flash_result(acc_sc, 2 * h + 1)
            o = o * lax.rsqrt(jnp.mean(o * o, axis=-1, keepdims=True) + EPS)
            outs.append(o * gain_ref[...] * (1.0 - lambda_init))
        o_ref[...] = jnp.concatenate(outs, axis=1).astype(o_ref.dtype)


def _diff_attention(qk, v, lam, norm_g, lambda_init, B, S, tq=512, tk=512):
    tq, tk = min(tq, S), min(tk, S)
    nq, nk = S // tq, S // tk
    lam_pad = jnp.zeros((4, LANES), F32).at[:, :DIFF_SUB].set(lam.astype(F32))
    kclamp = lambda i, k: jnp.minimum(k, (i * tq + tq - 1) // tk)
    const = lambda b, i, k: (0, 0)
    return pl.pallas_call(
        functools.partial(_diff_kernel, tq=tq, tk=tk, nk=nk, lambda_init=lambda_init),
        out_shape=jax.ShapeDtypeStruct((B * S, GROUP_WIDTH), BF16),
        grid=(B, nq, nk),
        in_specs=[pl.BlockSpec((tq, GROUP_WIDTH), lambda b, i, k: (b * nq + i, 0)),
                  pl.BlockSpec((tk, GROUP_WIDTH), lambda b, i, k: (b * nk + kclamp(i, k), 1)),
                  pl.BlockSpec((tk, N_HEADS * LANES), lambda b, i, k: (b * nk + kclamp(i, k), 0)),
                  pl.BlockSpec((1, LANES), const),
                  pl.BlockSpec((4, LANES), const),
                  pl.BlockSpec((1, HEAD_DIM), const)],
        out_specs=pl.BlockSpec((tq, GROUP_WIDTH), lambda b, i, k: (b * nq + i, 0)),
        scratch_shapes=[pltpu.VMEM((2 * N_HEADS, tq, LANES), F32),
                        pltpu.VMEM((2 * N_HEADS, tq, LANES), F32)],
        compiler_params=_params("parallel", "parallel", "arbitrary"),
        name="diff_attention",
    )(qk, qk, v, _ones_col_row(), lam_pad, norm_g.reshape(1, HEAD_DIM).astype(F32))


def _gdn_kernel(x_ref, z_ref, sm_ref, cw_ref, alog_ref, dtb_ref, ng_ref, gs_ref, ea_ref, eb_ref, tri_ref,
                o_ref, xs_sc, st_sc, *, tt):
    kconv = cw_ref.shape[0]

    @pl.when(pl.program_id(1) == 0)
    def _():
        xs_sc[0:8, :] = jnp.zeros((8, xs_sc.shape[1]), F32)
        st_sc[...] = jnp.zeros(st_sc.shape, F32)

    xs_sc[8:8 + tt, :] = x_ref[...]
    conv = cw_ref[0:1, :] * xs_sc[pl.ds(8 - (kconv - 1), tt), :]
    for j in range(1, kconv):
        conv = conv + cw_ref[j:j + 1, :] * xs_sc[pl.ds(8 - (kconv - 1) + j, tt), :]
    xs_sc[0:8, :] = x_ref[tt - 8:tt, :]
    qkv = jax.nn.silu(conv)

    def l2n(a):
        out = []
        for c in range(GROUP_WIDTH // LANES):
            xc = a[:, c * LANES:(c + 1) * LANES]
            out.append(xc * lax.rsqrt(_dot_c(xc * xc, gs_ref[...]) + EPS))
        return jnp.concatenate(out, axis=1)

    q = l2n(qkv[:, 0:GROUP_WIDTH]) * HEAD_DIM ** -0.5
    k = l2n(qkv[:, GROUP_WIDTH:2 * GROUP_WIDTH])
    v = qkv[:, 2 * GROUP_WIDTH:]
    sm = sm_ref[...]
    g_all = -jnp.exp(alog_ref[...]) * jax.nn.softplus(sm + dtb_ref[...])
    gx = _dot_c(g_all, ea_ref[...])
    bx = _dot_c(jax.nn.sigmoid(sm), eb_ref[...])
    ii = lax.broadcasted_iota(jnp.int32, (CHUNK, CHUNK), 0)
    jj = lax.broadcasted_iota(jnp.int32, (CHUNK, CHUNK), 1)
    eye = jnp.where(ii == jj, 1.0, 0.0)
    nchunk = tt // CHUNK
    heads = [slice(h * HEAD_DIM, (h + 1) * HEAD_DIM) for h in range(N_HEADS)]
    pairs, dec, rhs, qd, kend, eglast, qk_nt, kbk_nt = [], [], [], [], [], [], [], []
    for c in range(nchunk):
        r = slice(c * CHUNK, (c + 1) * CHUNK)
        gcx = _c_dot(tri_ref[...], gx[r])
        gct = gcx.T
        egc = jnp.exp(gcx)
        glast = gcx[CHUNK - 1:CHUNK, :]
        kc, qc, bc = k[r].astype(BF16), q[r], bx[r]
        kb = k[r] * bc
        vb = v[r] * bc
        kbe = kb * egc
        qd.append((qc * egc).astype(BF16))
        kend.append((k[r] * jnp.exp(glast - gcx)).astype(BF16))
        eglast.append(jnp.exp(glast))
        qcb, kbb = qc.astype(BF16), kb.astype(BF16)
        for hs in heads:
            pairs.append((c, hs))
            dec.append(jnp.exp(jnp.where(ii >= jj, gcx[:, hs] - gct[hs, :], NEG)))
            rhs.append(jnp.concatenate([vb[:, hs], kbe[:, hs]], axis=1).astype(BF16))
            kbk_nt.append(_dot_nt(kbb[:, hs], kc[:, hs]))
            qk_nt.append(_dot_nt(qcb[:, hs], kc[:, hs]))
    pw = [-jnp.where(ii > jj, a * d, 0.0) for a, d in zip(kbk_nt, dec)]
    inv = [eye + p for p in pw]
    for _ in range(5):
        pw = [_bdot(p, p) for p in pw]
        inv = [x + _bdot(x, p) for x, p in zip(inv, pw)]
    sol = [_bdot(x, b) for x, b in zip(inv, rhs)]
    attn = [(a * d).astype(BF16) for a, d in zip(qk_nt, dec)]
    state = [st_sc[h] for h in range(N_HEADS)]
    outs = []
    for c in range(nchunk):
        idx = [c * N_HEADS + h for h in range(N_HEADS)]
        sb = [s.astype(BF16) for s in state]
        ws = [_dot(sol[i][:, HEAD_DIM:].astype(BF16), sb[h]) for h, i in enumerate(idx)]
        qs = [_dot(qd[c][:, hs], sb[h]) for h, hs in enumerate(heads)]
        v_new = [(sol[i][:, :HEAD_DIM] - ws[h]).astype(BF16) for h, i in enumerate(idx)]
        outs.append(jnp.concatenate([qs[h] + _dot(attn[i], v_new[h]) for h, i in enumerate(idx)], axis=1))
        state = [state[h] * eglast[c][:, hs] + _dot_tn(kend[c][:, hs], v_new[h]) for h, hs in enumerate(heads)]
    for h in range(N_HEADS):
        st_sc[h] = state[h]
    o = jnp.concatenate(outs, axis=0)
    normed = []
    for c in range(GROUP_WIDTH // LANES):
        oc = o[:, c * LANES:(c + 1) * LANES]
        normed.append(oc * lax.rsqrt(_dot_c(oc * oc, gs_ref[...]) * (1.0 / HEAD_DIM) + EPS))
    o = jnp.concatenate(normed, axis=1) * ng_ref[...]
    o_ref[...] = (o * jax.nn.silu(z_ref[...])).astype(o_ref.dtype)


def _gdn(b_qkv, b_z, small, conv_w, a_log, dt_bias, norm_g, B, S, tt=256):
    tt = min(tt, S)
    ns = S // tt
    row = lambda b, i: (b * ns + i, 0)
    const = lambda b, i: (0, 0)
    tri = jnp.asarray(np.tril(np.ones((CHUNK, CHUNK), np.float32)), BF16)
    return pl.pallas_call(
        functools.partial(_gdn_kernel, tt=tt),
        out_shape=jax.ShapeDtypeStruct((B * S, GROUP_WIDTH), BF16),
        grid=(B, ns),
        in_specs=[pl.BlockSpec((tt, 3 * GROUP_WIDTH), row),
                  pl.BlockSpec((tt, GROUP_WIDTH), row),
                  pl.BlockSpec((tt, LANES), row),
                  pl.BlockSpec(conv_w.shape, const),
                  pl.BlockSpec((1, LANES), const),
                  pl.BlockSpec((1, LANES), const),
                  pl.BlockSpec((1, GROUP_WIDTH), const),
                  pl.BlockSpec((LANES, LANES), const),
                  pl.BlockSpec((LANES, GROUP_WIDTH), const),
                  pl.BlockSpec((LANES, GROUP_WIDTH), const),
                  pl.BlockSpec((CHUNK, CHUNK), const)],
        out_specs=pl.BlockSpec((tt, GROUP_WIDTH), row),
        scratch_shapes=[pltpu.VMEM((tt + 8, 3 * GROUP_WIDTH), F32),
                        pltpu.VMEM((N_HEADS, HEAD_DIM, HEAD_DIM), F32)],
        compiler_params=_params("parallel", "arbitrary"),
        name="gated_deltanet",
    )(b_qkv, b_z, small, conv_w.astype(F32), _row128(a_log, _LANE_GDN_A), _row128(dt_bias, _LANE_GDN_A),
      jnp.tile(norm_g.astype(F32), N_HEADS).reshape(1, GROUP_WIDTH), _group_matrix(LANES, HEAD_DIM, False),
      _expand_matrix(_LANE_GDN_A, N_HEADS, HEAD_DIM), _expand_matrix(_LANE_GDN_B, N_HEADS, HEAD_DIM), tri)


def _mlstm_kernel(qk_ref, v_ref, op_ref, sm_ref, ib_ref, fb_ref, ng_ref, gm_ref, ei_ref, ef_ref, tri_ref,
                  o_ref, c_sc, m_sc, *, tt):
    @pl.when(pl.program_id(1) == 0)
    def _():
        c_sc[...] = jnp.zeros(c_sc.shape, F32)
        m_sc[...] = jnp.zeros(m_sc.shape, F32)

    sm = sm_ref[...]
    ig = GATE_CAP * jnp.tanh((sm + ib_ref[...]) * (1.0 / GATE_CAP))
    lf = jax.nn.log_sigmoid(GATE_CAP * jnp.tanh((sm + fb_ref[...]) * (1.0 / GATE_CAP)))
    ix = _dot_c(ig, ei_ref[...])
    fx = _dot_c(lf, ef_ref[...])
    nqk = N_HEADS * MLSTM_QK
    q = qk_ref[:, 0:nqk]
    k = qk_ref[:, nqk:2 * nqk] * MLSTM_QK ** -0.5
    v = v_ref[...]
    ii = lax.broadcasted_iota(jnp.int32, (CHUNK, CHUNK), 0)
    jj = lax.broadcasted_iota(jnp.int32, (CHUNK, CHUNK), 1)
    one_col = jnp.where(jj == 0, 1.0, 0.0)
    nchunk = tt // CHUNK
    cb, dlog, blast, mloc, qh, vaug, cloc, qk_nt = [], [], [], [], [], [], [], []
    for c in range(nchunk):
        r = slice(c * CHUNK, (c + 1) * CHUNK)
        bx = _c_dot(tri_ref[...], fx[r])
        rowv_all = bx - ix[r]
        rowv_t = rowv_all.T
        for h in range(N_HEADS):
            hl = slice(h * LANES, (h + 1) * LANES)
            cb.append(bx[:, hl])
            dlog.append(jnp.where(ii >= jj, bx[:, h * LANES:h * LANES + CHUNK] - rowv_t[h * LANES:h * LANES + CHUNK, :],
                                  NEG))
            blast.append(bx[CHUNK - 1:CHUNK, hl])
            aend = blast[-1] - rowv_all[:, hl]
            mloc.append(jnp.max(aend, axis=0, keepdims=True))
            wend = jnp.exp(aend - mloc[-1])
            qh.append(q[r, h * MLSTM_QK:(h + 1) * MLSTM_QK].astype(BF16))
            kh = k[r, h * MLSTM_QK:(h + 1) * MLSTM_QK]
            vaug.append(jnp.concatenate([v[r, h * HEAD_DIM:(h + 1) * HEAD_DIM], one_col], axis=1).astype(BF16))
            cloc.append(_dot_tn((kh * wend[:, :MLSTM_QK]).astype(BF16), vaug[-1]))
            qk_nt.append(_dot_nt(qh[-1], kh.astype(BF16)))
    c_in, m_in = [], []
    c_st = [c_sc[h] for h in range(N_HEADS)]
    m_st = [m_sc[h][0:1, :] for h in range(N_HEADS)]
    for c in range(nchunk):
        for h in range(N_HEADS):
            i = c * N_HEADS + h
            c_in.append(c_st[h])
            m_in.append(m_st[h])
            m_new = jnp.maximum(blast[i] + m_st[h], mloc[i])
            c_st[h] = jnp.exp(blast[i] + m_st[h] - m_new) * c_st[h] + jnp.exp(mloc[i] - m_new) * cloc[i]
            m_st[h] = m_new
    for h in range(N_HEADS):
        c_sc[h] = c_st[h]
        m_sc[h] = jnp.broadcast_to(m_st[h], (8, LANES))
    inter = [b + m for b, m in zip(cb, m_in)]
    mt = [jnp.maximum(x, jnp.max(d, axis=-1, keepdims=True)) for x, d in zip(inter, dlog)]
    wintra = [(jnp.exp(d - m[:, :CHUNK]) * a).astype(BF16) for d, m, a in zip(dlog, mt, qk_nt)]
    numden = [jnp.exp(x - m) * _dot(qq, cc.astype(BF16)) + _dot(w, va)
              for x, m, qq, cc, w, va in zip(inter, mt, qh, c_in, wintra, vaug)]
    hout = [nd[:, :HEAD_DIM] / jnp.maximum(jnp.abs(nd[:, HEAD_DIM:HEAD_DIM + 1]), jnp.exp(-m[:, 0:1]))
            for nd, m in zip(numden, mt)]
    o = jnp.concatenate([jnp.concatenate(hout[c * N_HEADS:(c + 1) * N_HEADS], axis=1) for c in range(nchunk)], axis=0)
    normed = [_group_rms(o[:, c * LANES:(c + 1) * LANES], gm_ref[...]) for c in range(GROUP_WIDTH // LANES)]
    o = jnp.concatenate(normed, axis=1) * ng_ref[...]
    o_ref[...] = (o * jax.nn.sigmoid(op_ref[...])).astype(o_ref.dtype)


def _mlstm(c_qk, c_v, c_o, small, i_bias, f_bias, norm_g, B, S, tt=256):
    tt = min(tt, S)
    ns = S // tt
    row = lambda b, i: (b * ns + i, 0)
    const = lambda b, i: (0, 0)
    tri = jnp.asarray(np.tril(np.ones((CHUNK, CHUNK), np.float32)), BF16)
    return pl.pallas_call(
        functools.partial(_mlstm_kernel, tt=tt),
        out_shape=jax.ShapeDtypeStruct((B * S, GROUP_WIDTH), BF16),
        grid=(B, ns),
        in_specs=[pl.BlockSpec((tt, GROUP_WIDTH), row),
                  pl.BlockSpec((tt, GROUP_WIDTH), row),
                  pl.BlockSpec((tt, GROUP_WIDTH), row),
                  pl.BlockSpec((tt, LANES), row),
                  pl.BlockSpec((1, LANES), const),
                  pl.BlockSpec((1, LANES), const),
                  pl.BlockSpec((1, GROUP_WIDTH), const),
                  pl.BlockSpec((LANES, LANES), const),
                  pl.BlockSpec((LANES, N_HEADS * LANES), const),
                  pl.BlockSpec((LANES, N_HEADS * LANES), const),
                  pl.BlockSpec((CHUNK, CHUNK), const)],
        out_specs=pl.BlockSpec((tt, GROUP_WIDTH), row),
        scratch_shapes=[pltpu.VMEM((N_HEADS, MLSTM_QK, LANES), F32),
                        pltpu.VMEM((N_HEADS, 8, LANES), F32)],
        compiler_params=_params("parallel", "arbitrary"),
        name="mlstm",
    )(c_qk, c_v, c_o, small, _row128(i_bias, _LANE_I), _row128(f_bias, _LANE_F),
      jnp.tile(norm_g.astype(F32), N_HEADS).reshape(1, GROUP_WIDTH), _group_matrix(LANES, HEAD_DIM, True),
      _expand_matrix(_LANE_I, N_HEADS, LANES), _expand_matrix(_LANE_F, N_HEADS, LANES), tri)


def _memkv_kernel(mem_ref, ln_ref, wkv_ref, gk_ref, gm_ref, k_ref, v_ref):
    x = mem_ref[0]
    xn = (x * lax.rsqrt(jnp.mean(x * x, axis=-1, keepdims=True) + EPS) * ln_ref[...]).astype(BF16)
    kv = _dot(xn, wkv_ref[...])
    for c in range(GROUP_WIDTH // LANES):
        cs = slice(c * LANES, (c + 1) * LANES)
        k_ref[0, :, cs] = (_group_rms(kv[:, cs], gm_ref[...]) * gk_ref[:, cs]).astype(BF16)
    v_ref[0] = kv[:, GROUP_WIDTH:].astype(BF16)


def _memkv(mem, ln_mem, wkv, gain_k):
    B, M, _ = mem.shape
    const = lambda b: (0, 0)
    return pl.pallas_call(
        _memkv_kernel,
        out_shape=[jax.ShapeDtypeStruct((B, M, GROUP_WIDTH), BF16)] * 2,
        grid=(B,),
        in_specs=[pl.BlockSpec((1, M, D_MODEL), lambda b: (b, 0, 0)),
                  pl.BlockSpec((1, D_MODEL), const),
                  pl.BlockSpec((D_MODEL, 2 * GROUP_WIDTH), const),
                  pl.BlockSpec((1, GROUP_WIDTH), const),
                  pl.BlockSpec((LANES, LANES), const)],
        out_specs=[pl.BlockSpec((1, M, GROUP_WIDTH), lambda b: (b, 0, 0))] * 2,
        compiler_params=_params("parallel"),
        name="memory_kv",
    )(mem, ln_mem.reshape(1, D_MODEL), wkv.astype(BF16),
      jnp.tile(gain_k.astype(F32), N_HEADS).reshape(1, GROUP_WIDTH), _group_matrix(LANES, HEAD_DIM, True))


def _out_xattn_kernel(ya_ref, yb_ref, yc_ref, yd_ref, h_ref, wout_ref, lnx_ref, wq_ref, kx_ref, vx_ref, gq_ref, gm_ref,
                      wo_ref, o_ref):
    y = jnp.concatenate([ya_ref[...], yb_ref[...], yc_ref[...], yd_ref[...]], axis=1)
    h1 = h_ref[...] + _dot(y, wout_ref[...])
    hn = (h1 * lax.rsqrt(jnp.mean(h1 * h1, axis=-1, keepdims=True) + EPS) * lnx_ref[...]).astype(BF16)
    q = _dot(hn, wq_ref[...])
    chunks = []
    for c in range(GROUP_WIDTH // LANES):
        cs = slice(c * LANES, (c + 1) * LANES)
        chunks.append((_group_rms(q[:, cs], gm_ref[...]) * gq_ref[:, cs]).astype(BF16))
    qn = jnp.concatenate(chunks, axis=1)
    kx, vx = kx_ref[0], vx_ref[0]
    outs = []
    for h in range(N_HEADS):
        hs = slice(h * HEAD_DIM, (h + 1) * HEAD_DIM)
        s = _dot_nt(qn[:, hs], kx[:, hs]) * HEAD_DIM ** -0.5
        e = jnp.exp(s - jnp.max(s, axis=-1, keepdims=True))
        p = e / jnp.sum(e, axis=-1, keepdims=True)
        outs.append(_dot(p.astype(BF16), vx[:, hs]))
    o = jnp.concatenate(outs, axis=1).astype(BF16)
    o_ref[...] = h1 + _dot(o, wo_ref[...])


def _out_xattn(ys, h2d, w_out, ln_x, wq, kx, vx, gain_q, wo, B, S, tm=512):
    tm = min(tm, S)
    ns = S // tm
    M = kx.shape[1]
    row = lambda b, i: (b * ns + i, 0)
    const = lambda b, i: (0, 0)
    return pl.pallas_call(
        _out_xattn_kernel,
        out_shape=jax.ShapeDtypeStruct((B * S, D_MODEL), F32),
        grid=(B, ns),
        in_specs=[pl.BlockSpec((tm, GROUP_WIDTH), row)] * 4 + [
            pl.BlockSpec((tm, D_MODEL), row),
            pl.BlockSpec((D_MODEL, D_MODEL), const),
            pl.BlockSpec((1, D_MODEL), const),
            pl.BlockSpec((D_MODEL, GROUP_WIDTH), const),
            pl.BlockSpec((1, M, GROUP_WIDTH), lambda b, i: (b, 0, 0)),
            pl.BlockSpec((1, M, GROUP_WIDTH), lambda b, i: (b, 0, 0)),
            pl.BlockSpec((1, GROUP_WIDTH), const),
            pl.BlockSpec((LANES, LANES), const),
            pl.BlockSpec((GROUP_WIDTH, D_MODEL), const)],
        out_specs=pl.BlockSpec((tm, D_MODEL), row),
        compiler_params=_params("parallel", "parallel"),
        name="outproj_xattn",
    )(*ys, h2d, w_out.astype(BF16), ln_x.reshape(1, D_MODEL), wq.astype(BF16), kx, vx,
      jnp.tile(gain_q.astype(F32), N_HEADS).reshape(1, GROUP_WIDTH), _group_matrix(LANES, HEAD_DIM, True),
      wo.astype(BF16))


def _moe_route_kernel(h_ref, ln_ref, wr_ref, br_ref, tri_ref, hn_ref, comb_ref, rkt_ref, cnt_ref):
    tm = h_ref.shape[0]
    lane = lax.broadcasted_iota(jnp.int32, (tm, LANES), 1)
    x = h_ref[...]
    hn = x * lax.rsqrt(jnp.mean(x * x, axis=-1, keepdims=True) + EPS) * ln_ref[...]
    hn_ref[...] = hn.astype(BF16)
    logits = _dot(hn, wr_ref[...], HI) + br_ref[...]
    lanef = lane.astype(F32)
    big = 1e4
    isg = (lane >= MOE_EXPERTS) & (lane < MOE_EXPERTS + MOE_GROUPS)
    lg = jnp.where(isg, logits, NEG)
    gmax = jnp.max(lg, axis=-1, keepdims=True)
    grp_p = 1.0 / jnp.sum(jnp.exp(lg - gmax), axis=-1, keepdims=True)
    gidx = jnp.min(jnp.where(lg == gmax, lanef, big), axis=-1, keepdims=True) - MOE_EXPERTS
    ing = (lane < MOE_EXPERTS) & ((lane >> 3).astype(F32) == gidx)
    le = jnp.where(ing, logits, NEG)
    m1 = jnp.max(le, axis=-1, keepdims=True)
    z = jnp.sum(jnp.where(ing, jnp.exp(le - m1), 0.0), axis=-1, keepdims=True)
    i1 = jnp.min(jnp.where(le == m1, lanef, big), axis=-1, keepdims=True)
    oh1 = lanef == i1
    le2 = jnp.where(oh1, NEG, le)
    m2 = jnp.max(le2, axis=-1, keepdims=True)
    i2 = jnp.min(jnp.where((le2 == m2) & ing, jnp.where(oh1, big, lanef), big), axis=-1, keepdims=True)
    oh2 = lanef == i2
    p1 = 1.0 / z
    p2 = jnp.exp(m2 - m1) / z
    tot = p1 + p2
    comb_ref[...] = jnp.where(oh1, p1 / tot * grp_p, 0.0) + jnp.where(oh2, p2 / tot * grp_p, 0.0)
    member = jnp.where(lanef == gidx, 1.0, 0.0)
    rank = _dot(tri_ref[...], member.astype(BF16))
    rkt_ref[0] = jnp.where(member > 0.5, rank, -1.0).T[0:8, :]
    cnt_ref[0] = jnp.broadcast_to(jnp.sum(member, axis=0, keepdims=True), (8, LANES))


def _moe_group_kernel(cnt_ref, hn_ref, comb_ref, rkt_ref, prev_ref, w1_ref, w3_ref, w2_ref, o_ref, acc_sc,
                      *, group, tm, ch):
    cnt = cnt_ref[pl.program_id(0) * MOE_GROUPS + group]
    acc_sc[...] = jnp.zeros(acc_sc.shape, F32)
    rkg = rkt_ref[0][group:group + 1, :]
    starts = [(0, ch)] + [(s, ch // 2) for s in range(ch, tm, ch // 2)]
    for start, ch in starts:
        @pl.when(start < cnt)
        def _():
            rows = (start + lax.broadcasted_iota(jnp.int32, (ch, 1), 0)).astype(F32)
            sel = jnp.where(rkg == rows, 1.0, 0.0).astype(BF16)
            xg = _dot(sel, hn_ref[...]).astype(BF16)
            cg = _c_dot(sel, comb_ref[...])
            yg = jnp.zeros((ch, D_MODEL), F32)
            for e in range(MOE_PER_GROUP):
                hg = _dot(xg, w1_ref[e])
                hu = _dot(xg, w3_ref[e])
                ce = cg[:, group * MOE_PER_GROUP + e:group * MOE_PER_GROUP + e + 1]
                yg = yg + _dot((jax.nn.silu(hg) * hu * ce).astype(BF16), w2_ref[e])
            acc_sc[...] += _dot_tn(sel, yg.astype(BF16))
    o_ref[...] = prev_ref[...] + acc_sc[...]


def _moe(h2d, ln, w_group, b_group, w_expert, b_expert, w1, w3, w2, tm=1024, ch=256):
    T = h2d.shape[0]
    tm = min(tm, T)
    nt = T // tm
    pad = LANES - MOE_EXPERTS - MOE_GROUPS
    wr = jnp.concatenate([w_expert, w_group, jnp.zeros((D_MODEL, pad), F32)], axis=1)
    br = jnp.concatenate([b_expert, b_group, jnp.zeros((pad,), F32)]).reshape(1, LANES)
    tri = jnp.asarray(np.tril(np.ones((tm, tm), np.float32), -1), BF16)
    const = lambda i: (0, 0)
    hn, comb, rkt, cnt = pl.pallas_call(
        _moe_route_kernel,
        out_shape=[jax.ShapeDtypeStruct((T, D_MODEL), BF16),
                   jax.ShapeDtypeStruct((T, LANES), F32),
                   jax.ShapeDtypeStruct((nt, 8, tm), F32),
                   jax.ShapeDtypeStruct((nt, 8, LANES), F32)],
        grid=(nt,),
        in_specs=[pl.BlockSpec((tm, D_MODEL), lambda i: (i, 0)),
                  pl.BlockSpec((1, D_MODEL), const),
                  pl.BlockSpec((D_MODEL, LANES), const),
                  pl.BlockSpec((1, LANES), const),
                  pl.BlockSpec((tm, tm), const)],
        out_specs=[pl.BlockSpec((tm, D_MODEL), lambda i: (i, 0)),
                   pl.BlockSpec((tm, LANES), lambda i: (i, 0)),
                   pl.BlockSpec((1, 8, tm), lambda i: (i, 0, 0)),
                   pl.BlockSpec((1, 8, LANES), lambda i: (i, 0, 0))],
        compiler_params=_params("parallel"),
        name="moe_route",
    )(h2d, ln.reshape(1, D_MODEL), wr, br, tri)
    counts = cnt[:, 0, :MOE_GROUPS].astype(jnp.int32).reshape(nt * MOE_GROUPS)
    w1b, w3b, w2b = w1.astype(BF16), w3.astype(BF16), w2.astype(BF16)
    out = h2d
    for g in range(MOE_GROUPS):
        wspec = lambda shape: pl.BlockSpec((MOE_PER_GROUP,) + shape, lambda i, c, g=g: (g, 0, 0),
                                           pipeline_mode=pl.Buffered(1))
        out = pl.pallas_call(
            functools.partial(_moe_group_kernel, group=g, tm=tm, ch=ch),
            out_shape=jax.ShapeDtypeStruct((T, D_MODEL), F32),
            grid_spec=pltpu.PrefetchScalarGridSpec(
                num_scalar_prefetch=1,
                grid=(nt,),
                in_specs=[pl.BlockSpec((tm, D_MODEL), lambda i, c: (i, 0)),
                          pl.BlockSpec((tm, LANES), lambda i, c: (i, 0)),
                          pl.BlockSpec((1, 8, tm), lambda i, c: (i, 0, 0)),
                          pl.BlockSpec((tm, D_MODEL), lambda i, c: (i, 0)),
                          wspec((D_MODEL, MOE_FF)), wspec((D_MODEL, MOE_FF)), wspec((MOE_FF, D_MODEL))],
                out_specs=pl.BlockSpec((tm, D_MODEL), lambda i, c: (i, 0)),
                scratch_shapes=[pltpu.VMEM((tm, D_MODEL), F32)]),
            compiler_params=_params("parallel"),
            name="moe_group",
        )(counts, hn, comb, rkt, out, w1b, w3b, w2b)
    return out


def _nsa_mixer(a_q, a_kv, small, qk_gain, cmp_pe, cmp_w1, cmp_w2, cos, sin, B, S):
    nc = S // NSA_CMP_STRIDE
    wide = NSA_CMP_STRIDE * HEAD_DIM
    gain_k = jnp.concatenate([qk_gain[2], qk_gain[3]]).reshape(1, LANES).astype(F32)
    kn = _normrope(a_kv, gain_k, HEAD_DIM, cos, sin, LANES, B, S)
    cos_c, sin_c = _rope_tables(NSA_CMP_STRIDE * np.arange(nc) + NSA_CMP_LEN - 1, HEAD_DIM)
    half = HEAD_DIM // 2
    kc, vc = _nsa_compress(a_kv, cmp_pe.reshape(2, 2 * wide), cmp_w1, cmp_w2, qk_gain[1],
                           cos_c[:, :half], sin_c[:, half:2 * half], B, nc)
    gq_row = jnp.tile(qk_gain[0].astype(F32), N_HEADS).reshape(1, GROUP_WIDTH)
    qn, part, sel = _nsa1(a_q, small, kc, vc, kn, a_kv, cos, sin, gq_row, B, S)
    return _nsa2(qn, sel, kn, a_kv, part, small, B, S)


def kernel(x, mem, ln_mix, w_in, w_out, nsa_qk_gain, nsa_cmp_pe, nsa_cmp_w1, nsa_cmp_w2, gdn_conv, gdn_a_log, gdn_dt_bias, gdn_norm, mlstm_i_bias, mlstm_f_bias, mlstm_norm, diff_qk_gain, diff_lambda, diff_norm, ln_xattn, ln_mem, xattn_wq, xattn_wkv, xattn_qk_gain, xattn_wo, ln_moe, moe_w_group, moe_b_group, moe_w_expert, moe_b_expert, moe_w1, moe_w3, moe_w2):
    B, S, D = x.shape
    depth = w_in.shape[0]
    cos_a, sin_a = _rope_tables(np.arange(S), HEAD_DIM)
    cos_d, sin_d = _rope_tables(np.arange(S), DIFF_SUB)
    h = x.reshape(B * S, D)
    for l in range(depth):
        a_q, a_kv, b_qkv, b_z, c_qk, c_v, c_o, d_qk, d_v, small = _inproj(h, ln_mix[l], w_in[l])
        y_a = _nsa_mixer(a_q, a_kv, small, nsa_qk_gain[l], nsa_cmp_pe[l], nsa_cmp_w1[l], nsa_cmp_w2[l],
                         cos_a, sin_a, B, S)
        y_b = _gdn(b_qkv, b_z, small, gdn_conv[l], gdn_a_log[l], gdn_dt_bias[l], gdn_norm[l], B, S)
        y_c = _mlstm(c_qk, c_v, c_o, small, mlstm_i_bias[l], mlstm_f_bias[l], mlstm_norm[l], B, S)
        lambda_init = 0.8 - 0.6 * math.exp(-0.3 * l)
        gain_d = jnp.concatenate([jnp.tile(diff_qk_gain[l, 0], 2 * N_HEADS) * DIFF_SUB ** -0.5,
                                  jnp.tile(diff_qk_gain[l, 1], 2 * N_HEADS)]).reshape(1, 2 * GROUP_WIDTH).astype(F32)
        dqk = _normrope(d_qk, gain_d, DIFF_SUB, cos_d, sin_d, 2 * GROUP_WIDTH, B, S)
        y_d = _diff_attention(dqk, d_v, diff_lambda[l], diff_norm[l], lambda_init, B, S)
        kx, vx = _memkv(mem, ln_mem[l], xattn_wkv[l], xattn_qk_gain[l, 1])
        h = _out_xattn((y_a, y_b, y_c, y_d), h, w_out[l], ln_xattn[l], xattn_wq[l], kx, vx,
                       xattn_qk_gain[l, 0], xattn_wo[l], B, S)
        h = _moe(h, ln_moe[l], moe_w_group[l], moe_b_group[l], moe_w_expert[l], moe_b_expert[l],
                 moe_w1[l], moe_w3[l], moe_w2[l])
    return h.reshape(B, S, D)
```

```python
import functools
import math

import numpy as np
import jax
import jax.numpy as jnp
from jax import lax
from jax.experimental import pallas as pl
from jax.experimental.pallas import tpu as pltpu

F32 = jnp.float32
BF16 = jnp.bfloat16
HI = lax.Precision.HIGHEST

D_MODEL = 1024
HEAD_DIM = 64
N_HEADS = 4
GROUP_WIDTH = 256
ROPE_THETA = 10000.0
EPS = 1e-6
NEG = -1e30

NSA_CMP_LEN = 32
NSA_CMP_STRIDE = 16
NSA_SEL_LEN = 64
NSA_TOP_N = 16
NSA_WINDOW = 512
NSA_FORCE_BONUS = 1e3
CHUNK = 64
MLSTM_QK = 32
GATE_CAP = 15.0
DIFF_SUB = 32
MOE_GROUPS = 4
MOE_PER_GROUP = 8
MOE_EXPERTS = 32
MOE_FF = 256
LANES = 128
VMEM_LIMIT = 48 * 1024 * 1024

_A0, _B0, _C0, _D0 = 0, 652, 1684, 2460
_GROUPS = (
    ("a_q", ((_A0, 256),), F32),
    ("a_kv", ((_A0 + 384, 64), (_A0 + 512, 64), (_A0 + 448, 64), (None, 64), (_A0 + 576, 64), (None, 64),
              (_A0 + 256, 64), (_A0 + 320, 64)), F32),
    ("b_qkv", ((_B0, 768),), F32),
    ("b_z", ((_B0 + 776, 256),), F32),
    ("c_qk", ((_C0, 256),), F32),
    ("c_v", ((_C0 + 256, 256),), F32),
    ("c_o", ((_C0 + 520, 256),), F32),
    ("d_qk", ((_D0, 512),), F32),
    ("d_v", tuple(p for h in range(N_HEADS) for p in ((_D0 + 512 + 64 * h, 64), (None, 64))), BF16),
    ("small", ((_A0 + 640, 12), (_B0 + 768, 4), (_B0 + 772, 4), (_C0 + 512, 4), (_C0 + 516, 4), (None, 100)), F32),
)
_LANE_GDN_A, _LANE_GDN_B, _LANE_I, _LANE_F = 12, 16, 20, 24


def _dot(a, b, prec=None):
    return jnp.dot(a, b, preferred_element_type=F32, precision=prec)


def _dot_nt(a, b, prec=None):
    return lax.dot_general(a, b, (((1,), (1,)), ((), ())), preferred_element_type=F32, precision=prec)


def _dot_tn(a, b, prec=None):
    return lax.dot_general(a, b, (((0,), (0,)), ((), ())), preferred_element_type=F32, precision=prec)


def _bdot(a, b):
    return _dot(a.astype(BF16), b.astype(BF16))


def _bdot_nt(a, b):
    return _dot_nt(a.astype(BF16), b.astype(BF16))


def _bdot_tn(a, b):
    return _dot_tn(a.astype(BF16), b.astype(BF16))


def _split3(a):
    hi = a.astype(BF16)
    r = a - hi.astype(F32)
    mid = r.astype(BF16)
    return hi, mid, (r - mid.astype(F32)).astype(BF16)


def _dot_c(a, c):
    hi, mid, lo = _split3(a)
    return _dot(hi, c) + _dot(mid, c) + _dot(lo, c)


def _c_dot(c, b):
    hi, mid, lo = _split3(b)
    return _dot(c, hi) + _dot(c, mid) + _dot(c, lo)


def _dot_nt_x3(a, b):
    ah, bh = a.astype(BF16), b.astype(BF16)
    al, bl = (a - ah.astype(F32)).astype(BF16), (b - bh.astype(F32)).astype(BF16)
    return _dot_nt(ah, bh) + _dot_nt(ah, bl) + _dot_nt(al, bh)


def _params(*sem):
    return pltpu.CompilerParams(dimension_semantics=sem, vmem_limit_bytes=VMEM_LIMIT)


def _group_matrix(width, gsz, mean):
    g = np.kron(np.eye(width // gsz), np.ones((gsz, gsz)))
    return jnp.asarray(g / gsz if mean else g, BF16)


def _expand_matrix(src_lane0, n, out_per, stride=1):
    e = np.zeros((LANES, n * out_per), np.float32)
    for h in range(n):
        e[src_lane0 + stride * h, h * out_per:(h + 1) * out_per] = 1.0
    return jnp.asarray(e, BF16)


def _row128(vals, lane0):
    return jnp.zeros((1, LANES), F32).at[0, lane0:lane0 + vals.shape[0]].set(vals.astype(F32))


def _ones_col_row():
    return jnp.zeros((1, LANES), BF16).at[0, HEAD_DIM].set(1.0)


def _rope_tables(pos, dim):
    inv = 1.0 / (ROPE_THETA ** (jnp.arange(0, dim, 2, dtype=F32) / dim))
    ang = jnp.asarray(pos).astype(F32)[:, None] * inv[None, :]
    cos, sin = jnp.cos(ang), jnp.sin(ang)
    cosd = jnp.concatenate([cos, cos], axis=-1)
    sind = jnp.concatenate([-sin, sin], axis=-1)
    rep = LANES // dim
    return jnp.tile(cosd, (1, rep)), jnp.tile(sind, (1, rep))


def _rope128(x, cos, sin_signed, half):
    left = pltpu.roll(x, LANES - half, 1)
    right = pltpu.roll(x, half, 1)
    lane = lax.broadcasted_iota(jnp.int32, x.shape, 1)
    first = (lane & (2 * half - 1)) < half
    return x * cos + jnp.where(first, left, right) * sin_signed


def _rep_lanes(x, width):
    return x if width == LANES else jnp.concatenate([x] * (width // LANES), axis=1)


def _softmax_rows(s, mask):
    m = jnp.max(s, axis=-1, keepdims=True)
    e = jnp.exp(s - m)
    return jnp.where(mask, e / jnp.sum(e, axis=-1, keepdims=True), 0.0)


def _group_rms(x, gm):
    return x * lax.rsqrt(_dot_c(x * x, gm) + EPS)


def _flash_update(items, m_ref, acc_ref, tk):
    m_prev = [m_ref[i] for _, _, i in items]
    m_new = [jnp.maximum(mp, jnp.max(s, axis=-1, keepdims=True)) for mp, (s, _, _) in zip(m_prev, items)]
    p = [jnp.exp((s - _rep_lanes(mn, tk)).astype(BF16)) for mn, (s, _, _) in zip(m_new, items)]
    pv = [_dot(pp, va) for pp, (_, va, _) in zip(p, items)]
    for mp, mn, x, (_, _, i) in zip(m_prev, m_new, pv, items):
        acc_ref[i] = jnp.exp(mp - mn) * acc_ref[i] + x
        m_ref[i] = mn


def _flash_result(acc_ref, idx):
    acc = acc_ref[idx]
    return acc[:, :HEAD_DIM] / acc[:, HEAD_DIM:HEAD_DIM + 1]


def _normrope(y, gain, gm, cos, sin, half):
    out = []
    for c in range(y.shape[1] // LANES):
        cs = slice(c * LANES, (c + 1) * LANES)
        out.append(_rope128(_group_rms(y[:, cs], gm) * gain[:, cs], cos, sin, half))
    return out[0] if len(out) == 1 else jnp.concatenate(out, axis=1)


def _inproj_kernel(x_ref, g_ref, w_ref, gk_ref, gd_ref, gm64_ref, gm32_ref, cosa_ref, sina_ref, cosd_ref, sind_ref,
                   *outs, widths):
    x = x_ref[...]
    ms = jnp.mean(x * x, axis=-1, keepdims=True)
    xn = (x * lax.rsqrt(ms + EPS) * g_ref[...]).astype(BF16)
    outs = list(outs)
    kn_ref, dqk_ref = outs.pop(), outs.pop()
    off = 0
    for (name, _, _), wd in zip(_GROUPS, widths):
        y = _dot(xn, w_ref[:, off:off + wd])
        off += wd
        if name == "d_qk":
            dqk_ref[...] = _normrope(y, gd_ref[...], gm32_ref[...], cosd_ref[...], sind_ref[...],
                                     DIFF_SUB // 2).astype(BF16)
            continue
        o = outs.pop(0)
        o[...] = y.astype(o.dtype)
        if name == "a_kv":
            kn_ref[...] = _normrope(y[:, :LANES], gk_ref[...], gm64_ref[...], cosa_ref[...], sina_ref[...],
                                    HEAD_DIM // 2).astype(BF16)


def _permute_w_in(w):
    cols, widths = [], []
    for _, parts, _ in _GROUPS:
        for s, n in parts:
            cols.append(jnp.zeros((w.shape[0], n), w.dtype) if s is None else w[:, s:s + n])
        widths.append(sum(n for _, n in parts))
    return jnp.concatenate(cols, axis=1).astype(BF16), tuple(widths)


def _inproj(h2d, gain, w_in, gain_k, gain_d, cos_a, sin_a, cos_d, sin_d, S, tm=256):
    T = h2d.shape[0]
    tm = min(tm, S)
    ns = S // tm
    wp, widths = _permute_w_in(w_in)
    kept = [(wd, g) for wd, g in zip(widths, _GROUPS) if g[0] != "d_qk"]
    out_shape = [jax.ShapeDtypeStruct((T, wd), g[2]) for wd, g in kept]
    out_shape += [jax.ShapeDtypeStruct((T, 2 * GROUP_WIDTH), BF16), jax.ShapeDtypeStruct((T, LANES), BF16)]
    const = lambda i: (0, 0)
    table = pl.BlockSpec((tm, LANES), lambda i: (i % ns, 0))
    return pl.pallas_call(
        functools.partial(_inproj_kernel, widths=widths),
        out_shape=out_shape,
        grid=(T // tm,),
        in_specs=[pl.BlockSpec((tm, D_MODEL), lambda i: (i, 0)),
                  pl.BlockSpec((1, D_MODEL), const),
                  pl.BlockSpec((D_MODEL, sum(widths)), const),
                  pl.BlockSpec((1, LANES), const),
                  pl.BlockSpec((1, 2 * GROUP_WIDTH), const),
                  pl.BlockSpec((LANES, LANES), const),
                  pl.BlockSpec((LANES, LANES), const),
                  table, table, table, table],
        out_specs=[pl.BlockSpec((tm, s.shape[1]), lambda i: (i, 0)) for s in out_shape],
        compiler_params=_params("parallel"),
        name="inproj",
    )(h2d, gain.reshape(1, D_MODEL), wp, gain_k, gain_d, _group_matrix(LANES, HEAD_DIM, True),
      _group_matrix(LANES, DIFF_SUB, True), cos_a, sin_a, cos_d, sin_d)


def _nsa_cmp_kernel(kv_ref, pe_ref, w1_ref, w2_ref, gain_ref, cos_ref, sin_ref, kc_ref, vc_ref, *, nc):
    half_in = NSA_CMP_STRIDE * HEAD_DIM

    a = [jnp.zeros((nc, 2 * HEAD_DIM), F32) for _ in range(2)]
    b = [jnp.zeros((nc, 2 * HEAD_DIM), F32) for _ in range(2)]
    for l in range(NSA_CMP_STRIDE):
        xl = kv_ref[pl.ds(l, nc, stride=NSA_CMP_STRIDE), :]
        for j in range(2):
            xj = xl[:, j * HEAD_DIM:(j + 1) * HEAD_DIM]
            a[j] = a[j] + _dot(xj, w1_ref[j, l * HEAD_DIM:(l + 1) * HEAD_DIM, :], HI)
            b[j] = b[j] + _dot(xj, w1_ref[j, half_in + l * HEAD_DIM:half_in + (l + 1) * HEAD_DIM, :], HI)

    def finish(j):
        pe = jnp.broadcast_to(pe_ref[j], (8, 2 * half_in))
        c = _dot(pe, w1_ref[j], HI)[0:1]
        hid = jax.nn.gelu(a[j] + pltpu.roll(b[j], nc - 1, 0) + c)
        return _dot(hid, w2_ref[j], HI)

    kc = finish(0)
    kc = kc * lax.rsqrt(jnp.mean(kc * kc, axis=-1, keepdims=True) + EPS) * gain_ref[...]
    x1, x2 = kc[:, :HEAD_DIM // 2], kc[:, HEAD_DIM // 2:]
    cos, sin = cos_ref[...], sin_ref[...]
    kc_ref[0] = jnp.concatenate([x1 * cos - x2 * sin, x2 * cos + x1 * sin], axis=1)
    vc_ref[0] = finish(1)


def _nsa_compress(a_kv, pe, w1, w2, gain, cos_c, sin_c, B, nc):
    wide = NSA_CMP_STRIDE * HEAD_DIM
    return pl.pallas_call(
        functools.partial(_nsa_cmp_kernel, nc=nc),
        out_shape=[jax.ShapeDtypeStruct((B, nc, HEAD_DIM), F32)] * 2,
        grid=(B,),
        in_specs=[pl.BlockSpec((nc * NSA_CMP_STRIDE, LANES), lambda b: (b, 3)),
                  pl.BlockSpec((2, 1, 2 * wide), lambda b: (0, 0, 0)),
                  pl.BlockSpec((2, 2 * wide, 2 * HEAD_DIM), lambda b: (0, 0, 0)),
                  pl.BlockSpec((2, 2 * HEAD_DIM, HEAD_DIM), lambda b: (0, 0, 0)),
                  pl.BlockSpec((1, HEAD_DIM), lambda b: (0, 0)),
                  pl.BlockSpec((nc, HEAD_DIM // 2), lambda b: (0, 0)),
                  pl.BlockSpec((nc, HEAD_DIM // 2), lambda b: (0, 0))],
        out_specs=[pl.BlockSpec((1, nc, HEAD_DIM), lambda b: (b, 0, 0))] * 2,
        compiler_params=_params("parallel"),
        name="nsa_compress",
    )(a_kv, pe.reshape(2, 1, 2 * wide), w1, w2, gain.reshape(1, HEAD_DIM), cos_c, sin_c)


def _nsa1_kernel(q_ref, sm_ref, kc_ref, vc_ref, kn_ref, vw_ref, cos_ref, sin_ref, gq_ref, gm_ref, ovl_ref,
                 eg0_ref, eg2_ref, qn_ref, part_ref, sel_ref, *, tq, nc, n_sel):
    t0 = pl.program_id(1) * tq
    chunks = []
    for c in range(GROUP_WIDTH // LANES):
        cs = slice(c * LANES, (c + 1) * LANES)
        xn = _group_rms(q_ref[:, cs], gm_ref[...]) * gq_ref[:, cs]
        chunks.append(_rope128(xn, cos_ref[...], sin_ref[...], HEAD_DIM // 2))
    qs = jnp.concatenate(chunks, axis=1) * HEAD_DIM ** -0.5
    qb = qs.astype(BF16)
    qn_ref[...] = qb
    sig = jax.nn.sigmoid(sm_ref[...])
    g0x = _dot_c(sig, eg0_ref[...])
    g2x = _dot_c(sig, eg2_ref[...])
    tpos = t0 + lax.broadcasted_iota(jnp.int32, (tq, 1), 0)

    kc = kc_ref[0]
    vc = vc_ref[0].astype(BF16)
    cidx = lax.broadcasted_iota(jnp.int32, (1, nc), 1)
    cmask = ((NSA_CMP_STRIDE * cidx + NSA_CMP_LEN - 1) <= tpos) & (cidx < nc - 1)
    psum = jnp.zeros((tq, nc), F32)
    o_cmp = []
    for h in range(N_HEADS):
        hs = slice(h * HEAD_DIM, (h + 1) * HEAD_DIM)
        p = _softmax_rows(jnp.where(cmask, _dot_nt_x3(qs[:, hs], kc), NEG), cmask)
        o_cmp.append(_dot(p.astype(BF16), vc))
        psum = psum + p

    imp = _dot_c(psum, ovl_ref[...])
    j = lax.broadcasted_iota(jnp.int32, (tq, LANES), 1)
    cur = tpos >> 6
    valid = j <= cur
    forced = (j == 0) | (j == cur) | (j == cur - 1)
    score = jnp.where(valid, imp + jnp.where(forced, NSA_FORCE_BONUS, 0.0), NEG)
    nrow = -(-n_sel // 8) * 8
    st = score.T[0:nrow, :]
    blocks = [st[8 * g:8 * g + 8, :] for g in range(nrow // 8)]
    cnts = [jnp.zeros((8, tq), F32) for _ in blocks]
    j8 = lax.broadcasted_iota(jnp.int32, (8, 1), 0)
    for i in range(n_sel):
        row = st[i:i + 1, :]
        for g, blk in enumerate(blocks):
            if 8 * g > i:
                won = jnp.where(row >= blk, 1.0, 0.0)
            elif 8 * g + 7 < i:
                won = jnp.where(row > blk, 1.0, 0.0)
            else:
                tie = jnp.where(j8 + 8 * g > i, 1.0, 0.0)
                won = jnp.where(row > blk, 1.0, jnp.where(row == blk, tie, 0.0))
            cnts[g] = cnts[g] + won
    cnt = jnp.concatenate(cnts, axis=0)
    sel_t = jnp.where((cnt < min(NSA_TOP_N, n_sel)) & (st > 0.5 * NEG), 1.0, 0.0)
    if nrow < LANES:
        sel_t = jnp.concatenate([sel_t, jnp.zeros((LANES - nrow, tq), F32)], axis=0)
    sel_ref[...] = sel_t.T.astype(BF16)

    band = tq + NSA_WINDOW
    start = pl.multiple_of(jnp.maximum(t0 - NSA_WINDOW, 0), LANES)
    kw = kn_ref[pl.ds(start, band), HEAD_DIM:2 * HEAD_DIM]
    one_row = jnp.where(lax.broadcasted_iota(jnp.int32, (1, LANES), 1) == HEAD_DIM, 1.0, 0.0).astype(BF16)
    vaug = vw_ref[pl.ds(start, band), :].astype(BF16) + one_row
    dist = tpos - (start + lax.broadcasted_iota(jnp.int32, (1, band), 1))
    wmask = (dist >= 0) & (dist < NSA_WINDOW)
    s_win = [jnp.where(wmask, _dot_nt(qb[:, h * HEAD_DIM:(h + 1) * HEAD_DIM], kw), NEG) for h in range(N_HEADS)]
    p_win = [jnp.exp((s - jnp.max(s, axis=-1, keepdims=True)).astype(BF16)) for s in s_win]
    pv = [_dot(p, vaug) for p in p_win]
    o_win = [x[:, :HEAD_DIM] / x[:, HEAD_DIM:HEAD_DIM + 1] for x in pv]
    part_ref[...] = g0x * jnp.concatenate(o_cmp, axis=1) + g2x * jnp.concatenate(o_win, axis=1)


def _nsa_overlap(nc, n_sel):
    c0 = NSA_CMP_STRIDE * np.arange(nc)[:, None]
    s0 = NSA_SEL_LEN * np.arange(n_sel)[None, :]
    ov = np.clip(np.minimum(c0 + NSA_CMP_LEN, s0 + NSA_SEL_LEN) - np.maximum(c0, s0), 0, None) / NSA_CMP_STRIDE
    ov[nc - 1:] = 0.0
    out = np.zeros((nc, LANES), np.float32)
    out[:, :n_sel] = ov
    return jnp.asarray(out, BF16)


def _nsa1(a_q, small, kc, vc, kn, a_kv, cos, sin, gq_row, B, S, tq=256):
    tq = min(tq, S)
    nq = S // tq
    nc = S // NSA_CMP_STRIDE
    n_sel = S // NSA_SEL_LEN
    row = lambda b, i: (b * nq + i, 0)
    return pl.pallas_call(
        functools.partial(_nsa1_kernel, tq=tq, nc=nc, n_sel=n_sel),
        out_shape=[jax.ShapeDtypeStruct((B * S, GROUP_WIDTH), BF16),
                   jax.ShapeDtypeStruct((B * S, GROUP_WIDTH), F32),
                   jax.ShapeDtypeStruct((B * S, LANES), BF16)],
        grid=(B, nq),
        in_specs=[pl.BlockSpec((tq, GROUP_WIDTH), row),
                  pl.BlockSpec((tq, LANES), row),
                  pl.BlockSpec((1, nc, HEAD_DIM), lambda b, i: (b, 0, 0)),
                  pl.BlockSpec((1, nc, HEAD_DIM), lambda b, i: (b, 0, 0)),
                  pl.BlockSpec((S, LANES), lambda b, i: (b, 0)),
                  pl.BlockSpec((S, LANES), lambda b, i: (b, 2)),
                  pl.BlockSpec((tq, LANES), lambda b, i: (i, 0)),
                  pl.BlockSpec((tq, LANES), lambda b, i: (i, 0)),
                  pl.BlockSpec((1, GROUP_WIDTH), lambda b, i: (0, 0)),
                  pl.BlockSpec((LANES, LANES), lambda b, i: (0, 0)),
                  pl.BlockSpec((nc, LANES), lambda b, i: (0, 0)),
                  pl.BlockSpec((LANES, GROUP_WIDTH), lambda b, i: (0, 0)),
                  pl.BlockSpec((LANES, GROUP_WIDTH), lambda b, i: (0, 0))],
        out_specs=[pl.BlockSpec((tq, GROUP_WIDTH), row),
                   pl.BlockSpec((tq, GROUP_WIDTH), row),
                   pl.BlockSpec((tq, LANES), row)],
        compiler_params=_params("parallel", "parallel"),
        name="nsa_cmp_win_select",
    )(a_q, small, kc, vc, kn, a_kv, cos, sin, gq_row, _group_matrix(LANES, HEAD_DIM, True),
      _nsa_overlap(nc, n_sel), _expand_matrix(0, N_HEADS, HEAD_DIM, 3), _expand_matrix(2, N_HEADS, HEAD_DIM, 3))


def _nsa2_kernel(qn_ref, sel_ref, e_ref, kn_ref, vs_ref, one_ref, part_ref, sm_ref, eg1_ref, o_ref, m_sc, acc_sc,
                 *, tq, tk, nk):
    qi = pl.program_id(1)
    ki = pl.program_id(2)

    @pl.when(ki == 0)
    def _():
        m_sc[...] = jnp.full(m_sc.shape, NEG, F32)
        acc_sc[...] = jnp.zeros(acc_sc.shape, F32)

    def step(causal):
        mask = _dot(sel_ref[...], e_ref[...]) > 0.5
        if causal:
            tpos = qi * tq + lax.broadcasted_iota(jnp.int32, (tq, 1), 0)
            kpos = ki * tk + lax.broadcasted_iota(jnp.int32, (1, tk), 1)
            mask = mask & (kpos <= tpos)
        ks = kn_ref[:, 0:HEAD_DIM]
        vaug = vs_ref[...].astype(BF16) + one_ref[...]
        for h0 in range(0, N_HEADS, 2):
            _flash_update([(jnp.where(mask, _dot_nt(qn_ref[:, h * HEAD_DIM:(h + 1) * HEAD_DIM], ks), NEG), vaug, h)
                           for h in (h0, h0 + 1)], m_sc, acc_sc, tk)

    @pl.when(ki * tk + tk - 1 <= qi * tq)
    def _():
        step(False)

    @pl.when((ki * tk + tk - 1 > qi * tq) & (ki * tk <= qi * tq + tq - 1))
    def _():
        step(True)

    @pl.when(ki == nk - 1)
    def _():
        g1x = _dot_c(jax.nn.sigmoid(sm_ref[...]), eg1_ref[...])
        o = jnp.concatenate([_flash_result(acc_sc, h) for h in range(N_HEADS)], axis=1)
        o_ref[...] = (part_ref[...] + g1x * o).astype(o_ref.dtype)


def _nsa2(qn, sel, kn, a_kv, part, small, B, S, tq=512, tk=512):
    tq, tk = min(tq, S), min(tk, S)
    nq, nk = S // tq, S // tk
    e = np.zeros((LANES, S), np.float32)
    e[np.arange(S) // NSA_SEL_LEN, np.arange(S)] = 1.0
    row = lambda b, i, k: (b * nq + i, 0)
    kclamp = lambda i, k: jnp.minimum(k, (i * tq + tq - 1) // tk)
    return pl.pallas_call(
        functools.partial(_nsa2_kernel, tq=tq, tk=tk, nk=nk),
        out_shape=jax.ShapeDtypeStruct((B * S, GROUP_WIDTH), BF16),
        grid=(B, nq, nk),
        in_specs=[pl.BlockSpec((tq, GROUP_WIDTH), row),
                  pl.BlockSpec((tq, LANES), row),
                  pl.BlockSpec((LANES, tk), lambda b, i, k: (0, kclamp(i, k))),
                  pl.BlockSpec((tk, LANES), lambda b, i, k: (b * nk + kclamp(i, k), 0)),
                  pl.BlockSpec((tk, LANES), lambda b, i, k: (b * nk + kclamp(i, k), 1)),
                  pl.BlockSpec((1, LANES), lambda b, i, k: (0, 0)),
                  pl.BlockSpec((tq, GROUP_WIDTH), row),
                  pl.BlockSpec((tq, LANES), row),
                  pl.BlockSpec((LANES, GROUP_WIDTH), lambda b, i, k: (0, 0))],
        out_specs=pl.BlockSpec((tq, GROUP_WIDTH), row),
        scratch_shapes=[pltpu.VMEM((N_HEADS, tq, LANES), F32),
                        pltpu.VMEM((N_HEADS, tq, LANES), F32)],
        compiler_params=_params("parallel", "parallel", "arbitrary"),
        name="nsa_selected",
    )(qn, sel, jnp.asarray(e, BF16), kn, a_kv, _ones_col_row(), part, small, _expand_matrix(1, N_HEADS, HEAD_DIM, 3))


def _diff_kernel(q_ref, k_ref, v_ref, one_ref, lam_ref, gain_ref, o_ref, m_sc, acc_sc, *, tq, tk, nk, lambda_init):
    qi = pl.program_id(1)
    ki = pl.program_id(2)

    @pl.when(ki == 0)
    def _():
        m_sc[...] = jnp.full(m_sc.shape, NEG, F32)
        acc_sc[...] = jnp.zeros(acc_sc.shape, F32)

    def step(causal):
        if causal:
            tpos = qi * tq + lax.broadcasted_iota(jnp.int32, (tq, 1), 0)
            kpos = ki * tk + lax.broadcasted_iota(jnp.int32, (1, tk), 1)
            mask = kpos <= tpos
        for h in range(N_HEADS):
            vaug = v_ref[:, h * LANES:(h + 1) * LANES] + one_ref[...]
            items = []
            for idx in (2 * h, 2 * h + 1):
                cs = slice(idx * DIFF_SUB, (idx + 1) * DIFF_SUB)
                s = _dot_nt(q_ref[:, cs], k_ref[:, cs])
                items.append((jnp.where(mask, s, NEG) if causal else s, vaug, idx))
            _flash_update(items, m_sc, acc_sc, tk)

    @pl.when(ki * tk + tk - 1 <= qi * tq)
    def _():
        step(False)

    @pl.when((ki * tk + tk - 1 > qi * tq) & (ki * tk <= qi * tq + tq - 1))
    def _():
        step(True)

    @pl.when(ki == nk - 1)
    def _():
        lm = lam_ref[...]
        lam = (jnp.exp(jnp.sum(lm[0:1] * lm[1:2], axis=-1, keepdims=True))
               - jnp.exp(jnp.sum(lm[2:3] * lm[3:4], axis=-1, keepdims=True)) + lambda_init)
        outs = []
        for h in range(N_HEADS):
            o = _flash_result(acc_sc, 2 * h) - lam * _flash_result(acc_sc, 2 * h + 1)
            o = o * lax.rsqrt(jnp.mean(o * o, axis=-1, keepdims=True) + EPS)
            outs.append(o * gain_ref[...] * (1.0 - lambda_init))
        o_ref[...] = jnp.concatenate(outs, axis=1).astype(o_ref.dtype)


def _diff_attention(qk, v, lam, norm_g, lambda_init, B, S, tq=512, tk=512):
    tq, tk = min(tq, S), min(tk, S)
    nq, nk = S // tq, S // tk
    lam_pad = jnp.zeros((4, LANES), F32).at[:, :DIFF_SUB].set(lam.astype(F32))
    kclamp = lambda i, k: jnp.minimum(k, (i * tq + tq - 1) // tk)
    const = lambda b, i, k: (0, 0)
    return pl.pallas_call(
        functools.partial(_diff_kernel, tq=tq, tk=tk, nk=nk, lambda_init=lambda_init),
        out_shape=jax.ShapeDtypeStruct((B * S, GROUP_WIDTH), BF16),
        grid=(B, nq, nk),
        in_specs=[pl.BlockSpec((tq, GROUP_WIDTH), lambda b, i, k: (b * nq + i, 0)),
                  pl.BlockSpec((tk, GROUP_WIDTH), lambda b, i, k: (b * nk + kclamp(i, k), 1)),
                  pl.BlockSpec((tk, N_HEADS * LANES), lambda b, i, k: (b * nk + kclamp(i, k), 0)),
                  pl.BlockSpec((1, LANES), const),
                  pl.BlockSpec((4, LANES), const),
                  pl.BlockSpec((1, HEAD_DIM), const)],
        out_specs=pl.BlockSpec((tq, GROUP_WIDTH), lambda b, i, k: (b * nq + i, 0)),
        scratch_shapes=[pltpu.VMEM((2 * N_HEADS, tq, LANES), F32),
                        pltpu.VMEM((2 * N_HEADS, tq, LANES), F32)],
        compiler_params=_params("parallel", "parallel", "arbitrary"),
        name="diff_attention",
    )(qk, qk, v, _ones_col_row(), lam_pad, norm_g.reshape(1, HEAD_DIM).astype(F32))


def _gdn_kernel(x_ref, z_ref, sm_ref, cw_ref, alog_ref, dtb_ref, ng_ref, gs_ref, ea_ref, eb_ref, tri_ref,
                o_ref, xs_sc, st_sc, *, tt):
    kconv = cw_ref.shape[0]

    @pl.when(pl.program_id(1) == 0)
    def _():
        xs_sc[0:8, :] = jnp.zeros((8, xs_sc.shape[1]), F32)
        st_sc[...] = jnp.zeros(st_sc.shape, F32)

    xs_sc[8:8 + tt, :] = x_ref[...]
    conv = cw_ref[0:1, :] * xs_sc[pl.ds(8 - (kconv - 1), tt), :]
    for j in range(1, kconv):
        conv = conv + cw_ref[j:j + 1, :] * xs_sc[pl.ds(8 - (kconv - 1) + j, tt), :]
    xs_sc[0:8, :] = x_ref[tt - 8:tt, :]
    qkv = jax.nn.silu(conv)

    def l2n(a):
        out = []
        for c in range(GROUP_WIDTH // LANES):
            xc = a[:, c * LANES:(c + 1) * LANES]
            out.append(xc * lax.rsqrt(_dot_c(xc * xc, gs_ref[...]) + EPS))
        return jnp.concatenate(out, axis=1)

    q = l2n(qkv[:, 0:GROUP_WIDTH]) * HEAD_DIM ** -0.5
    k = l2n(qkv[:, GROUP_WIDTH:2 * GROUP_WIDTH])
    v = qkv[:, 2 * GROUP_WIDTH:]
    sm = sm_ref[...]
    g_all = -jnp.exp(alog_ref[...]) * jax.nn.softplus(sm + dtb_ref[...])
    gx = _dot_c(g_all, ea_ref[...])
    bx = _dot_c(jax.nn.sigmoid(sm), eb_ref[...])
    ii = lax.broadcasted_iota(jnp.int32, (CHUNK, CHUNK), 0)
    jj = lax.broadcasted_iota(jnp.int32, (CHUNK, CHUNK), 1)
    eye = jnp.where(ii == jj, 1.0, 0.0)
    nchunk = tt // CHUNK
    heads = [slice(h * HEAD_DIM, (h + 1) * HEAD_DIM) for h in range(N_HEADS)]
    pairs, dec, rhs, qd, kend, eglast, qk_nt, kbk_nt = [], [], [], [], [], [], [], []
    for c in range(nchunk):
        r = slice(c * CHUNK, (c + 1) * CHUNK)
        gcx = _c_dot(tri_ref[...], gx[r])
        gct = gcx.T
        egc = jnp.exp(gcx)
        glast = gcx[CHUNK - 1:CHUNK, :]
        kc, qc, bc = k[r].astype(BF16), q[r], bx[r]
        kb = k[r] * bc
        vb = v[r] * bc
        kbe = kb * egc
        qd.append((qc * egc).astype(BF16))
        kend.append((k[r] * jnp.exp(glast - gcx)).astype(BF16))
        eglast.append(jnp.exp(glast))
        qcb, kbb = qc.astype(BF16), kb.astype(BF16)
        for hs in heads:
            pairs.append((c, hs))
            dec.append(jnp.exp(jnp.where(ii >= jj, gcx[:, hs] - gct[hs, :], NEG)))
            rhs.append(jnp.concatenate([vb[:, hs], kbe[:, hs]], axis=1).astype(BF16))
            kbk_nt.append(_dot_nt(kbb[:, hs], kc[:, hs]))
            qk_nt.append(_dot_nt(qcb[:, hs], kc[:, hs]))
    pw = [-jnp.where(ii > jj, a * d, 0.0) for a, d in zip(kbk_nt, dec)]
    inv = [eye + p for p in pw]
    for _ in range(5):
        pw = [_bdot(p, p) for p in pw]
        inv = [x + _bdot(x, p) for x, p in zip(inv, pw)]
    sol = [_bdot(x, b) for x, b in zip(inv, rhs)]
    attn = [(a * d).astype(BF16) for a, d in zip(qk_nt, dec)]
    state = [st_sc[h] for h in range(N_HEADS)]
    outs = []
    for c in range(nchunk):
        idx = [c * N_HEADS + h for h in range(N_HEADS)]
        sb = [s.astype(BF16) for s in state]
        ws = [_dot(sol[i][:, HEAD_DIM:].astype(BF16), sb[h]) for h, i in enumerate(idx)]
        qs = [_dot(qd[c][:, hs], sb[h]) for h, hs in enumerate(heads)]
        v_new = [(sol[i][:, :HEAD_DIM] - ws[h]).astype(BF16) for h, i in enumerate(idx)]
        outs.append(jnp.concatenate([qs[h] + _dot(attn[i], v_new[h]) for h, i in enumerate(idx)], axis=1))
        state = [state[h] * eglast[c][:, hs] + _dot_tn(kend[c][:, hs], v_new[h]) for h, hs in enumerate(heads)]
    for h in range(N_HEADS):
        st_sc[h] = state[h]
    o = jnp.concatenate(outs, axis=0)
    normed = []
    for c in range(GROUP_WIDTH // LANES):
        oc = o[:, c * LANES:(c + 1) * LANES]
        normed.append(oc * lax.rsqrt(_dot_c(oc * oc, gs_ref[...]) * (1.0 / HEAD_DIM) + EPS))
    o = jnp.concatenate(normed, axis=1) * ng_ref[...]
    o_ref[...] = (o * jax.nn.silu(z_ref[...])).astype(o_ref.dtype)


def _gdn(b_qkv, b_z, small, conv_w, a_log, dt_bias, norm_g, B, S, tt=256):
    tt = min(tt, S)
    ns = S // tt
    row = lambda b, i: (b * ns + i, 0)
    const = lambda b, i: (0, 0)
    tri = jnp.asarray(np.tril(np.ones((CHUNK, CHUNK), np.float32)), BF16)
    return pl.pallas_call(
        functools.partial(_gdn_kernel, tt=tt),
        out_shape=jax.ShapeDtypeStruct((B * S, GROUP_WIDTH), BF16),
        grid=(B, ns),
        in_specs=[pl.BlockSpec((tt, 3 * GROUP_WIDTH), row),
                  pl.BlockSpec((tt, GROUP_WIDTH), row),
                  pl.BlockSpec((tt, LANES), row),
                  pl.BlockSpec(conv_w.shape, const),
                  pl.BlockSpec((1, LANES), const),
                  pl.BlockSpec((1, LANES), const),
                  pl.BlockSpec((1, GROUP_WIDTH), const),
                  pl.BlockSpec((LANES, LANES), const),
                  pl.BlockSpec((LANES, GROUP_WIDTH), const),
                  pl.BlockSpec((LANES, GROUP_WIDTH), const),
                  pl.BlockSpec((CHUNK, CHUNK), const)],
        out_specs=pl.BlockSpec((tt, GROUP_WIDTH), row),
        scratch_shapes=[pltpu.VMEM((tt + 8, 3 * GROUP_WIDTH), F32),
                        pltpu.VMEM((N_HEADS, HEAD_DIM, HEAD_DIM), F32)],
        compiler_params=_params("parallel", "arbitrary"),
        name="gated_deltanet",
    )(b_qkv, b_z, small, conv_w.astype(F32), _row128(a_log, _LANE_GDN_A), _row128(dt_bias, _LANE_GDN_A),
      jnp.tile(norm_g.astype(F32), N_HEADS).reshape(1, GROUP_WIDTH), _group_matrix(LANES, HEAD_DIM, False),
      _expand_matrix(_LANE_GDN_A, N_HEADS, HEAD_DIM), _expand_matrix(_LANE_GDN_B, N_HEADS, HEAD_DIM), tri)


def _mlstm_kernel(qk_ref, v_ref, op_ref, sm_ref, ib_ref, fb_ref, ng_ref, gm_ref, ei_ref, ef_ref, tri_ref,
                  o_ref, c_sc, m_sc, *, tt):
    @pl.when(pl.program_id(1) == 0)
    def _():
        c_sc[...] = jnp.zeros(c_sc.shape, F32)
        m_sc[...] = jnp.zeros(m_sc.shape, F32)

    sm = sm_ref[...]
    ig = GATE_CAP * jnp.tanh((sm + ib_ref[...]) * (1.0 / GATE_CAP))
    lf = jax.nn.log_sigmoid(GATE_CAP * jnp.tanh((sm + fb_ref[...]) * (1.0 / GATE_CAP)))
    ix = _dot_c(ig, ei_ref[...])
    fx = _dot_c(lf, ef_ref[...])
    nqk = N_HEADS * MLSTM_QK
    q = qk_ref[:, 0:nqk]
    k = qk_ref[:, nqk:2 * nqk] * MLSTM_QK ** -0.5
    v = v_ref[...]
    ii = lax.broadcasted_iota(jnp.int32, (CHUNK, CHUNK), 0)
    jj = lax.broadcasted_iota(jnp.int32, (CHUNK, CHUNK), 1)
    one_col = jnp.where(jj == 0, 1.0, 0.0)
    nchunk = tt // CHUNK
    cb, dlog, blast, mloc, qh, vaug, cloc, qk_nt = [], [], [], [], [], [], [], []
    for c in range(nchunk):
        r = slice(c * CHUNK, (c + 1) * CHUNK)
        bx = _c_dot(tri_ref[...], fx[r])
        rowv_all = bx - ix[r]
        rowv_t = rowv_all.T
        for h in range(N_HEADS):
            hl = slice(h * LANES, (h + 1) * LANES)
            cb.append(bx[:, hl])
            dlog.append(jnp.where(ii >= jj, bx[:, h * LANES:h * LANES + CHUNK] - rowv_t[h * LANES:h * LANES + CHUNK, :],
                                  NEG))
            blast.append(bx[CHUNK - 1:CHUNK, hl])
            aend = blast[-1] - rowv_all[:, hl]
            mloc.append(jnp.max(aend, axis=0, keepdims=True))
            wend = jnp.exp(aend - mloc[-1])
            qh.append(q[r, h * MLSTM_QK:(h + 1) * MLSTM_QK].astype(BF16))
            kh = k[r, h * MLSTM_QK:(h + 1) * MLSTM_QK]
            vaug.append(jnp.concatenate([v[r, h * HEAD_DIM:(h + 1) * HEAD_DIM], one_col], axis=1).astype(BF16))
            cloc.append(_dot_tn((kh * wend[:, :MLSTM_QK]).astype(BF16), vaug[-1]))
            qk_nt.append(_dot_nt(qh[-1], kh.astype(BF16)))
    c_in, m_in = [], []
    c_st = [c_sc[h] for h in range(N_HEADS)]
    m_st = [m_sc[h][0:1, :] for h in range(N_HEADS)]
    for c in range(nchunk):
        for h in range(N_HEADS):
            i = c * N_HEADS + h
            c_in.append(c_st[h])
            m_in.append(m_st[h])
            m_new = jnp.maximum(blast[i] + m_st[h], mloc[i])
            c_st[h] = jnp.exp(blast[i] + m_st[h] - m_new) * c_st[h] + jnp.exp(mloc[i] - m_new) * cloc[i]
            m_st[h] = m_new
    for h in range(N_HEADS):
        c_sc[h] = c_st[h]
        m_sc[h] = jnp.broadcast_to(m_st[h], (8, LANES))
    inter = [b + m for b, m in zip(cb, m_in)]
    mt = [jnp.maximum(x, jnp.max(d, axis=-1, keepdims=True)) for x, d in zip(inter, dlog)]
    wintra = [(jnp.exp(d - m[:, :CHUNK]) * a).astype(BF16) for d, m, a in zip(dlog, mt, qk_nt)]
    numden = [jnp.exp(x - m) * _dot(qq, cc.astype(BF16)) + _dot(w, va)
              for x, m, qq, cc, w, va in zip(inter, mt, qh, c_in, wintra, vaug)]
    hout = [nd[:, :HEAD_DIM] / jnp.maximum(jnp.abs(nd[:, HEAD_DIM:HEAD_DIM + 1]), jnp.exp(-m[:, 0:1]))
            for nd, m in zip(numden, mt)]
    o = jnp.concatenate([jnp.concatenate(hout[c * N_HEADS:(c + 1) * N_HEADS], axis=1) for c in range(nchunk)], axis=0)
    normed = [_group_rms(o[:, c * LANES:(c + 1) * LANES], gm_ref[...]) for c in range(GROUP_WIDTH // LANES)]
    o = jnp.concatenate(normed, axis=1) * ng_ref[...]
    o_ref[...] = (o * jax.nn.sigmoid(op_ref[...])).astype(o_ref.dtype)


def _mlstm(c_qk, c_v, c_o, small, i_bias, f_bias, norm_g, B, S, tt=256):
    tt = min(tt, S)
    ns = S // tt
    row = lambda b, i: (b * ns + i, 0)
    const = lambda b, i: (0, 0)
    tri = jnp.asarray(np.tril(np.ones((CHUNK, CHUNK), np.float32)), BF16)
    return pl.pallas_call(
        functools.partial(_mlstm_kernel, tt=tt),
        out_shape=jax.ShapeDtypeStruct((B * S, GROUP_WIDTH), BF16),
        grid=(B, ns),
        in_specs=[pl.BlockSpec((tt, GROUP_WIDTH), row),
                  pl.BlockSpec((tt, GROUP_WIDTH), row),
                  pl.BlockSpec((tt, GROUP_WIDTH), row),
                  pl.BlockSpec((tt, LANES), row),
                  pl.BlockSpec((1, LANES), const),
                  pl.BlockSpec((1, LANES), const),
                  pl.BlockSpec((1, GROUP_WIDTH), const),
                  pl.BlockSpec((LANES, LANES), const),
                  pl.BlockSpec((LANES, N_HEADS * LANES), const),
                  pl.BlockSpec((LANES, N_HEADS * LANES), const),
                  pl.BlockSpec((CHUNK, CHUNK), const)],
        out_specs=pl.BlockSpec((tt, GROUP_WIDTH), row),
        scratch_shapes=[pltpu.VMEM((N_HEADS, MLSTM_QK, LANES), F32),
                        pltpu.VMEM((N_HEADS, 8, LANES), F32)],
        compiler_params=_params("parallel", "arbitrary"),
        name="mlstm",
    )(c_qk, c_v, c_o, small, _row128(i_bias, _LANE_I), _row128(f_bias, _LANE_F),
      jnp.tile(norm_g.astype(F32), N_HEADS).reshape(1, GROUP_WIDTH), _group_matrix(LANES, HEAD_DIM, True),
      _expand_matrix(_LANE_I, N_HEADS, LANES), _expand_matrix(_LANE_F, N_HEADS, LANES), tri)


def _memkv_kernel(mem_ref, ln_ref, wkv_ref, gk_ref, gm_ref, k_ref, v_ref):
    x = mem_ref[0]
    xn = (x * lax.rsqrt(jnp.mean(x * x, axis=-1, keepdims=True) + EPS) * ln_ref[...]).astype(BF16)
    kv = _dot(xn, wkv_ref[...])
    for c in range(GROUP_WIDTH // LANES):
        cs = slice(c * LANES, (c + 1) * LANES)
        k_ref[0, :, cs] = (_group_rms(kv[:, cs], gm_ref[...]) * gk_ref[:, cs]).astype(BF16)
    v_ref[0] = kv[:, GROUP_WIDTH:].astype(BF16)


def _memkv(mem, ln_mem, wkv, gain_k):
    B, M, _ = mem.shape
    const = lambda b: (0, 0)
    return pl.pallas_call(
        _memkv_kernel,
        out_shape=[jax.ShapeDtypeStruct((B, M, GROUP_WIDTH), BF16)] * 2,
        grid=(B,),
        in_specs=[pl.BlockSpec((1, M, D_MODEL), lambda b: (b, 0, 0)),
                  pl.BlockSpec((1, D_MODEL), const),
                  pl.BlockSpec((D_MODEL, 2 * GROUP_WIDTH), const),
                  pl.BlockSpec((1, GROUP_WIDTH), const),
                  pl.BlockSpec((LANES, LANES), const)],
        out_specs=[pl.BlockSpec((1, M, GROUP_WIDTH), lambda b: (b, 0, 0))] * 2,
        compiler_params=_params("parallel"),
        name="memory_kv",
    )(mem, ln_mem.reshape(1, D_MODEL), wkv.astype(BF16),
      jnp.tile(gain_k.astype(F32), N_HEADS).reshape(1, GROUP_WIDTH), _group_matrix(LANES, HEAD_DIM, True))


def _out_xattn_kernel(ya_ref, yb_ref, yc_ref, yd_ref, h_ref, wout_ref, lnx_ref, wq_ref, kx_ref, vx_ref, gq_ref, gm_ref,
                      wo_ref, o_ref):
    y = jnp.concatenate([ya_ref[...], yb_ref[...], yc_ref[...], yd_ref[...]], axis=1)
    h1 = h_ref[...] + _dot(y, wout_ref[...])
    hn = (h1 * lax.rsqrt(jnp.mean(h1 * h1, axis=-1, keepdims=True) + EPS) * lnx_ref[...]).astype(BF16)
    q = _dot(hn, wq_ref[...])
    chunks = []
    for c in range(GROUP_WIDTH // LANES):
        cs = slice(c * LANES, (c + 1) * LANES)
        chunks.append((_group_rms(q[:, cs], gm_ref[...]) * gq_ref[:, cs]).astype(BF16))
    qn = jnp.concatenate(chunks, axis=1)
    kx, vx = kx_ref[0], vx_ref[0]
    outs = []
    for h in range(N_HEADS):
        hs = slice(h * HEAD_DIM, (h + 1) * HEAD_DIM)
        s = _dot_nt(qn[:, hs], kx[:, hs]) * HEAD_DIM ** -0.5
        e = jnp.exp(s - jnp.max(s, axis=-1, keepdims=True))
        p = e / jnp.sum(e, axis=-1, keepdims=True)
        outs.append(_dot(p.astype(BF16), vx[:, hs]))
    o = jnp.concatenate(outs, axis=1).astype(BF16)
    o_ref[...] = h1 + _dot(o, wo_ref[...])


def _out_xattn(ys, h2d, w_out, ln_x, wq, kx, vx, gain_q, wo, B, S, tm=512):
    tm = min(tm, S)
    ns = S // tm
    M = kx.shape[1]
    row = lambda b, i: (b * ns + i, 0)
    const = lambda b, i: (0, 0)
    return pl.pallas_call(
        _out_xattn_kernel,
        out_shape=jax.ShapeDtypeStruct((B * S, D_MODEL), F32),
        grid=(B, ns),
        in_specs=[pl.BlockSpec((tm, GROUP_WIDTH), row)] * 4 + [
            pl.BlockSpec((tm, D_MODEL), row),
            pl.BlockSpec((D_MODEL, D_MODEL), const),
            pl.BlockSpec((1, D_MODEL), const),
            pl.BlockSpec((D_MODEL, GROUP_WIDTH), const),
            pl.BlockSpec((1, M, GROUP_WIDTH), lambda b, i: (b, 0, 0)),
            pl.BlockSpec((1, M, GROUP_WIDTH), lambda b, i: (b, 0, 0)),
            pl.BlockSpec((1, GROUP_WIDTH), const),
            pl.BlockSpec((LANES, LANES), const),
            pl.BlockSpec((GROUP_WIDTH, D_MODEL), const)],
        out_specs=pl.BlockSpec((tm, D_MODEL), row),
        compiler_params=_params("parallel", "parallel"),
        name="outproj_xattn",
    )(*ys, h2d, w_out.astype(BF16), ln_x.reshape(1, D_MODEL), wq.astype(BF16), kx, vx,
      jnp.tile(gain_q.astype(F32), N_HEADS).reshape(1, GROUP_WIDTH), _group_matrix(LANES, HEAD_DIM, True),
      wo.astype(BF16))


def _moe_route_kernel(h_ref, ln_ref, wrh_ref, wrl_ref, br_ref, tri_ref, hn_ref, comb_ref, rkt_ref, cnt_ref):
    tm = h_ref.shape[0]
    lane = lax.broadcasted_iota(jnp.int32, (tm, LANES), 1)
    x = h_ref[...]
    hn = x * lax.rsqrt(jnp.mean(x * x, axis=-1, keepdims=True) + EPS) * ln_ref[...]
    hn_hi = hn.astype(BF16)
    hn_ref[...] = hn_hi
    hn_lo = (hn - hn_hi.astype(F32)).astype(BF16)
    logits = _dot(hn_hi, wrh_ref[...]) + _dot(hn_hi, wrl_ref[...]) + _dot(hn_lo, wrh_ref[...]) + br_ref[...]
    lanef = lane.astype(F32)
    big = 1e4
    isg = (lane >= MOE_EXPERTS) & (lane < MOE_EXPERTS + MOE_GROUPS)
    lg = jnp.where(isg, logits, NEG)
    gmax = jnp.max(lg, axis=-1, keepdims=True)
    grp_p = 1.0 / jnp.sum(jnp.exp(lg - gmax), axis=-1, keepdims=True)
    gidx = jnp.min(jnp.where(lg == gmax, lanef, big), axis=-1, keepdims=True) - MOE_EXPERTS
    ing = (lane < MOE_EXPERTS) & ((lane >> 3).astype(F32) == gidx)
    le = jnp.where(ing, logits, NEG)
    m1 = jnp.max(le, axis=-1, keepdims=True)
    z = jnp.sum(jnp.where(ing, jnp.exp(le - m1), 0.0), axis=-1, keepdims=True)
    i1 = jnp.min(jnp.where(le == m1, lanef, big), axis=-1, keepdims=True)
    oh1 = lanef == i1
    le2 = jnp.where(oh1, NEG, le)
    m2 = jnp.max(le2, axis=-1, keepdims=True)
    i2 = jnp.min(jnp.where((le2 == m2) & ing, jnp.where(oh1, big, lanef), big), axis=-1, keepdims=True)
    oh2 = lanef == i2
    p1 = 1.0 / z
    p2 = jnp.exp(m2 - m1) / z
    tot = p1 + p2
    comb_ref[...] = jnp.where(oh1, p1 / tot * grp_p, 0.0) + jnp.where(oh2, p2 / tot * grp_p, 0.0)
    member = jnp.where(lanef == gidx, 1.0, 0.0)
    rank = _dot(tri_ref[...], member.astype(BF16))
    rkt_ref[0] = jnp.where(member > 0.5, rank, -1.0).T[0:8, :]
    cnt_ref[0] = jnp.broadcast_to(jnp.sum(member, axis=0, keepdims=True), (8, LANES))


def _moe_group_kernel(cnt_ref, hn_ref, comb_ref, rkt_ref, prev_ref, w1_ref, w3_ref, w2_ref, o_ref, acc_sc,
                      *, group, tm, ch):
    cnt = cnt_ref[pl.program_id(0) * MOE_GROUPS + group]
    acc_sc[...] = jnp.zeros(acc_sc.shape, F32)
    rkg = rkt_ref[0][group:group + 1, :]
    starts = [(0, ch)] + [(s, ch // 2) for s in range(ch, tm, ch // 2)]
    for start, ch in starts:
        @pl.when(start < cnt)
        def _():
            rows = (start + lax.broadcasted_iota(jnp.int32, (ch, 1), 0)).astype(F32)
            sel = jnp.where(rkg == rows, 1.0, 0.0).astype(BF16)
            xg = _dot(sel, hn_ref[...]).astype(BF16)
            cg = _c_dot(sel, comb_ref[...])
            yg = jnp.zeros((ch, D_MODEL), F32)
            for e in range(MOE_PER_GROUP):
                hg = _dot(xg, w1_ref[e])
                hu = _dot(xg, w3_ref[e])
                ce = cg[:, group * MOE_PER_GROUP + e:group * MOE_PER_GROUP + e + 1]
                yg = yg + _dot((jax.nn.silu(hg) * hu * ce).astype(BF16), w2_ref[e])
            acc_sc[...] += _dot_tn(sel, yg.astype(BF16))
    o_ref[...] = prev_ref[...] + acc_sc[...]


def _moe(h2d, ln, w_group, b_group, w_expert, b_expert, w1, w3, w2, tm=1024, ch=256):
    T = h2d.shape[0]
    tm = min(tm, T)
    nt = T // tm
    pad = LANES - MOE_EXPERTS - MOE_GROUPS
    wr = jnp.concatenate([w_expert, w_group, jnp.zeros((D_MODEL, pad), F32)], axis=1)
    wr_hi = wr.astype(BF16)
    br =jnp.concatenate([b_expert, b_group, jnp.zeros((pad,), F32)]).reshape(1, LANES)
    tri = jnp.asarray(np.tril(np.ones((tm, tm), np.float32), -1), BF16)
    const = lambda i: (0, 0)
    hn, comb, rkt, cnt = pl.pallas_call(
        _moe_route_kernel,
        out_shape=[jax.ShapeDtypeStruct((T, D_MODEL), BF16),
                   jax.ShapeDtypeStruct((T, LANES), F32),
                   jax.ShapeDtypeStruct((nt, 8, tm), F32),
                   jax.ShapeDtypeStruct((nt, 8, LANES), F32)],
        grid=(nt,),
        in_specs=[pl.BlockSpec((tm, D_MODEL), lambda i: (i, 0)),
                  pl.BlockSpec((1, D_MODEL), const),
                  pl.BlockSpec((D_MODEL, LANES), const),
                  pl.BlockSpec((D_MODEL, LANES), const),
                  pl.BlockSpec((1, LANES), const),
                  pl.BlockSpec((tm, tm), const)],
        out_specs=[pl.BlockSpec((tm, D_MODEL), lambda i: (i, 0)),
                   pl.BlockSpec((tm, LANES), lambda i: (i, 0)),
                   pl.BlockSpec((1, 8, tm), lambda i: (i, 0, 0)),
                   pl.BlockSpec((1, 8, LANES), lambda i: (i, 0, 0))],
        compiler_params=_params("parallel"),
        name="moe_route",
    )(h2d, ln.reshape(1, D_MODEL), wr_hi, (wr - wr_hi.astype(F32)).astype(BF16), br, tri)
    counts = cnt[:, 0, :MOE_GROUPS].astype(jnp.int32).reshape(nt * MOE_GROUPS)
    w1b, w3b, w2b = w1.astype(BF16), w3.astype(BF16), w2.astype(BF16)
    out = h2d
    for g in range(MOE_GROUPS):
        wspec = lambda shape: pl.BlockSpec((MOE_PER_GROUP,) + shape, lambda i, c, g=g: (g, 0, 0),
                                           pipeline_mode=pl.Buffered(1))
        out = pl.pallas_call(
            functools.partial(_moe_group_kernel, group=g, tm=tm, ch=ch),
            out_shape=jax.ShapeDtypeStruct((T, D_MODEL), F32),
            grid_spec=pltpu.PrefetchScalarGridSpec(
                num_scalar_prefetch=1,
                grid=(nt,),
                in_specs=[pl.BlockSpec((tm, D_MODEL), lambda i, c: (i, 0)),
                          pl.BlockSpec((tm, LANES), lambda i, c: (i, 0)),
                          pl.BlockSpec((1, 8, tm), lambda i, c: (i, 0, 0)),
                          pl.BlockSpec((tm, D_MODEL), lambda i, c: (i, 0)),
                          wspec((D_MODEL, MOE_FF)), wspec((D_MODEL, MOE_FF)), wspec((MOE_FF, D_MODEL))],
                out_specs=pl.BlockSpec((tm, D_MODEL), lambda i, c: (i, 0)),
                scratch_shapes=[pltpu.VMEM((tm, D_MODEL), F32)]),
            compiler_params=_params("parallel"),
            name="moe_group",
        )(counts, hn, comb, rkt, out, w1b, w3b, w2b)
    return out


def _nsa_mixer(a_q, a_kv, kn, small, qk_gain, cmp_pe, cmp_w1, cmp_w2, cos, sin, B, S):
    nc = S // NSA_CMP_STRIDE
    wide = NSA_CMP_STRIDE * HEAD_DIM
    cos_c, sin_c = _rope_tables(NSA_CMP_STRIDE * np.arange(nc) + NSA_CMP_LEN - 1, HEAD_DIM)
    half = HEAD_DIM // 2
    kc, vc = _nsa_compress(a_kv, cmp_pe.reshape(2, 2 * wide), cmp_w1, cmp_w2, qk_gain[1],
                           cos_c[:, :half], sin_c[:, half:2 * half], B, nc)
    gq_row = jnp.tile(qk_gain[0].astype(F32), N_HEADS).reshape(1, GROUP_WIDTH)
    qn, part, sel = _nsa1(a_q, small, kc, vc, kn, a_kv, cos, sin, gq_row, B, S)
    return _nsa2(qn, sel, kn, a_kv, part, small, B, S)


def kernel(x, mem, ln_mix, w_in, w_out, nsa_qk_gain, nsa_cmp_pe, nsa_cmp_w1, nsa_cmp_w2, gdn_conv, gdn_a_log, gdn_dt_bias, gdn_norm, mlstm_i_bias, mlstm_f_bias, mlstm_norm, diff_qk_gain, diff_lambda, diff_norm, ln_xattn, ln_mem, xattn_wq, xattn_wkv, xattn_qk_gain, xattn_wo, ln_moe, moe_w_group, moe_b_group, moe_w_expert, moe_b_expert, moe_w1, moe_w3, moe_w2):
    B, S, D = x.shape
    depth = w_in.shape[0]
    cos_a, sin_a = _rope_tables(np.arange(S), HEAD_DIM)
    cos_d, sin_d = _rope_tables(np.arange(S), DIFF_SUB)
    h = x.reshape(B * S, D)
    for l in range(depth):
        gain_k = jnp.concatenate([nsa_qk_gain[l, 2], nsa_qk_gain[l, 3]]).reshape(1, LANES).astype(F32)
        gain_d = jnp.concatenate([jnp.tile(diff_qk_gain[l, 0], 2 * N_HEADS) * DIFF_SUB ** -0.5,
                                  jnp.tile(diff_qk_gain[l, 1], 2 * N_HEADS)]).reshape(1, 2 * GROUP_WIDTH).astype(F32)
        a_q, a_kv, b_qkv, b_z, c_qk, c_v, c_o, d_v, small, dqk, kn = _inproj(
            h, ln_mix[l], w_in[l], gain_k, gain_d, cos_a, sin_a, cos_d, sin_d, S)
        y_a = _nsa_mixer(a_q, a_kv, kn, small, nsa_qk_gain[l], nsa_cmp_pe[l], nsa_cmp_w1[l], nsa_cmp_w2[l],
                         cos_a, sin_a, B, S)
        y_b = _gdn(b_qkv, b_z, small, gdn_conv[l], gdn_a_log[l], gdn_dt_bias[l], gdn_norm[l], B, S)
        y_c = _mlstm(c_qk, c_v, c_o, small, mlstm_i_bias[l], mlstm_f_bias[l], mlstm_norm[l], B, S)
        lambda_init = 0.8 - 0.6 * math.exp(-0.3 * l)
        y_d = _diff_attention(dqk, d_v, diff_lambda[l], diff_norm[l], lambda_init, B, S)
        kx, vx = _memkv(mem, ln_mem[l], xattn_wkv[l], xattn_qk_gain[l, 1])
        h = _out_xattn((y_a, y_b, y_c, y_d), h, w_out[l], ln_xattn[l], xattn_wq[l], kx, vx,
                       xattn_qk_gain[l, 0], xattn_wo[l], B, S)
        h = _moe(h, ln_moe[l], moe_w_group[l], moe_b_group[l], moe_w_expert[l], moe_b_expert[l],
                 moe_w1[l], moe_w3[l], moe_w2[l])
    return h.reshape(B, S, D)
```

```python
import functools
import math

import numpy as np
import jax
import jax.numpy as jnp
from jax import lax
from jax.experimental import pallas as pl
from jax.experimental.pallas import tpu as pltpu

F32 = jnp.float32
BF16 = jnp.bfloat16
HI = lax.Precision.HIGHEST

D_MODEL = 1024
HEAD_DIM = 64
N_HEADS = 4
GROUP_WIDTH = 256
ROPE_THETA = 10000.0
EPS = 1e-6
NEG = -1e30

NSA_CMP_LEN = 32
NSA_CMP_STRIDE = 16
NSA_SEL_LEN = 64
NSA_TOP_N = 16
NSA_WINDOW = 512
NSA_FORCE_BONUS = 1e3
CHUNK = 64
MLSTM_QK = 32
GATE_CAP = 15.0
DIFF_SUB = 32
MOE_GROUPS = 4
MOE_PER_GROUP = 8
MOE_EXPERTS = 32
MOE_FF = 256
LANES = 128
VMEM_LIMIT = 48 * 1024 * 1024

_A0, _B0, _C0, _D0 = 0, 652, 1684, 2460
_GROUPS = (
    ("a_q", ((_A0, 256),), F32),
    ("a_kv", ((_A0 + 384, 64), (_A0 + 512, 64), (_A0 + 448, 64), (None, 64), (_A0 + 576, 64), (None, 64),
              (_A0 + 256, 64), (_A0 + 320, 64)), F32),
    ("b_qkv", ((_B0, 768),), F32),
    ("b_z", ((_B0 + 776, 256),), F32),
    ("c_qk", ((_C0, 256),), F32),
    ("c_v", ((_C0 + 256, 256),), F32),
    ("c_o", ((_C0 + 520, 256),), F32),
    ("d_qk", ((_D0, 512),), F32),
    ("d_v", tuple(p for h in range(N_HEADS) for p in ((_D0 + 512 + 64 * h, 64), (None, 64))), BF16),
    ("small", ((_A0 + 640, 12), (_B0 + 768, 4), (_B0 + 772, 4), (_C0 + 512, 4), (_C0 + 516, 4), (None, 100)), F32),
)
_LANE_GDN_A, _LANE_GDN_B, _LANE_I, _LANE_F = 12, 16, 20, 24


def _dot(a, b, prec=None):
    return jnp.dot(a, b, preferred_element_type=F32, precision=prec)


def _dot_nt(a, b, prec=None):
    return lax.dot_general(a, b, (((1,), (1,)), ((), ())), preferred_element_type=F32, precision=prec)


def _dot_tn(a, b, prec=None):
    return lax.dot_general(a, b, (((0,), (0,)), ((), ())), preferred_element_type=F32, precision=prec)


def _bdot(a, b):
    return _dot(a.astype(BF16), b.astype(BF16))


def _bdot_nt(a, b):
    return _dot_nt(a.astype(BF16), b.astype(BF16))


def _bdot_tn(a, b):
    return _dot_tn(a.astype(BF16), b.astype(BF16))


def _split3(a):
    hi = a.astype(BF16)
    r = a - hi.astype(F32)
    mid = r.astype(BF16)
    return hi, mid, (r - mid.astype(F32)).astype(BF16)


def _dot_c(a, c):
    hi, mid, lo = _split3(a)
    return _dot(hi, c) + _dot(mid, c) + _dot(lo, c)


def _c_dot(c, b):
    hi, mid, lo = _split3(b)
    return _dot(c, hi) + _dot(c, mid) + _dot(c, lo)


def _dot_c2(a, c):
    hi = a.astype(BF16)
    return _dot(hi, c) + _dot((a - hi.astype(F32)).astype(BF16), c)


def _c_dot2(c, b):
    hi = b.astype(BF16)
    return _dot(c, hi) + _dot(c, (b - hi.astype(F32)).astype(BF16))


def _dot_nt_x3(a, b):
    ah, bh = a.astype(BF16), b.astype(BF16)
    al, bl = (a - ah.astype(F32)).astype(BF16), (b - bh.astype(F32)).astype(BF16)
    return _dot_nt(ah, bh) + _dot_nt(ah, bl) + _dot_nt(al, bh)


def _params(*sem):
    return pltpu.CompilerParams(dimension_semantics=sem, vmem_limit_bytes=VMEM_LIMIT)


def _group_matrix(width, gsz, mean):
    g = np.kron(np.eye(width // gsz), np.ones((gsz, gsz)))
    return jnp.asarray(g / gsz if mean else g, BF16)


def _expand_matrix(src_lane0, n, out_per, stride=1):
    e = np.zeros((LANES, n * out_per), np.float32)
    for h in range(n):
        e[src_lane0 + stride * h, h * out_per:(h + 1) * out_per] = 1.0
    return jnp.asarray(e, BF16)


def _row128(vals, lane0):
    return jnp.zeros((1, LANES), F32).at[0, lane0:lane0 + vals.shape[0]].set(vals.astype(F32))


def _ones_col_row():
    return jnp.zeros((1, LANES), BF16).at[0, HEAD_DIM].set(1.0)


def _rope_tables(pos, dim):
    inv = 1.0 / (ROPE_THETA ** (jnp.arange(0, dim, 2, dtype=F32) / dim))
    ang = jnp.asarray(pos).astype(F32)[:, None] * inv[None, :]
    cos, sin = jnp.cos(ang), jnp.sin(ang)
    cosd = jnp.concatenate([cos, cos], axis=-1)
    sind = jnp.concatenate([-sin, sin], axis=-1)
    rep = LANES // dim
    return jnp.tile(cosd, (1, rep)), jnp.tile(sind, (1, rep))


def _rope128(x, cos, sin_signed, half):
    left = pltpu.roll(x, LANES - half, 1)
    right = pltpu.roll(x, half, 1)
    lane = lax.broadcasted_iota(jnp.int32, x.shape, 1)
    first = (lane & (2 * half - 1)) < half
    return x * cos + jnp.where(first, left, right) * sin_signed


def _rep_lanes(x, width):
    return x if width == LANES else jnp.concatenate([x] * (width // LANES), axis=1)


def _softmax_rows(s, mask):
    m = jnp.max(s, axis=-1, keepdims=True)
    e = jnp.exp(s - m)
    return jnp.where(mask, e / jnp.sum(e, axis=-1, keepdims=True), 0.0)


def _group_rms(x, gm):
    return x * lax.rsqrt(_dot_c2(x * x, gm) + EPS)


def _flash_update(items, m_ref, acc_ref, tk):
    m_prev = [m_ref[i] for _, _, i in items]
    m_new = [jnp.maximum(mp, jnp.max(s, axis=-1, keepdims=True)) for mp, (s, _, _) in zip(m_prev, items)]
    p = [jnp.exp((s - _rep_lanes(mn, tk)).astype(BF16)) for mn, (s, _, _) in zip(m_new, items)]
    pv = [_dot(pp, va) for pp, (_, va, _) in zip(p, items)]
    for mp, mn, x, (_, _, i) in zip(m_prev, m_new, pv, items):
        acc_ref[i] = jnp.exp(mp - mn) * acc_ref[i] + x
        m_ref[i] = mn


def _flash_result(acc_ref, idx):
    acc = acc_ref[idx]
    return acc[:, :HEAD_DIM] / acc[:, HEAD_DIM:HEAD_DIM + 1]


def _normrope(y, gain, gm, cos, sin, half):
    out = []
    for c in range(y.shape[1] // LANES):
        cs = slice(c * LANES, (c + 1) * LANES)
        out.append(_rope128(_group_rms(y[:, cs], gm) * gain[:, cs], cos, sin, half))
    return out[0] if len(out) == 1 else jnp.concatenate(out, axis=1)


def _inproj_kernel(x_ref, g_ref, w_ref, gk_ref, gd_ref, gm64_ref, gm32_ref, cosa_ref, sina_ref, cosd_ref, sind_ref,
                   *outs, widths):
    x = x_ref[...]
    ms = jnp.mean(x * x, axis=-1, keepdims=True)
    xn = (x * lax.rsqrt(ms + EPS) * g_ref[...]).astype(BF16)
    outs = list(outs)
    kn_ref, dqk_ref = outs.pop(), outs.pop()
    off = 0
    for (name, _, _), wd in zip(_GROUPS, widths):
        y = _dot(xn, w_ref[:, off:off + wd])
        off += wd
        if name == "d_qk":
            dqk_ref[...] = _normrope(y, gd_ref[...], gm32_ref[...], cosd_ref[...], sind_ref[...],
                                     DIFF_SUB // 2).astype(BF16)
            continue
        o = outs.pop(0)
        o[...] = y.astype(o.dtype)
        if name == "a_kv":
            kn_ref[...] = _normrope(y[:, :LANES], gk_ref[...], gm64_ref[...], cosa_ref[...], sina_ref[...],
                                    HEAD_DIM // 2).astype(BF16)


def _permute_w_in(w):
    cols, widths = [], []
    for _, parts, _ in _GROUPS:
        for s, n in parts:
            cols.append(jnp.zeros((w.shape[0], n), w.dtype) if s is None else w[:, s:s + n])
        widths.append(sum(n for _, n in parts))
    return jnp.concatenate(cols, axis=1).astype(BF16), tuple(widths)


def _inproj(h2d, gain, w_in, gain_k, gain_d, cos_a, sin_a, cos_d, sin_d, S, tm=256):
    T = h2d.shape[0]
    tm = min(tm, S)
    ns = S // tm
    wp, widths = _permute_w_in(w_in)
    kept = [(wd, g) for wd, g in zip(widths, _GROUPS) if g[0] != "d_qk"]
    out_shape = [jax.ShapeDtypeStruct((T, wd), g[2]) for wd, g in kept]
    out_shape += [jax.ShapeDtypeStruct((T, 2 * GROUP_WIDTH), BF16), jax.ShapeDtypeStruct((T, LANES), BF16)]
    const = lambda i: (0, 0)
    table = pl.BlockSpec((tm, LANES), lambda i: (i % ns, 0))
    return pl.pallas_call(
        functools.partial(_inproj_kernel, widths=widths),
        out_shape=out_shape,
        grid=(T // tm,),
        in_specs=[pl.BlockSpec((tm, D_MODEL), lambda i: (i, 0)),
                  pl.BlockSpec((1, D_MODEL), const),
                  pl.BlockSpec((D_MODEL, sum(widths)), const),
                  pl.BlockSpec((1, LANES), const),
                  pl.BlockSpec((1, 2 * GROUP_WIDTH), const),
                  pl.BlockSpec((LANES, LANES), const),
                  pl.BlockSpec((LANES, LANES), const),
                  table, table, table, table],
        out_specs=[pl.BlockSpec((tm, s.shape[1]), lambda i: (i, 0)) for s in out_shape],
        compiler_params=_params("parallel"),
        name="inproj",
    )(h2d, gain.reshape(1, D_MODEL), wp, gain_k, gain_d, _group_matrix(LANES, HEAD_DIM, True),
      _group_matrix(LANES, DIFF_SUB, True), cos_a, sin_a, cos_d, sin_d)


def _nsa_cmp_kernel(kv_ref, pe_ref, w1_ref, w2_ref, gain_ref, cos_ref, sin_ref, kc_ref, vc_ref, *, nc):
    half_in = NSA_CMP_STRIDE * HEAD_DIM

    a = [jnp.zeros((nc, 2 * HEAD_DIM), F32) for _ in range(2)]
    b = [jnp.zeros((nc, 2 * HEAD_DIM), F32) for _ in range(2)]
    for l in range(NSA_CMP_STRIDE):
        xl = kv_ref[pl.ds(l, nc, stride=NSA_CMP_STRIDE), :]
        for j in range(2):
            xj = xl[:, j * HEAD_DIM:(j + 1) * HEAD_DIM]
            a[j] = a[j] + _dot(xj, w1_ref[j, l * HEAD_DIM:(l + 1) * HEAD_DIM, :], HI)
            b[j] = b[j] + _dot(xj, w1_ref[j, half_in + l * HEAD_DIM:half_in + (l + 1) * HEAD_DIM, :], HI)

    def finish(j):
        pe = jnp.broadcast_to(pe_ref[j], (8, 2 * half_in))
        c = _dot(pe, w1_ref[j], HI)[0:1]
        hid = jax.nn.gelu(a[j] + pltpu.roll(b[j], nc - 1, 0) + c)
        return _dot(hid, w2_ref[j], HI)

    kc = finish(0)
    kc = kc * lax.rsqrt(jnp.mean(kc * kc, axis=-1, keepdims=True) + EPS) * gain_ref[...]
    x1, x2 = kc[:, :HEAD_DIM // 2], kc[:, HEAD_DIM // 2:]
    cos, sin = cos_ref[...], sin_ref[...]
    kc_ref[0] = jnp.concatenate([x1 * cos - x2 * sin, x2 * cos + x1 * sin], axis=1)
    vc_ref[0] = finish(1)


def _nsa_compress(a_kv, pe, w1, w2, gain, cos_c, sin_c, B, nc):
    wide = NSA_CMP_STRIDE * HEAD_DIM
    return pl.pallas_call(
        functools.partial(_nsa_cmp_kernel, nc=nc),
        out_shape=[jax.ShapeDtypeStruct((B, nc, HEAD_DIM), F32)] * 2,
        grid=(B,),
        in_specs=[pl.BlockSpec((nc * NSA_CMP_STRIDE, LANES), lambda b: (b, 3)),
                  pl.BlockSpec((2, 1, 2 * wide), lambda b: (0, 0, 0)),
                  pl.BlockSpec((2, 2 * wide, 2 * HEAD_DIM), lambda b: (0, 0, 0)),
                  pl.BlockSpec((2, 2 * HEAD_DIM, HEAD_DIM), lambda b: (0, 0, 0)),
                  pl.BlockSpec((1, HEAD_DIM), lambda b: (0, 0)),
                  pl.BlockSpec((nc, HEAD_DIM // 2), lambda b: (0, 0)),
                  pl.BlockSpec((nc, HEAD_DIM // 2), lambda b: (0, 0))],
        out_specs=[pl.BlockSpec((1, nc, HEAD_DIM), lambda b: (b, 0, 0))] * 2,
        compiler_params=_params("parallel"),
        name="nsa_compress",
    )(a_kv, pe.reshape(2, 1, 2 * wide), w1, w2, gain.reshape(1, HEAD_DIM), cos_c, sin_c)


def _nsa1_kernel(q_ref, sm_ref, kc_ref, vc_ref, kn_ref, vw_ref, cos_ref, sin_ref, gq_ref, gm_ref, ovl_ref,
                 eg0_ref, eg2_ref, qn_ref, part_ref, sel_ref, *, tq, nc, n_sel):
    t0 = pl.program_id(1) * tq
    chunks = []
    for c in range(GROUP_WIDTH // LANES):
        cs = slice(c * LANES, (c + 1) * LANES)
        xn = _group_rms(q_ref[:, cs], gm_ref[...]) * gq_ref[:, cs]
        chunks.append(_rope128(xn, cos_ref[...], sin_ref[...], HEAD_DIM // 2))
    qs = jnp.concatenate(chunks, axis=1) * HEAD_DIM ** -0.5
    qb = qs.astype(BF16)
    qn_ref[...] = qb
    sig = jax.nn.sigmoid(sm_ref[...])
    g0x = _dot_c2(sig, eg0_ref[...])
    g2x = _dot_c2(sig, eg2_ref[...])
    tpos = t0 + lax.broadcasted_iota(jnp.int32, (tq, 1), 0)

    kc = kc_ref[0]
    vc = vc_ref[0].astype(BF16)
    cidx = lax.broadcasted_iota(jnp.int32, (1, nc), 1)
    cmask = ((NSA_CMP_STRIDE * cidx + NSA_CMP_LEN - 1) <= tpos) & (cidx < nc - 1)
    psum = jnp.zeros((tq, nc), F32)
    o_cmp = []
    for h in range(N_HEADS):
        hs = slice(h * HEAD_DIM, (h + 1) * HEAD_DIM)
        p = _softmax_rows(jnp.where(cmask, _dot_nt_x3(qs[:, hs], kc), NEG), cmask)
        o_cmp.append(_dot(p.astype(BF16), vc))
        psum = psum + p

    imp = _dot_c(psum, ovl_ref[...])
    j = lax.broadcasted_iota(jnp.int32, (tq, LANES), 1)
    cur = tpos >> 6
    valid = j <= cur
    forced = (j == 0) | (j == cur) | (j == cur - 1)
    score = jnp.where(valid, imp + jnp.where(forced, NSA_FORCE_BONUS, 0.0), NEG)
    nrow = -(-n_sel // 8) * 8
    st = score.T[0:nrow, :]
    blocks = [st[8 * g:8 * g + 8, :] for g in range(nrow // 8)]
    cnts = [jnp.zeros((8, tq), F32) for _ in blocks]
    j8 = lax.broadcasted_iota(jnp.int32, (8, 1), 0)
    for i in range(n_sel):
        row = st[i:i + 1, :]
        for g, blk in enumerate(blocks):
            if 8 * g > i:
                won = jnp.where(row >= blk, 1.0, 0.0)
            elif 8 * g + 7 < i:
                won = jnp.where(row > blk, 1.0, 0.0)
            else:
                tie = jnp.where(j8 + 8 * g > i, 1.0, 0.0)
                won = jnp.where(row > blk, 1.0, jnp.where(row == blk, tie, 0.0))
            cnts[g] = cnts[g] + won
    cnt = jnp.concatenate(cnts, axis=0)
    sel_t = jnp.where((cnt < min(NSA_TOP_N, n_sel)) & (st > 0.5 * NEG), 1.0, 0.0)
    if nrow < LANES:
        sel_t = jnp.concatenate([sel_t, jnp.zeros((LANES - nrow, tq), F32)], axis=0)
    sel_ref[...] = sel_t.T.astype(BF16)

    band = tq + NSA_WINDOW
    start = pl.multiple_of(jnp.maximum(t0 - NSA_WINDOW, 0), LANES)
    kw = kn_ref[pl.ds(start, band), HEAD_DIM:2 * HEAD_DIM]
    one_row = jnp.where(lax.broadcasted_iota(jnp.int32, (1, LANES), 1) == HEAD_DIM, 1.0, 0.0).astype(BF16)
    vaug = vw_ref[pl.ds(start, band), :].astype(BF16) + one_row
    dist = tpos - (start + lax.broadcasted_iota(jnp.int32, (1, band), 1))
    wmask = (dist >= 0) & (dist < NSA_WINDOW)
    s_win = [jnp.where(wmask, _dot_nt(qb[:, h * HEAD_DIM:(h + 1) * HEAD_DIM], kw), NEG) for h in range(N_HEADS)]
    p_win = [jnp.exp((s - jnp.max(s, axis=-1, keepdims=True)).astype(BF16)) for s in s_win]
    pv = [_dot(p, vaug) for p in p_win]
    o_win = [x[:, :HEAD_DIM] / x[:, HEAD_DIM:HEAD_DIM + 1] for x in pv]
    part_ref[...] = g0x * jnp.concatenate(o_cmp, axis=1) + g2x * jnp.concatenate(o_win, axis=1)


def _nsa_overlap(nc, n_sel):
    c0 = NSA_CMP_STRIDE * np.arange(nc)[:, None]
    s0 = NSA_SEL_LEN * np.arange(n_sel)[None, :]
    ov = np.clip(np.minimum(c0 + NSA_CMP_LEN, s0 + NSA_SEL_LEN) - np.maximum(c0, s0), 0, None) / NSA_CMP_STRIDE
    ov[nc - 1:] = 0.0
    out = np.zeros((nc, LANES), np.float32)
    out[:, :n_sel] = ov
    return jnp.asarray(out, BF16)


def _nsa1(a_q, small, kc, vc, kn, a_kv, cos, sin, gq_row, B, S, tq=256):
    tq = min(tq, S)
    nq = S // tq
    nc = S // NSA_CMP_STRIDE
    n_sel = S // NSA_SEL_LEN
    row = lambda b, i: (b * nq + i, 0)
    return pl.pallas_call(
        functools.partial(_nsa1_kernel, tq=tq, nc=nc, n_sel=n_sel),
        out_shape=[jax.ShapeDtypeStruct((B * S, GROUP_WIDTH), BF16),
                   jax.ShapeDtypeStruct((B * S, GROUP_WIDTH), F32),
                   jax.ShapeDtypeStruct((B * S, LANES), BF16)],
        grid=(B, nq),
        in_specs=[pl.BlockSpec((tq, GROUP_WIDTH), row),
                  pl.BlockSpec((tq, LANES), row),
                  pl.BlockSpec((1, nc, HEAD_DIM), lambda b, i: (b, 0, 0)),
                  pl.BlockSpec((1, nc, HEAD_DIM), lambda b, i: (b, 0, 0)),
                  pl.BlockSpec((S, LANES), lambda b, i: (b, 0)),
                  pl.BlockSpec((S, LANES), lambda b, i: (b, 2)),
                  pl.BlockSpec((tq, LANES), lambda b, i: (i, 0)),
                  pl.BlockSpec((tq, LANES), lambda b, i: (i, 0)),
                  pl.BlockSpec((1, GROUP_WIDTH), lambda b, i: (0, 0)),
                  pl.BlockSpec((LANES, LANES), lambda b, i: (0, 0)),
                  pl.BlockSpec((nc, LANES), lambda b, i: (0, 0)),
                  pl.BlockSpec((LANES, GROUP_WIDTH), lambda b, i: (0, 0)),
                  pl.BlockSpec((LANES, GROUP_WIDTH), lambda b, i: (0, 0))],
        out_specs=[pl.BlockSpec((tq, GROUP_WIDTH), row),
                   pl.BlockSpec((tq, GROUP_WIDTH), row),
                   pl.BlockSpec((tq, LANES), row)],
        compiler_params=_params("parallel", "parallel"),
        name="nsa_cmp_win_select",
    )(a_q, small, kc, vc, kn, a_kv, cos, sin, gq_row, _group_matrix(LANES, HEAD_DIM, True),
      _nsa_overlap(nc, n_sel), _expand_matrix(0, N_HEADS, HEAD_DIM, 3), _expand_matrix(2, N_HEADS, HEAD_DIM, 3))


def _causal_tiles(nq, tq, tk):
    pairs = [(i, k) for i in range(nq) for k in range((i * tq + tq - 1) // tk + 1)]
    return (jnp.asarray([p[0] for p in pairs], jnp.int32), jnp.asarray([p[1] for p in pairs], jnp.int32))


def _nsa2_kernel(qt_ref, kt_ref, qn_ref, sel_ref, e_ref, kn_ref, vs_ref, one_ref, part_ref, sm_ref, eg1_ref, o_ref,
                 m_sc, acc_sc, *, tq, tk):
    qi = qt_ref[pl.program_id(1)]
    ki = kt_ref[pl.program_id(1)]

    @pl.when(ki == 0)
    def _():
        m_sc[...] = jnp.full(m_sc.shape, NEG, F32)
        acc_sc[...] = jnp.zeros(acc_sc.shape, F32)

    def step(causal):
        mask = _dot(sel_ref[...], e_ref[...]) > 0.5
        if causal:
            tpos = qi * tq + lax.broadcasted_iota(jnp.int32, (tq, 1), 0)
            kpos = ki * tk + lax.broadcasted_iota(jnp.int32, (1, tk), 1)
            mask = mask & (kpos <= tpos)
        ks = kn_ref[:, 0:HEAD_DIM]
        vaug = vs_ref[...].astype(BF16) + one_ref[...]
        for h0 in range(0, N_HEADS, 2):
            _flash_update([(jnp.where(mask, _dot_nt(qn_ref[:, h * HEAD_DIM:(h + 1) * HEAD_DIM], ks), NEG), vaug, h)
                           for h in (h0, h0 + 1)], m_sc, acc_sc, tk)

    @pl.when(ki * tk + tk - 1 <= qi * tq)
    def _():
        step(False)

    @pl.when(ki * tk + tk - 1 > qi * tq)
    def _():
        step(True)

    @pl.when(ki == (qi * tq + tq - 1) // tk)
    def _():
        g1x = _dot_c2(jax.nn.sigmoid(sm_ref[...]), eg1_ref[...])
        o = jnp.concatenate([_flash_result(acc_sc, h) for h in range(N_HEADS)], axis=1)
        o_ref[...] = (part_ref[...] + g1x * o).astype(o_ref.dtype)


def _nsa2(qn, sel, kn, a_kv, part, small, B, S, tq=512, tk=512):
    tq, tk = min(tq, S), min(tk, S)
    nq, nk = S // tq, S // tk
    e = np.zeros((LANES, S), np.float32)
    e[np.arange(S) // NSA_SEL_LEN, np.arange(S)] = 1.0
    qt, kt = _causal_tiles(nq, tq, tk)
    row = lambda b, j, qt, kt: (b * nq + qt[j], 0)
    const = lambda b, j, qt, kt: (0, 0)
    return pl.pallas_call(
        functools.partial(_nsa2_kernel, tq=tq, tk=tk),
        out_shape=jax.ShapeDtypeStruct((B * S, GROUP_WIDTH), BF16),
        grid_spec=pltpu.PrefetchScalarGridSpec(
            num_scalar_prefetch=2,
            grid=(B, qt.shape[0]),
            in_specs=[pl.BlockSpec((tq, GROUP_WIDTH), row),
                      pl.BlockSpec((tq, LANES), row),
                      pl.BlockSpec((LANES, tk), lambda b, j, qt, kt: (0, kt[j])),
                      pl.BlockSpec((tk, LANES), lambda b, j, qt, kt: (b * nk + kt[j], 0)),
                      pl.BlockSpec((tk, LANES), lambda b, j, qt, kt: (b * nk + kt[j], 1)),
                      pl.BlockSpec((1, LANES), const),
                      pl.BlockSpec((tq, GROUP_WIDTH), row),
                      pl.BlockSpec((tq, LANES), row),
                      pl.BlockSpec((LANES, GROUP_WIDTH), const)],
            out_specs=pl.BlockSpec((tq, GROUP_WIDTH), row),
            scratch_shapes=[pltpu.VMEM((N_HEADS, tq, LANES), F32),
                            pltpu.VMEM((N_HEADS, tq, LANES), F32)]),
        compiler_params=_params("parallel", "arbitrary"),
        name="nsa_selected",
    )(qt, kt, qn, sel, jnp.asarray(e, BF16), kn, a_kv, _ones_col_row(), part, small,
      _expand_matrix(1, N_HEADS, HEAD_DIM, 3))


def _diff_kernel(qt_ref, kt_ref, q_ref, k_ref, v_ref, one_ref, lam_ref, gain_ref, o_ref, m_sc, acc_sc,
                 *, tq, tk, lambda_init):
    qi = qt_ref[pl.program_id(1)]
    ki = kt_ref[pl.program_id(1)]

    @pl.when(ki == 0)
    def _():
        m_sc[...] = jnp.full(m_sc.shape, NEG, F32)
        acc_sc[...] = jnp.zeros(acc_sc.shape, F32)

    def step(causal):
        if causal:
            tpos = qi * tq + lax.broadcasted_iota(jnp.int32, (tq, 1), 0)
            kpos = ki * tk + lax.broadcasted_iota(jnp.int32, (1, tk), 1)
            mask = kpos <= tpos
        for h in range(N_HEADS):
            vaug = v_ref[:, h * LANES:(h + 1) * LANES] + one_ref[...]
            items = []
            for idx in (2 * h, 2 * h + 1):
                cs = slice(idx * DIFF_SUB, (idx + 1) * DIFF_SUB)
                s = _dot_nt(q_ref[:, cs], k_ref[:, cs])
                items.append((jnp.where(mask, s, NEG) if causal else s, vaug, idx))
            _flash_update(items, m_sc, acc_sc, tk)

    @pl.when(ki * tk + tk - 1 <= qi * tq)
    def _():
        step(False)

    @pl.when(ki * tk + tk - 1 > qi * tq)
    def _():
        step(True)

    @pl.when(ki == (qi * tq + tq - 1) // tk)
    def _():
        lm = lam_ref[...]
        lam = (jnp.exp(jnp.sum(lm[0:1] * lm[1:2], axis=-1, keepdims=True))
               - jnp.exp(jnp.sum(lm[2:3] * lm[3:4], axis=-1, keepdims=True)) + lambda_init)
        outs = []
        for h in range(N_HEADS):
            o = _flash_result(acc_sc, 2 * h) - lam * _flash_result(acc_sc, 2 * h + 1)
            o = o * lax.rsqrt(jnp.mean(o * o, axis=-1, keepdims=True) + EPS)
            outs.append(o * gain_ref[...] * (1.0 - lambda_init))
        o_ref[...] = jnp.concatenate(outs, axis=1).astype(o_ref.dtype)


def _diff_attention(qk, v, lam, norm_g, lambda_init, B, S, tq=512, tk=512):
    tq, tk = min(tq, S), min(tk, S)
    nq, nk = S // tq, S // tk
    lam_pad = jnp.zeros((4, LANES), F32).at[:, :DIFF_SUB].set(lam.astype(F32))
    qt, kt = _causal_tiles(nq, tq, tk)
    row = lambda b, j, qt, kt: (b * nq + qt[j], 0)
    const = lambda b, j, qt, kt: (0, 0)
    return pl.pallas_call(
        functools.partial(_diff_kernel, tq=tq, tk=tk, lambda_init=lambda_init),
        out_shape=jax.ShapeDtypeStruct((B * S, GROUP_WIDTH), BF16),
        grid_spec=pltpu.PrefetchScalarGridSpec(
            num_scalar_prefetch=2,
            grid=(B, qt.shape[0]),
            in_specs=[pl.BlockSpec((tq, GROUP_WIDTH), row),
                      pl.BlockSpec((tk, GROUP_WIDTH), lambda b, j, qt, kt: (b * nk + kt[j], 1)),
                      pl.BlockSpec((tk, N_HEADS * LANES), lambda b, j, qt, kt: (b * nk + kt[j], 0)),
                      pl.BlockSpec((1, LANES), const),
                      pl.BlockSpec((4, LANES), const),
                      pl.BlockSpec((1, HEAD_DIM), const)],
            out_specs=pl.BlockSpec((tq, GROUP_WIDTH), row),
            scratch_shapes=[pltpu.VMEM((2 * N_HEADS, tq, LANES), F32),
                            pltpu.VMEM((2 * N_HEADS, tq, LANES), F32)]),
        compiler_params=_params("parallel", "arbitrary"),
        name="diff_attention",
    )(qt, kt, qk, qk, v, _ones_col_row(), lam_pad, norm_g.reshape(1, HEAD_DIM).astype(F32))


def _gdn_kernel(x_ref, z_ref, sm_ref, cw_ref, alog_ref, dtb_ref, ng_ref, gs_ref, ea_ref, eb_ref, tri_ref,
                o_ref, xs_sc, st_sc, *, tt):
    kconv = cw_ref.shape[0]

    @pl.when(pl.program_id(1) == 0)
    def _():
        xs_sc[0:8, :] = jnp.zeros((8, xs_sc.shape[1]), F32)
        st_sc[...] = jnp.zeros(st_sc.shape, F32)

    xs_sc[8:8 + tt, :] = x_ref[...]
    conv = cw_ref[0:1, :] * xs_sc[pl.ds(8 - (kconv - 1), tt), :]
    for j in range(1, kconv):
        conv = conv + cw_ref[j:j + 1, :] * xs_sc[pl.ds(8 - (kconv - 1) + j, tt), :]
    xs_sc[0:8, :] = x_ref[tt - 8:tt, :]
    qkv = jax.nn.silu(conv)

    def l2n(a):
        out = []
        for c in range(GROUP_WIDTH // LANES):
            xc = a[:, c * LANES:(c + 1) * LANES]
            out.append(xc * lax.rsqrt(_dot_c2(xc * xc, gs_ref[...]) + EPS))
        return jnp.concatenate(out, axis=1)

    q = l2n(qkv[:, 0:GROUP_WIDTH]) * HEAD_DIM ** -0.5
    k = l2n(qkv[:, GROUP_WIDTH:2 * GROUP_WIDTH])
    v = qkv[:, 2 * GROUP_WIDTH:]
    sm = sm_ref[...]
    g_all = -jnp.exp(alog_ref[...]) * jax.nn.softplus(sm + dtb_ref[...])
    gx = _dot_c(g_all, ea_ref[...])
    bx = _dot_c2(jax.nn.sigmoid(sm), eb_ref[...])
    ii = lax.broadcasted_iota(jnp.int32, (CHUNK, CHUNK), 0)
    jj = lax.broadcasted_iota(jnp.int32, (CHUNK, CHUNK), 1)
    eye = jnp.where(ii == jj, 1.0, 0.0)
    nchunk = tt // CHUNK
    heads = [slice(h * HEAD_DIM, (h + 1) * HEAD_DIM) for h in range(N_HEADS)]
    pairs, dec, rhs, qd, kend, eglast, qk_nt, kbk_nt = [], [], [], [], [], [], [], []
    for c in range(nchunk):
        r = slice(c * CHUNK, (c + 1) * CHUNK)
        gcx = _c_dot(tri_ref[...], gx[r])
        gct = gcx.T
        egc = jnp.exp(gcx)
        glast = gcx[CHUNK - 1:CHUNK, :]
        kc, qc, bc = k[r].astype(BF16), q[r], bx[r]
        kb = k[r] * bc
        vb = v[r] * bc
        kbe = kb * egc
        qd.append((qc * egc).astype(BF16))
        kend.append((k[r] * jnp.exp(glast - gcx)).astype(BF16))
        eglast.append(jnp.exp(glast))
        qcb, kbb = qc.astype(BF16), kb.astype(BF16)
        for hs in heads:
            pairs.append((c, hs))
            dec.append(jnp.exp(jnp.where(ii >= jj, gcx[:, hs] - gct[hs, :], NEG)))
            rhs.append(jnp.concatenate([vb[:, hs], kbe[:, hs]], axis=1).astype(BF16))
            kbk_nt.append(_dot_nt(kbb[:, hs], kc[:, hs]))
            qk_nt.append(_dot_nt(qcb[:, hs], kc[:, hs]))
    pw = [-jnp.where(ii > jj, a * d, 0.0) for a, d in zip(kbk_nt, dec)]
    inv = [eye + p for p in pw]
    for _ in range(5):
        pw = [_bdot(p, p) for p in pw]
        inv = [x + _bdot(x, p) for x, p in zip(inv, pw)]
    sol = [_bdot(x, b) for x, b in zip(inv, rhs)]
    attn = [(a * d).astype(BF16) for a, d in zip(qk_nt, dec)]
    state = [st_sc[h] for h in range(N_HEADS)]
    outs = []
    for c in range(nchunk):
        idx = [c * N_HEADS + h for h in range(N_HEADS)]
        sb = [s.astype(BF16) for s in state]
        ws = [_dot(sol[i][:, HEAD_DIM:].astype(BF16), sb[h]) for h, i in enumerate(idx)]
        qs = [_dot(qd[c][:, hs], sb[h]) for h, hs in enumerate(heads)]
        v_new = [(sol[i][:, :HEAD_DIM] - ws[h]).astype(BF16) for h, i in enumerate(idx)]
        outs.append(jnp.concatenate([qs[h] + _dot(attn[i], v_new[h]) for h, i in enumerate(idx)], axis=1))
        state = [state[h] * eglast[c][:, hs] + _dot_tn(kend[c][:, hs], v_new[h]) for h, hs in enumerate(heads)]
    for h in range(N_HEADS):
        st_sc[h] = state[h]
    o = jnp.concatenate(outs, axis=0)
    normed = []
    for c in range(GROUP_WIDTH // LANES):
        oc = o[:, c * LANES:(c + 1) * LANES]
        normed.append(oc * lax.rsqrt(_dot_c2(oc * oc, gs_ref[...]) * (1.0 / HEAD_DIM) + EPS))
    o = jnp.concatenate(normed, axis=1) * ng_ref[...]
    o_ref[...] = (o * jax.nn.silu(z_ref[...])).astype(o_ref.dtype)


def _gdn(b_qkv, b_z, small, conv_w, a_log, dt_bias, norm_g, B, S, tt=512):
    tt = min(tt, S)
    ns = S // tt
    row = lambda b, i: (b * ns + i, 0)
    const = lambda b, i: (0, 0)
    tri = jnp.asarray(np.tril(np.ones((CHUNK, CHUNK), np.float32)), BF16)
    return pl.pallas_call(
        functools.partial(_gdn_kernel, tt=tt),
        out_shape=jax.ShapeDtypeStruct((B * S, GROUP_WIDTH), BF16),
        grid=(B, ns),
        in_specs=[pl.BlockSpec((tt, 3 * GROUP_WIDTH), row),
                  pl.BlockSpec((tt, GROUP_WIDTH), row),
                  pl.BlockSpec((tt, LANES), row),
                  pl.BlockSpec(conv_w.shape, const),
                  pl.BlockSpec((1, LANES), const),
                  pl.BlockSpec((1, LANES), const),
                  pl.BlockSpec((1, GROUP_WIDTH), const),
                  pl.BlockSpec((LANES, LANES), const),
                  pl.BlockSpec((LANES, GROUP_WIDTH), const),
                  pl.BlockSpec((LANES, GROUP_WIDTH), const),
                  pl.BlockSpec((CHUNK, CHUNK), const)],
        out_specs=pl.BlockSpec((tt, GROUP_WIDTH), row),
        scratch_shapes=[pltpu.VMEM((tt + 8, 3 * GROUP_WIDTH), F32),
                        pltpu.VMEM((N_HEADS, HEAD_DIM, HEAD_DIM), F32)],
        compiler_params=_params("parallel", "arbitrary"),
        name="gated_deltanet",
    )(b_qkv, b_z, small, conv_w.astype(F32), _row128(a_log, _LANE_GDN_A), _row128(dt_bias, _LANE_GDN_A),
      jnp.tile(norm_g.astype(F32), N_HEADS).reshape(1, GROUP_WIDTH), _group_matrix(LANES, HEAD_DIM, False),
      _expand_matrix(_LANE_GDN_A, N_HEADS, HEAD_DIM), _expand_matrix(_LANE_GDN_B, N_HEADS, HEAD_DIM), tri)


def _mlstm_kernel(qk_ref, v_ref, op_ref, sm_ref, ib_ref, fb_ref, ng_ref, gm_ref, ei_ref, ef_ref, tri_ref,
                  o_ref, c_sc, m_sc, *, tt):
    @pl.when(pl.program_id(1) == 0)
    def _():
        c_sc[...] = jnp.zeros(c_sc.shape, F32)
        m_sc[...] = jnp.zeros(m_sc.shape, F32)

    sm = sm_ref[...]
    ig = GATE_CAP * jnp.tanh((sm + ib_ref[...]) * (1.0 / GATE_CAP))
    lf = jax.nn.log_sigmoid(GATE_CAP * jnp.tanh((sm + fb_ref[...]) * (1.0 / GATE_CAP)))
    ix = _dot_c(ig, ei_ref[...])
    fx = _dot_c(lf, ef_ref[...])
    nqk = N_HEADS * MLSTM_QK
    q = qk_ref[:, 0:nqk]
    k = qk_ref[:, nqk:2 * nqk] * MLSTM_QK ** -0.5
    v = v_ref[...]
    ii = lax.broadcasted_iota(jnp.int32, (CHUNK, CHUNK), 0)
    jj = lax.broadcasted_iota(jnp.int32, (CHUNK, CHUNK), 1)
    one_col = jnp.where(jj == 0, 1.0, 0.0)
    nchunk = tt // CHUNK
    cb, dlog, blast, mloc, qh, vaug, cloc, qk_nt = [], [], [], [], [], [], [], []
    for c in range(nchunk):
        r = slice(c * CHUNK, (c + 1) * CHUNK)
        bx = _c_dot(tri_ref[...], fx[r])
        rowv_all = bx - ix[r]
        rowv_t = rowv_all.T
        for h in range(N_HEADS):
            hl = slice(h * LANES, (h + 1) * LANES)
            cb.append(bx[:, hl])
            dlog.append(jnp.where(ii >= jj, bx[:, h * LANES:h * LANES + CHUNK] - rowv_t[h * LANES:h * LANES + CHUNK, :],
                                  NEG))
            blast.append(bx[CHUNK - 1:CHUNK, hl])
            aend = blast[-1] - rowv_all[:, hl]
            mloc.append(jnp.max(aend, axis=0, keepdims=True))
            wend = jnp.exp(aend - mloc[-1])
            qh.append(q[r, h * MLSTM_QK:(h + 1) * MLSTM_QK].astype(BF16))
            kh = k[r, h * MLSTM_QK:(h + 1) * MLSTM_QK]
            vaug.append(jnp.concatenate([v[r, h * HEAD_DIM:(h + 1) * HEAD_DIM], one_col], axis=1).astype(BF16))
            cloc.append(_dot_tn((kh * wend[:, :MLSTM_QK]).astype(BF16), vaug[-1]))
            qk_nt.append(_dot_nt(qh[-1], kh.astype(BF16)))
    c_in, m_in = [], []
    c_st = [c_sc[h] for h in range(N_HEADS)]
    m_st = [m_sc[h][0:1, :] for h in range(N_HEADS)]
    for c in range(nchunk):
        for h in range(N_HEADS):
            i = c * N_HEADS + h
            c_in.append(c_st[h])
            m_in.append(m_st[h])
            m_new = jnp.maximum(blast[i] + m_st[h], mloc[i])
            c_st[h] = jnp.exp(blast[i] + m_st[h] - m_new) * c_st[h] + jnp.exp(mloc[i] - m_new) * cloc[i]
            m_st[h] = m_new
    for h in range(N_HEADS):
        c_sc[h] = c_st[h]
        m_sc[h] = jnp.broadcast_to(m_st[h], (8, LANES))
    inter = [b + m for b, m in zip(cb, m_in)]
    mt = [jnp.maximum(x, jnp.max(d, axis=-1, keepdims=True)) for x, d in zip(inter, dlog)]
    wintra = [(jnp.exp(d - m[:, :CHUNK]) * a).astype(BF16) for d, m, a in zip(dlog, mt, qk_nt)]
    numden = [jnp.exp(x - m) * _dot(qq, cc.astype(BF16)) + _dot(w, va)
              for x, m, qq, cc, w, va in zip(inter, mt, qh, c_in, wintra, vaug)]
    hout = [nd[:, :HEAD_DIM] / jnp.maximum(jnp.abs(nd[:, HEAD_DIM:HEAD_DIM + 1]), jnp.exp(-m[:, 0:1]))
            for nd, m in zip(numden, mt)]
    o = jnp.concatenate([jnp.concatenate(hout[c * N_HEADS:(c + 1) * N_HEADS], axis=1) for c in range(nchunk)], axis=0)
    normed = [_group_rms(o[:, c * LANES:(c + 1) * LANES], gm_ref[...]) for c in range(GROUP_WIDTH // LANES)]
    o = jnp.concatenate(normed, axis=1) * ng_ref[...]
    o_ref[...] = (o * jax.nn.sigmoid(op_ref[...])).astype(o_ref.dtype)


def _mlstm(c_qk, c_v, c_o, small, i_bias, f_bias, norm_g, B, S, tt=512):
    tt = min(tt, S)
    ns = S // tt
    row = lambda b, i: (b * ns + i, 0)
    const = lambda b, i: (0, 0)
    tri = jnp.asarray(np.tril(np.ones((CHUNK, CHUNK), np.float32)), BF16)
    return pl.pallas_call(
        functools.partial(_mlstm_kernel, tt=tt),
        out_shape=jax.ShapeDtypeStruct((B * S, GROUP_WIDTH), BF16),
        grid=(B, ns),
        in_specs=[pl.BlockSpec((tt, GROUP_WIDTH), row),
                  pl.BlockSpec((tt, GROUP_WIDTH), row),
                  pl.BlockSpec((tt, GROUP_WIDTH), row),
                  pl.BlockSpec((tt, LANES), row),
                  pl.BlockSpec((1, LANES), const),
                  pl.BlockSpec((1, LANES), const),
                  pl.BlockSpec((1, GROUP_WIDTH), const),
                  pl.BlockSpec((LANES, LANES), const),
                  pl.BlockSpec((LANES, N_HEADS * LANES), const),
                  pl.BlockSpec((LANES, N_HEADS * LANES), const),
                  pl.BlockSpec((CHUNK, CHUNK), const)],
        out_specs=pl.BlockSpec((tt, GROUP_WIDTH), row),
        scratch_shapes=[pltpu.VMEM((N_HEADS, MLSTM_QK, LANES), F32),
                        pltpu.VMEM((N_HEADS, 8, LANES), F32)],
        compiler_params=_params("parallel", "arbitrary"),
        name="mlstm",
    )(c_qk, c_v, c_o, small, _row128(i_bias, _LANE_I), _row128(f_bias, _LANE_F),
      jnp.tile(norm_g.astype(F32), N_HEADS).reshape(1, GROUP_WIDTH), _group_matrix(LANES, HEAD_DIM, True),
      _expand_matrix(_LANE_I, N_HEADS, LANES), _expand_matrix(_LANE_F, N_HEADS, LANES), tri)


def _memkv_kernel(mem_ref, ln_ref, wkv_ref, gk_ref, gm_ref, k_ref, v_ref):
    x = mem_ref[0]
    xn = (x * lax.rsqrt(jnp.mean(x * x, axis=-1, keepdims=True) + EPS) * ln_ref[...]).astype(BF16)
    kv = _dot(xn, wkv_ref[...])
    for c in range(GROUP_WIDTH // LANES):
        cs = slice(c * LANES, (c + 1) * LANES)
        k_ref[0, :, cs] = (_group_rms(kv[:, cs], gm_ref[...]) * gk_ref[:, cs]).astype(BF16)
    v_ref[0] = kv[:, GROUP_WIDTH:].astype(BF16)


def _memkv(mem, ln_mem, wkv, gain_k):
    B, M, _ = mem.shape
    const = lambda b: (0, 0)
    return pl.pallas_call(
        _memkv_kernel,
        out_shape=[jax.ShapeDtypeStruct((B, M, GROUP_WIDTH), BF16)] * 2,
        grid=(B,),
        in_specs=[pl.BlockSpec((1, M, D_MODEL), lambda b: (b, 0, 0)),
                  pl.BlockSpec((1, D_MODEL), const),
                  pl.BlockSpec((D_MODEL, 2 * GROUP_WIDTH), const),
                  pl.BlockSpec((1, GROUP_WIDTH), const),
                  pl.BlockSpec((LANES, LANES), const)],
        out_specs=[pl.BlockSpec((1, M, GROUP_WIDTH), lambda b: (b, 0, 0))] * 2,
        compiler_params=_params("parallel"),
        name="memory_kv",
    )(mem, ln_mem.reshape(1, D_MODEL), wkv.astype(BF16),
      jnp.tile(gain_k.astype(F32), N_HEADS).reshape(1, GROUP_WIDTH), _group_matrix(LANES, HEAD_DIM, True))


def _out_xattn_kernel(ya_ref, yb_ref, yc_ref, yd_ref, h_ref, wout_ref, lnx_ref, wq_ref, kx_ref, vx_ref, gq_ref, gm_ref,
                      wo_ref, o_ref):
    y = jnp.concatenate([ya_ref[...], yb_ref[...], yc_ref[...], yd_ref[...]], axis=1)
    h1 = h_ref[...] + _dot(y, wout_ref[...])
    hn = (h1 * lax.rsqrt(jnp.mean(h1 * h1, axis=-1, keepdims=True) + EPS) * lnx_ref[...]).astype(BF16)
    q = _dot(hn, wq_ref[...])
    chunks = []
    for c in range(GROUP_WIDTH // LANES):
        cs = slice(c * LANES, (c + 1) * LANES)
        chunks.append((_group_rms(q[:, cs], gm_ref[...]) * gq_ref[:, cs]).astype(BF16))
    qn = jnp.concatenate(chunks, axis=1)
    kx, vx = kx_ref[0], vx_ref[0]
    outs = []
    for h in range(N_HEADS):
        hs = slice(h * HEAD_DIM, (h + 1) * HEAD_DIM)
        s = _dot_nt(qn[:, hs], kx[:, hs]) * HEAD_DIM ** -0.5
        e = jnp.exp(s - jnp.max(s, axis=-1, keepdims=True))
        p = e / jnp.sum(e, axis=-1, keepdims=True)
        outs.append(_dot(p.astype(BF16), vx[:, hs]))
    o = jnp.concatenate(outs, axis=1).astype(BF16)
    o_ref[...] = h1 + _dot(o, wo_ref[...])


def _out_xattn(ys, h2d, w_out, ln_x, wq, kx, vx, gain_q, wo, B, S, tm=512):
    tm = min(tm, S)
    ns = S // tm
    M = kx.shape[1]
    row = lambda b, i: (b * ns + i, 0)
    const = lambda b, i: (0, 0)
    return pl.pallas_call(
        _out_xattn_kernel,
        out_shape=jax.ShapeDtypeStruct((B * S, D_MODEL), F32),
        grid=(B, ns),
        in_specs=[pl.BlockSpec((tm, GROUP_WIDTH), row)] * 4 + [
            pl.BlockSpec((tm, D_MODEL), row),
            pl.BlockSpec((D_MODEL, D_MODEL), const),
            pl.BlockSpec((1, D_MODEL), const),
            pl.BlockSpec((D_MODEL, GROUP_WIDTH), const),
            pl.BlockSpec((1, M, GROUP_WIDTH), lambda b, i: (b, 0, 0)),
            pl.BlockSpec((1, M, GROUP_WIDTH), lambda b, i: (b, 0, 0)),
            pl.BlockSpec((1, GROUP_WIDTH), const),
            pl.BlockSpec((LANES, LANES), const),
            pl.BlockSpec((GROUP_WIDTH, D_MODEL), const)],
        out_specs=pl.BlockSpec((tm, D_MODEL), row),
        compiler_params=_params("parallel", "parallel"),
        name="outproj_xattn",
    )(*ys, h2d, w_out.astype(BF16), ln_x.reshape(1, D_MODEL), wq.astype(BF16), kx, vx,
      jnp.tile(gain_q.astype(F32), N_HEADS).reshape(1, GROUP_WIDTH), _group_matrix(LANES, HEAD_DIM, True),
      wo.astype(BF16))


def _moe_route_kernel(h_ref, ln_ref, wrh_ref, wrl_ref, br_ref, tri_ref, hn_ref, comb_ref, rkt_ref, cnt_ref):
    tm = h_ref.shape[0]
    lane = lax.broadcasted_iota(jnp.int32, (tm, LANES), 1)
    x = h_ref[...]
    hn = x * lax.rsqrt(jnp.mean(x * x, axis=-1, keepdims=True) + EPS) * ln_ref[...]
    hn_hi = hn.astype(BF16)
    hn_ref[...] = hn_hi
    hn_lo = (hn - hn_hi.astype(F32)).astype(BF16)
    logits = _dot(hn_hi, wrh_ref[...]) + _dot(hn_hi, wrl_ref[...]) + _dot(hn_lo, wrh_ref[...]) + br_ref[...]
    lanef = lane.astype(F32)
    big = 1e4
    isg = (lane >= MOE_EXPERTS) & (lane < MOE_EXPERTS + MOE_GROUPS)
    lg = jnp.where(isg, logits, NEG)
    gmax = jnp.max(lg, axis=-1, keepdims=True)
    grp_p = 1.0 / jnp.sum(jnp.exp(lg - gmax), axis=-1, keepdims=True)
    gidx = jnp.min(jnp.where(lg == gmax, lanef, big), axis=-1, keepdims=True) - MOE_EXPERTS
    ing = (lane < MOE_EXPERTS) & ((lane >> 3).astype(F32) == gidx)
    le = jnp.where(ing, logits, NEG)
    m1 = jnp.max(le, axis=-1, keepdims=True)
    z = jnp.sum(jnp.where(ing, jnp.exp(le - m1), 0.0), axis=-1, keepdims=True)
    i1 = jnp.min(jnp.where(le == m1, lanef, big), axis=-1, keepdims=True)
    oh1 = lanef == i1
    le2 = jnp.where(oh1, NEG, le)
    m2 = jnp.max(le2, axis=-1, keepdims=True)
    i2 = jnp.min(jnp.where((le2 == m2) & ing, jnp.where(oh1, big, lanef), big), axis=-1, keepdims=True)
    oh2 = lanef == i2
    p1 = 1.0 / z
    p2 = jnp.exp(m2 - m1) / z
    tot = p1 + p2
    comb_ref[...] = jnp.where(oh1, p1 / tot * grp_p, 0.0) + jnp.where(oh2, p2 / tot * grp_p, 0.0)
    member = jnp.where(lanef == gidx, 1.0, 0.0)
    rank = _dot(tri_ref[...], member.astype(BF16))
    rkt_ref[0] = jnp.where(member > 0.5, rank, -1.0).T[0:8, :]
    cnt_ref[0] = jnp.broadcast_to(jnp.sum(member, axis=0, keepdims=True), (8, LANES))


def _moe_group_kernel(cnt_ref, hn_ref, comb_ref, rkt_ref, prev_ref, w1_ref, w3_ref, w2_ref, o_ref, acc_sc,
                      *, group, tm, ch):
    cnt = cnt_ref[pl.program_id(0) * MOE_GROUPS + group]
    acc_sc[...] = jnp.zeros(acc_sc.shape, F32)
    rkg = rkt_ref[0][group:group + 1, :]
    starts = [(0, ch)] + [(s, ch // 2) for s in range(ch, tm, ch // 2)]
    for start, ch in starts:
        @pl.when(start < cnt)
        def _():
            rows = (start + lax.broadcasted_iota(jnp.int32, (ch, 1), 0)).astype(F32)
            sel = jnp.where(rkg == rows, 1.0, 0.0).astype(BF16)
            xg = _dot(sel, hn_ref[...]).astype(BF16)
            cg = _c_dot2(sel, comb_ref[...])
            yg = jnp.zeros((ch, D_MODEL), F32)
            for e in range(MOE_PER_GROUP):
                hg = _dot(xg, w1_ref[e])
                hu = _dot(xg, w3_ref[e])
                ce = cg[:, group * MOE_PER_GROUP + e:group * MOE_PER_GROUP + e + 1]
                yg = yg + _dot((jax.nn.silu(hg) * hu * ce).astype(BF16), w2_ref[e])
            acc_sc[...] += _dot_tn(sel, yg.astype(BF16))
    o_ref[...] = prev_ref[...] + acc_sc[...]


def _moe(h2d, ln, w_group, b_group, w_expert, b_expert, w1, w3, w2, tm=1024, ch=256):
    T = h2d.shape[0]
    tm = min(tm, T)
    nt = T // tm
    pad = LANES - MOE_EXPERTS - MOE_GROUPS
    wr = jnp.concatenate([w_expert, w_group, jnp.zeros((D_MODEL, pad), F32)], axis=1)
    wr_hi = wr.astype(BF16)
    br =jnp.concatenate([b_expert, b_group, jnp.zeros((pad,), F32)]).reshape(1, LANES)
    tri = jnp.asarray(np.tril(np.ones((tm, tm), np.float32), -1), BF16)
    const = lambda i: (0, 0)
    hn, comb, rkt, cnt = pl.pallas_call(
        _moe_route_kernel,
        out_shape=[jax.ShapeDtypeStruct((T, D_MODEL), BF16),
                   jax.ShapeDtypeStruct((T, LANES), F32),
                   jax.ShapeDtypeStruct((nt, 8, tm), F32),
                   jax.ShapeDtypeStruct((nt, 8, LANES), F32)],
        grid=(nt,),
        in_specs=[pl.BlockSpec((tm, D_MODEL), lambda i: (i, 0)),
                  pl.BlockSpec((1, D_MODEL), const),
                  pl.BlockSpec((D_MODEL, LANES), const),
                  pl.BlockSpec((D_MODEL, LANES), const),
                  pl.BlockSpec((1, LANES), const),
                  pl.BlockSpec((tm, tm), const)],
        out_specs=[pl.BlockSpec((tm, D_MODEL), lambda i: (i, 0)),
                   pl.BlockSpec((tm, LANES), lambda i: (i, 0)),
                   pl.BlockSpec((1, 8, tm), lambda i: (i, 0, 0)),
                   pl.BlockSpec((1, 8, LANES), lambda i: (i, 0, 0))],
        compiler_params=_params("parallel"),
        name="moe_route",
    )(h2d, ln.reshape(1, D_MODEL), wr_hi, (wr - wr_hi.astype(F32)).astype(BF16), br, tri)
    counts = cnt[:, 0, :MOE_GROUPS].astype(jnp.int32).reshape(nt * MOE_GROUPS)
    w1b, w3b, w2b = w1.astype(BF16), w3.astype(BF16), w2.astype(BF16)
    out = h2d
    for g in range(MOE_GROUPS):
        wspec = lambda shape: pl.BlockSpec((MOE_PER_GROUP,) + shape, lambda i, c, g=g: (g, 0, 0),
                                           pipeline_mode=pl.Buffered(1))
        out = pl.pallas_call(
            functools.partial(_moe_group_kernel, group=g, tm=tm, ch=ch),
            out_shape=jax.ShapeDtypeStruct((T, D_MODEL), F32),
            grid_spec=pltpu.PrefetchScalarGridSpec(
                num_scalar_prefetch=1,
                grid=(nt,),
                in_specs=[pl.BlockSpec((tm, D_MODEL), lambda i, c: (i, 0)),
                          pl.BlockSpec((tm, LANES), lambda i, c: (i, 0)),
                          pl.BlockSpec((1, 8, tm), lambda i, c: (i, 0, 0)),
                          pl.BlockSpec((tm, D_MODEL), lambda i, c: (i, 0)),
                          wspec((D_MODEL, MOE_FF)), wspec((D_MODEL, MOE_FF)), wspec((MOE_FF, D_MODEL))],
                out_specs=pl.BlockSpec((tm, D_MODEL), lambda i, c: (i, 0)),
                scratch_shapes=[pltpu.VMEM((tm, D_MODEL), F32)]),
            compiler_params=_params("parallel"),
            name="moe_group",
        )(counts, hn, comb, rkt, out, w1b, w3b, w2b)
    return out


def _nsa_mixer(a_q, a_kv, kn, small, qk_gain, cmp_pe, cmp_w1, cmp_w2, cos, sin, B, S):
    nc = S // NSA_CMP_STRIDE
    wide = NSA_CMP_STRIDE * HEAD_DIM
    cos_c, sin_c = _rope_tables(NSA_CMP_STRIDE * np.arange(nc) + NSA_CMP_LEN - 1, HEAD_DIM)
    half = HEAD_DIM // 2
    kc, vc = _nsa_compress(a_kv, cmp_pe.reshape(2, 2 * wide), cmp_w1, cmp_w2, qk_gain[1],
                           cos_c[:, :half], sin_c[:, half:2 * half], B, nc)
    gq_row = jnp.tile(qk_gain[0].astype(F32), N_HEADS).reshape(1, GROUP_WIDTH)
    qn, part, sel = _nsa1(a_q, small, kc, vc, kn, a_kv, cos, sin, gq_row, B, S)
    return _nsa2(qn, sel, kn, a_kv, part, small, B, S)


def kernel(x, mem, ln_mix, w_in, w_out, nsa_qk_gain, nsa_cmp_pe, nsa_cmp_w1, nsa_cmp_w2, gdn_conv, gdn_a_log, gdn_dt_bias, gdn_norm, mlstm_i_bias, mlstm_f_bias, mlstm_norm, diff_qk_gain, diff_lambda, diff_norm, ln_xattn, ln_mem, xattn_wq, xattn_wkv, xattn_qk_gain, xattn_wo, ln_moe, moe_w_group, moe_b_group, moe_w_expert, moe_b_expert, moe_w1, moe_w3, moe_w2):
    B, S, D = x.shape
    depth = w_in.shape[0]
    cos_a, sin_a = _rope_tables(np.arange(S), HEAD_DIM)
    cos_d, sin_d = _rope_tables(np.arange(S), DIFF_SUB)
    h = x.reshape(B * S, D)
    for l in range(depth):
        gain_k = jnp.concatenate([nsa_qk_gain[l, 2], nsa_qk_gain[l, 3]]).reshape(1, LANES).astype(F32)
        gain_d = jnp.concatenate([jnp.tile(diff_qk_gain[l, 0], 2 * N_HEADS) * DIFF_SUB ** -0.5,
                                  jnp.tile(diff_qk_gain[l, 1], 2 * N_HEADS)]).reshape(1, 2 * GROUP_WIDTH).astype(F32)
        a_q, a_kv, b_qkv, b_z, c_qk, c_v, c_o, d_v, small, dqk, kn = _inproj(
            h, ln_mix[l], w_in[l], gain_k, gain_d, cos_a, sin_a, cos_d, sin_d, S)
        y_a = _nsa_mixer(a_q, a_kv, kn, small, nsa_qk_gain[l], nsa_cmp_pe[l], nsa_cmp_w1[l], nsa_cmp_w2[l],
                         cos_a, sin_a, B, S)
        y_b = _gdn(b_qkv, b_z, small, gdn_conv[l], gdn_a_log[l], gdn_dt_bias[l], gdn_norm[l], B, S)
        y_c = _mlstm(c_qk, c_v, c_o, small, mlstm_i_bias[l], mlstm_f_bias[l], mlstm_norm[l], B, S)
        lambda_init = 0.8 - 0.6 * math.exp(-0.3 * l)
        y_d = _diff_attention(dqk, d_v, diff_lambda[l], diff_norm[l], lambda_init, B, S)
        kx, vx = _memkv(mem, ln_mem[l], xattn_wkv[l], xattn_qk_gain[l, 1])
        h = _out_xattn((y_a, y_b, y_c, y_d), h, w_out[l], ln_xattn[l], xattn_wq[l], kx, vx,
                       xattn_qk_gain[l, 0], xattn_wo[l], B, S)
        h = _moe(h, ln_moe[l], moe_w_group[l], moe_b_group[l], moe_w_expert[l], moe_b_expert[l],
                 moe_w1[l], moe_w3[l], moe_w2[l])
    return h.reshape(B, S, D)
```

```python
import functools
import math

import numpy as np
import jax
import jax.numpy as jnp
from jax import lax
from jax.experimental import pallas as pl
from jax.experimental.pallas import tpu as pltpu

F32 = jnp.float32
BF16 = jnp.bfloat16
HI = lax.Precision.HIGHEST

D_MODEL = 1024
HEAD_DIM = 64
N_HEADS = 4
GROUP_WIDTH = 256
ROPE_THETA = 10000.0
EPS = 1e-6
NEG = -1e30

NSA_CMP_LEN = 32
NSA_CMP_STRIDE = 16
NSA_SEL_LEN = 64
NSA_TOP_N = 16
NSA_WINDOW = 512
NSA_FORCE_BONUS = 1e3
CHUNK = 64
MLSTM_QK = 32
GATE_CAP = 15.0
DIFF_SUB = 32
MOE_GROUPS = 4
MOE_PER_GROUP = 8
MOE_EXPERTS = 32
MOE_FF = 256
LANES = 128
VMEM_LIMIT = 48 * 1024 * 1024

_A0, _B0, _C0, _D0 = 0, 652, 1684, 2460
_GROUPS = (
    ("a_q", ((_A0, 256),), F32),
    ("a_kv", ((_A0 + 384, 64), (_A0 + 512, 64), (_A0 + 448, 64), (None, 64), (_A0 + 576, 64), (None, 64),
              (_A0 + 256, 64), (_A0 + 320, 64)), F32),
    ("b_qkv", ((_B0, 768),), F32),
    ("b_z", ((_B0 + 776, 256),), F32),
    ("c_qk", ((_C0, 256),), F32),
    ("c_v", ((_C0 + 256, 256),), F32),
    ("c_o", ((_C0 + 520, 256),), F32),
    ("d_qk", ((_D0, 512),), F32),
    ("d_v", tuple(p for h in range(N_HEADS) for p in ((_D0 + 512 + 64 * h, 64), (None, 64))), BF16),
    ("small", ((_A0 + 640, 12), (_B0 + 768, 4), (_B0 + 772, 4), (_C0 + 512, 4), (_C0 + 516, 4), (None, 100)), F32),
)
_LANE_GDN_A, _LANE_GDN_B, _LANE_I, _LANE_F = 12, 16, 20, 24


def _dot(a, b, prec=None):
    return jnp.dot(a, b, preferred_element_type=F32, precision=prec)


def _dot_nt(a, b, prec=None):
    return lax.dot_general(a, b, (((1,), (1,)), ((), ())), preferred_element_type=F32, precision=prec)


def _dot_tn(a, b, prec=None):
    return lax.dot_general(a, b, (((0,), (0,)), ((), ())), preferred_element_type=F32, precision=prec)


def _bdot(a, b):
    return _dot(a.astype(BF16), b.astype(BF16))


def _bdot_nt(a, b):
    return _dot_nt(a.astype(BF16), b.astype(BF16))


def _bdot_tn(a, b):
    return _dot_tn(a.astype(BF16), b.astype(BF16))


def _split3(a):
    hi = a.astype(BF16)
    r = a - hi.astype(F32)
    mid = r.astype(BF16)
    return hi, mid, (r - mid.astype(F32)).astype(BF16)


def _dot_c(a, c):
    hi, mid, lo = _split3(a)
    return _dot(hi, c) + _dot(mid, c) + _dot(lo, c)


def _c_dot(c, b):
    hi, mid, lo = _split3(b)
    return _dot(c, hi) + _dot(c, mid) + _dot(c, lo)


def _dot_c2(a, c):
    hi = a.astype(BF16)
    return _dot(hi, c) + _dot((a - hi.astype(F32)).astype(BF16), c)


def _c_dot2(c, b):
    hi = b.astype(BF16)
    return _dot(c, hi) + _dot(c, (b - hi.astype(F32)).astype(BF16))


def _dot_nt_x3(a, b):
    ah, bh = a.astype(BF16), b.astype(BF16)
    al, bl = (a - ah.astype(F32)).astype(BF16), (b - bh.astype(F32)).astype(BF16)
    return _dot_nt(ah, bh) + _dot_nt(ah, bl) + _dot_nt(al, bh)


def _params(*sem):
    return pltpu.CompilerParams(dimension_semantics=sem, vmem_limit_bytes=VMEM_LIMIT)


def _group_matrix(width, gsz, mean):
    g = np.kron(np.eye(width // gsz), np.ones((gsz, gsz)))
    return jnp.asarray(g / gsz if mean else g, BF16)


def _expand_matrix(src_lane0, n, out_per, stride=1):
    e = np.zeros((LANES, n * out_per), np.float32)
    for h in range(n):
        e[src_lane0 + stride * h, h * out_per:(h + 1) * out_per] = 1.0
    return jnp.asarray(e, BF16)


def _row128(vals, lane0):
    return jnp.zeros((1, LANES), F32).at[0, lane0:lane0 + vals.shape[0]].set(vals.astype(F32))


def _ones_col_row():
    return jnp.zeros((1, LANES), BF16).at[0, HEAD_DIM].set(1.0)


def _rope_tables(pos, dim):
    inv = 1.0 / (ROPE_THETA ** (jnp.arange(0, dim, 2, dtype=F32) / dim))
    ang = jnp.asarray(pos).astype(F32)[:, None] * inv[None, :]
    cos, sin = jnp.cos(ang), jnp.sin(ang)
    cosd = jnp.concatenate([cos, cos], axis=-1)
    sind = jnp.concatenate([-sin, sin], axis=-1)
    rep = LANES // dim
    return jnp.tile(cosd, (1, rep)), jnp.tile(sind, (1, rep))


def _rope128(x, cos, sin_signed, half):
    left = pltpu.roll(x, LANES - half, 1)
    right = pltpu.roll(x, half, 1)
    lane = lax.broadcasted_iota(jnp.int32, x.shape, 1)
    first = (lane & (2 * half - 1)) < half
    return x * cos + jnp.where(first, left, right) * sin_signed


def _rep_lanes(x, width):
    return x if width == LANES else jnp.concatenate([x] * (width // LANES), axis=1)


def _softmax_rows(s, mask):
    m = jnp.max(s, axis=-1, keepdims=True)
    e = jnp.exp(s - m)
    return jnp.where(mask, e * (1.0 / jnp.sum(e, axis=-1, keepdims=True)), 0.0)


def _group_rms(x, gm):
    return x * lax.rsqrt(_dot_c2(x * x, gm) + EPS)


def _flash_update(items, m_ref, acc_ref, tk):
    items = [it if len(it) == 4 else it + (slice(None),) for it in items]
    m_prev = [m_ref[i, r] for _, _, i, r in items]
    m_new = [jnp.maximum(mp, jnp.max(s, axis=-1, keepdims=True)) for mp, (s, _, _, _) in zip(m_prev, items)]
    p = [jnp.exp((s - _rep_lanes(mn, s.shape[1])).astype(BF16)) for mn, (s, _, _, _) in zip(m_new, items)]
    pv = [_dot(pp, va) for pp, (_, va, _, _) in zip(p, items)]
    for mp, mn, x, (_, _, i, r) in zip(m_prev, m_new, pv, items):
        acc_ref[i, r] = jnp.exp(mp - mn) * acc_ref[i, r] + x
        m_ref[i, r] = mn


def _flash_result(acc_ref, idx):
    acc = acc_ref[idx]
    return acc[:, :HEAD_DIM] * (1.0 / acc[:, HEAD_DIM:HEAD_DIM + 1])


def _normrope(y, gain, gm, cos, sin, half):
    out = []
    for c in range(y.shape[1] // LANES):
        cs = slice(c * LANES, (c + 1) * LANES)
        out.append(_rope128(_group_rms(y[:, cs], gm) * gain[:, cs], cos, sin, half))
    return out[0] if len(out) == 1 else jnp.concatenate(out, axis=1)


def _inproj_kernel(x_ref, g_ref, w_ref, gk_ref, gd_ref, gm64_ref, gm32_ref, cosa_ref, sina_ref, cosd_ref, sind_ref,
                   *outs, widths):
    x = x_ref[...]
    ms = jnp.mean(x * x, axis=-1, keepdims=True)
    xn = (x * lax.rsqrt(ms + EPS) * g_ref[...]).astype(BF16)
    outs = list(outs)
    kn_ref, dqk_ref = outs.pop(), outs.pop()
    off = 0
    for (name, _, _), wd in zip(_GROUPS, widths):
        y = _dot(xn, w_ref[:, off:off + wd])
        off += wd
        if name == "d_qk":
            dqk_ref[...] = _normrope(y, gd_ref[...], gm32_ref[...], cosd_ref[...], sind_ref[...],
                                     DIFF_SUB // 2).astype(BF16)
            continue
        o = outs.pop(0)
        o[...] = y.astype(o.dtype)
        if name == "a_kv":
            kn_ref[...] = _normrope(y[:, :LANES], gk_ref[...], gm64_ref[...], cosa_ref[...], sina_ref[...],
                                    HEAD_DIM // 2).astype(BF16)


def _permute_w_in(w):
    cols, widths = [], []
    for _, parts, _ in _GROUPS:
        for s, n in parts:
            cols.append(jnp.zeros((w.shape[0], n), w.dtype) if s is None else w[:, s:s + n])
        widths.append(sum(n for _, n in parts))
    return jnp.concatenate(cols, axis=1).astype(BF16), tuple(widths)


def _inproj(h2d, gain, w_in, gain_k, gain_d, cos_a, sin_a, cos_d, sin_d, S, tm=256):
    T = h2d.shape[0]
    tm = min(tm, S)
    ns = S // tm
    wp, widths = _permute_w_in(w_in)
    kept = [(wd, g) for wd, g in zip(widths, _GROUPS) if g[0] != "d_qk"]
    out_shape = [jax.ShapeDtypeStruct((T, wd), g[2]) for wd, g in kept]
    out_shape += [jax.ShapeDtypeStruct((T, 2 * GROUP_WIDTH), BF16), jax.ShapeDtypeStruct((T, LANES), BF16)]
    const = lambda i: (0, 0)
    table = pl.BlockSpec((tm, LANES), lambda i: (i % ns, 0))
    return pl.pallas_call(
        functools.partial(_inproj_kernel, widths=widths),
        out_shape=out_shape,
        grid=(T // tm,),
        in_specs=[pl.BlockSpec((tm, D_MODEL), lambda i: (i, 0)),
                  pl.BlockSpec((1, D_MODEL), const),
                  pl.BlockSpec((D_MODEL, sum(widths)), const),
                  pl.BlockSpec((1, LANES), const),
                  pl.BlockSpec((1, 2 * GROUP_WIDTH), const),
                  pl.BlockSpec((LANES, LANES), const),
                  pl.BlockSpec((LANES, LANES), const),
                  table, table, table, table],
        out_specs=[pl.BlockSpec((tm, s.shape[1]), lambda i: (i, 0)) for s in out_shape],
        compiler_params=_params("parallel"),
        name="inproj",
    )(h2d, gain.reshape(1, D_MODEL), wp, gain_k, gain_d, _group_matrix(LANES, HEAD_DIM, True),
      _group_matrix(LANES, DIFF_SUB, True), cos_a, sin_a, cos_d, sin_d)


def _nsa_cmp_kernel(kv_ref, pe_ref, w1_ref, w2_ref, gain_ref, cos_ref, sin_ref, kc_ref, vc_ref, *, nc):
    half_in = NSA_CMP_STRIDE * HEAD_DIM

    a = [jnp.zeros((nc, 2 * HEAD_DIM), F32) for _ in range(2)]
    b = [jnp.zeros((nc, 2 * HEAD_DIM), F32) for _ in range(2)]
    for l in range(NSA_CMP_STRIDE):
        xl = kv_ref[pl.ds(l, nc, stride=NSA_CMP_STRIDE), :]
        for j in range(2):
            xj = xl[:, j * HEAD_DIM:(j + 1) * HEAD_DIM]
            a[j] = a[j] + _dot(xj, w1_ref[j, l * HEAD_DIM:(l + 1) * HEAD_DIM, :], HI)
            b[j] = b[j] + _dot(xj, w1_ref[j, half_in + l * HEAD_DIM:half_in + (l + 1) * HEAD_DIM, :], HI)

    def finish(j):
        pe = jnp.broadcast_to(pe_ref[j], (8, 2 * half_in))
        c = _dot(pe, w1_ref[j], HI)[0:1]
        hid = jax.nn.gelu(a[j] + pltpu.roll(b[j], nc - 1, 0) + c)
        return _dot(hid, w2_ref[j], HI)

    kc = finish(0)
    kc = kc * lax.rsqrt(jnp.mean(kc * kc, axis=-1, keepdims=True) + EPS) * gain_ref[...]
    x1, x2 = kc[:, :HEAD_DIM // 2], kc[:, HEAD_DIM // 2:]
    cos, sin = cos_ref[...], sin_ref[...]
    kc_ref[0] = jnp.concatenate([x1 * cos - x2 * sin, x2 * cos + x1 * sin], axis=1)
    vc_ref[0] = finish(1)


def _nsa_compress(a_kv, pe, w1, w2, gain, cos_c, sin_c, B, nc):
    wide = NSA_CMP_STRIDE * HEAD_DIM
    return pl.pallas_call(
        functools.partial(_nsa_cmp_kernel, nc=nc),
        out_shape=[jax.ShapeDtypeStruct((B, nc, HEAD_DIM), F32)] * 2,
        grid=(B,),
        in_specs=[pl.BlockSpec((nc * NSA_CMP_STRIDE, LANES), lambda b: (b, 3)),
                  pl.BlockSpec((2, 1, 2 * wide), lambda b: (0, 0, 0)),
                  pl.BlockSpec((2, 2 * wide, 2 * HEAD_DIM), lambda b: (0, 0, 0)),
                  pl.BlockSpec((2, 2 * HEAD_DIM, HEAD_DIM), lambda b: (0, 0, 0)),
                  pl.BlockSpec((1, HEAD_DIM), lambda b: (0, 0)),
                  pl.BlockSpec((nc, HEAD_DIM // 2), lambda b: (0, 0)),
                  pl.BlockSpec((nc, HEAD_DIM // 2), lambda b: (0, 0))],
        out_specs=[pl.BlockSpec((1, nc, HEAD_DIM), lambda b: (b, 0, 0))] * 2,
        compiler_params=_params("parallel"),
        name="nsa_compress",
    )(a_kv, pe.reshape(2, 1, 2 * wide), w1, w2, gain.reshape(1, HEAD_DIM), cos_c, sin_c)


def _nsa1_kernel(q_ref, sm_ref, kc_ref, vc_ref, kn_ref, vw_ref, cos_ref, sin_ref, gq_ref, gm_ref, ovl_ref,
                 eg0_ref, eg2_ref, qn_ref, part_ref, sel_ref, *, tq, nc, n_sel):
    t0 = pl.program_id(1) * tq
    chunks = []
    for c in range(GROUP_WIDTH // LANES):
        cs = slice(c * LANES, (c + 1) * LANES)
        xn = _group_rms(q_ref[:, cs], gm_ref[...]) * gq_ref[:, cs]
        chunks.append(_rope128(xn, cos_ref[...], sin_ref[...], HEAD_DIM // 2))
    qs = jnp.concatenate(chunks, axis=1) * HEAD_DIM ** -0.5
    qb = qs.astype(BF16)
    qn_ref[...] = qb
    sig = jax.nn.sigmoid(sm_ref[...])
    g0x = _dot_c2(sig, eg0_ref[...])
    g2x = _dot_c2(sig, eg2_ref[...])
    tpos = t0 + lax.broadcasted_iota(jnp.int32, (tq, 1), 0)

    kc = kc_ref[0]
    vc = vc_ref[0].astype(BF16)
    cidx = lax.broadcasted_iota(jnp.int32, (1, nc), 1)
    cmask = ((NSA_CMP_STRIDE * cidx + NSA_CMP_LEN - 1) <= tpos) & (cidx < nc - 1)
    psum = jnp.zeros((tq, nc), F32)
    o_cmp = []
    for h in range(N_HEADS):
        hs = slice(h * HEAD_DIM, (h + 1) * HEAD_DIM)
        p = _softmax_rows(jnp.where(cmask, _dot_nt_x3(qs[:, hs], kc), NEG), cmask)
        o_cmp.append(_dot(p.astype(BF16), vc))
        psum = psum + p

    imp = _dot_c(psum, ovl_ref[...])
    j = lax.broadcasted_iota(jnp.int32, (tq, LANES), 1)
    cur = tpos >> 6
    valid = j <= cur
    forced = (j == 0) | (j == cur) | (j == cur - 1)
    score = jnp.where(valid, imp + jnp.where(forced, NSA_FORCE_BONUS, 0.0), NEG)
    nrow = -(-n_sel // 8) * 8
    st = score.T[0:nrow, :]
    blocks = [st[8 * g:8 * g + 8, :] for g in range(nrow // 8)]
    cnts = [jnp.zeros((8, tq), F32) for _ in blocks]
    j8 = lax.broadcasted_iota(jnp.int32, (8, 1), 0)
    for i in range(n_sel):
        row = st[i:i + 1, :]
        for g, blk in enumerate(blocks):
            if 8 * g > i:
                won = jnp.where(row >= blk, 1.0, 0.0)
            elif 8 * g + 7 < i:
                won = jnp.where(row > blk, 1.0, 0.0)
            else:
                tie = jnp.where(j8 + 8 * g > i, 1.0, 0.0)
                won = jnp.where(row > blk, 1.0, jnp.where(row == blk, tie, 0.0))
            cnts[g] = cnts[g] + won
    cnt = jnp.concatenate(cnts, axis=0)
    sel_t = jnp.where((cnt < min(NSA_TOP_N, n_sel)) & (st > 0.5 * NEG), 1.0, 0.0)
    if nrow < LANES:
        sel_t = jnp.concatenate([sel_t, jnp.zeros((LANES - nrow, tq), F32)], axis=0)
    sel_ref[...] = sel_t.T.astype(BF16)

    band = tq + NSA_WINDOW
    start = pl.multiple_of(jnp.maximum(t0 - NSA_WINDOW, 0), LANES)
    kw = kn_ref[pl.ds(start, band), HEAD_DIM:2 * HEAD_DIM]
    one_row = jnp.where(lax.broadcasted_iota(jnp.int32, (1, LANES), 1) == HEAD_DIM, 1.0, 0.0).astype(BF16)
    vaug = vw_ref[pl.ds(start, band), :].astype(BF16) + one_row
    dist = tpos - (start + lax.broadcasted_iota(jnp.int32, (1, band), 1))
    wmask = (dist >= 0) & (dist < NSA_WINDOW)
    s_win = [jnp.where(wmask, _dot_nt(qb[:, h * HEAD_DIM:(h + 1) * HEAD_DIM], kw), NEG) for h in range(N_HEADS)]
    p_win = [jnp.exp((s - jnp.max(s, axis=-1, keepdims=True)).astype(BF16)) for s in s_win]
    pv = [_dot(p, vaug) for p in p_win]
    o_win = [x[:, :HEAD_DIM] * (1.0 / x[:, HEAD_DIM:HEAD_DIM + 1]) for x in pv]
    part_ref[...] = g0x * jnp.concatenate(o_cmp, axis=1) + g2x * jnp.concatenate(o_win, axis=1)


def _nsa_overlap(nc, n_sel):
    c0 = NSA_CMP_STRIDE * np.arange(nc)[:, None]
    s0 = NSA_SEL_LEN * np.arange(n_sel)[None, :]
    ov = np.clip(np.minimum(c0 + NSA_CMP_LEN, s0 + NSA_SEL_LEN) - np.maximum(c0, s0), 0, None) / NSA_CMP_STRIDE
    ov[nc - 1:] = 0.0
    out = np.zeros((nc, LANES), np.float32)
    out[:, :n_sel] = ov
    return jnp.asarray(out, BF16)


def _nsa1(a_q, small, kc, vc, kn, a_kv, cos, sin, gq_row, B, S, tq=256):
    tq = min(tq, S)
    nq = S // tq
    nc = S // NSA_CMP_STRIDE
    n_sel = S // NSA_SEL_LEN
    row = lambda b, i: (b * nq + i, 0)
    return pl.pallas_call(
        functools.partial(_nsa1_kernel, tq=tq, nc=nc, n_sel=n_sel),
        out_shape=[jax.ShapeDtypeStruct((B * S, GROUP_WIDTH), BF16),
                   jax.ShapeDtypeStruct((B * S, GROUP_WIDTH), F32),
                   jax.ShapeDtypeStruct((B * S, LANES), BF16)],
        grid=(B, nq),
        in_specs=[pl.BlockSpec((tq, GROUP_WIDTH), row),
                  pl.BlockSpec((tq, LANES), row),
                  pl.BlockSpec((1, nc, HEAD_DIM), lambda b, i: (b, 0, 0)),
                  pl.BlockSpec((1, nc, HEAD_DIM), lambda b, i: (b, 0, 0)),
                  pl.BlockSpec((S, LANES), lambda b, i: (b, 0)),
                  pl.BlockSpec((S, LANES), lambda b, i: (b, 2)),
                  pl.BlockSpec((tq, LANES), lambda b, i: (i, 0)),
                  pl.BlockSpec((tq, LANES), lambda b, i: (i, 0)),
                  pl.BlockSpec((1, GROUP_WIDTH), lambda b, i: (0, 0)),
                  pl.BlockSpec((LANES, LANES), lambda b, i: (0, 0)),
                  pl.BlockSpec((nc, LANES), lambda b, i: (0, 0)),
                  pl.BlockSpec((LANES, GROUP_WIDTH), lambda b, i: (0, 0)),
                  pl.BlockSpec((LANES, GROUP_WIDTH), lambda b, i: (0, 0))],
        out_specs=[pl.BlockSpec((tq, GROUP_WIDTH), row),
                   pl.BlockSpec((tq, GROUP_WIDTH), row),
                   pl.BlockSpec((tq, LANES), row)],
        compiler_params=_params("parallel", "parallel"),
        name="nsa_cmp_win_select",
    )(a_q, small, kc, vc, kn, a_kv, cos, sin, gq_row, _group_matrix(LANES, HEAD_DIM, True),
      _nsa_overlap(nc, n_sel), _expand_matrix(0, N_HEADS, HEAD_DIM, 3), _expand_matrix(2, N_HEADS, HEAD_DIM, 3))


def _causal_tiles(nq, tq, tk):
    pairs = [(i, k) for i in range(nq) for k in range((i * tq + tq - 1) // tk + 1)]
    return (jnp.asarray([p[0] for p in pairs], jnp.int32), jnp.asarray([p[1] for p in pairs], jnp.int32))


def _nsa2_kernel(qt_ref, kt_ref, qn_ref, sel_ref, e_ref, kn_ref, vs_ref, one_ref, part_ref, sm_ref, eg1_ref, o_ref,
                 m_sc, acc_sc, *, tq, tk):
    qi = qt_ref[pl.program_id(1)]
    ki = kt_ref[pl.program_id(1)]

    @pl.when(ki == 0)
    def _():
        m_sc[...] = jnp.full(m_sc.shape, NEG, F32)
        acc_sc[...] = jnp.zeros(acc_sc.shape, F32)

    def step(causal):
        ks = kn_ref[:, 0:HEAD_DIM]
        vaug = vs_ref[...].astype(BF16) + one_ref[...]
        half = tq // 2
        if causal:
            tpos = qi * tq + lax.broadcasted_iota(jnp.int32, (tq, 1), 0)
            kpos = ki * tk + lax.broadcasted_iota(jnp.int32, (1, tk), 1)
            mask_top = (_dot(sel_ref[0:half, :], e_ref[:, 0:half]) > 0.5) & (kpos[:, 0:half] <= tpos[0:half])
            mask_bot = (_dot(sel_ref[half:tq, :], e_ref[...]) > 0.5) & (kpos <= tpos[half:tq])
        else:
            mask = _dot(sel_ref[...], e_ref[...]) > 0.5
        for h0 in range(0, N_HEADS, 2):
            items = []
            for h in (h0, h0 + 1):
                hs = slice(h * HEAD_DIM, (h + 1) * HEAD_DIM)
                if not causal:
                    items.append((jnp.where(mask, _dot_nt(qn_ref[:, hs], ks), NEG), vaug, h))
                    continue
                s_top = _dot_nt(qn_ref[0:half, hs], ks[0:half])
                items.append((jnp.where(mask_top, s_top, NEG), vaug[0:half], h, slice(0, half)))
                s_bot = _dot_nt(qn_ref[half:tq, hs], ks)
                items.append((jnp.where(mask_bot, s_bot, NEG), vaug, h, slice(half, tq)))
            _flash_update(items, m_sc, acc_sc, tk)

    @pl.when(ki * tk + tk - 1 <= qi * tq)
    def _():
        step(False)

    @pl.when(ki * tk + tk - 1 > qi * tq)
    def _():
        step(True)

    @pl.when(ki == (qi * tq + tq - 1) // tk)
    def _():
        g1x = _dot_c2(jax.nn.sigmoid(sm_ref[...]), eg1_ref[...])
        o = jnp.concatenate([_flash_result(acc_sc, h) for h in range(N_HEADS)], axis=1)
        o_ref[...] = (part_ref[...] + g1x * o).astype(o_ref.dtype)


def _nsa2(qn, sel, kn, a_kv, part, small, B, S, tq=512, tk=512):
    tq, tk = min(tq, S), min(tk, S)
    assert tq == tk, "the diagonal-tile split assumes square tiles"
    nq, nk = S // tq, S // tk
    e = np.zeros((LANES, S), np.float32)
    e[np.arange(S) // NSA_SEL_LEN, np.arange(S)] = 1.0
    qt, kt = _causal_tiles(nq, tq, tk)
    row = lambda b, j, qt, kt: (b * nq + qt[j], 0)
    const = lambda b, j, qt, kt: (0, 0)
    return pl.pallas_call(
        functools.partial(_nsa2_kernel, tq=tq, tk=tk),
        out_shape=jax.ShapeDtypeStruct((B * S, GROUP_WIDTH), BF16),
        grid_spec=pltpu.PrefetchScalarGridSpec(
            num_scalar_prefetch=2,
            grid=(B, qt.shape[0]),
            in_specs=[pl.BlockSpec((tq, GROUP_WIDTH), row),
                      pl.BlockSpec((tq, LANES), row),
                      pl.BlockSpec((LANES, tk), lambda b, j, qt, kt: (0, kt[j])),
                      pl.BlockSpec((tk, LANES), lambda b, j, qt, kt: (b * nk + kt[j], 0)),
                      pl.BlockSpec((tk, LANES), lambda b, j, qt, kt: (b * nk + kt[j], 1)),
                      pl.BlockSpec((1, LANES), const),
                      pl.BlockSpec((tq, GROUP_WIDTH), row),
                      pl.BlockSpec((tq, LANES), row),
                      pl.BlockSpec((LANES, GROUP_WIDTH), const)],
            out_specs=pl.BlockSpec((tq, GROUP_WIDTH), row),
            scratch_shapes=[pltpu.VMEM((N_HEADS, tq, LANES), F32),
                            pltpu.VMEM((N_HEADS, tq, LANES), F32)]),
        compiler_params=_params("parallel", "arbitrary"),
        name="nsa_selected",
    )(qt, kt, qn, sel, jnp.asarray(e, BF16), kn, a_kv, _ones_col_row(), part, small,
      _expand_matrix(1, N_HEADS, HEAD_DIM, 3))


def _diff_kernel(qt_ref, kt_ref, q_ref, k_ref, v_ref, one_ref, lam_ref, gain_ref, o_ref, m_sc, acc_sc,
                 *, tq, tk, lambda_init):
    qi = qt_ref[pl.program_id(1)]
    ki = kt_ref[pl.program_id(1)]

    @pl.when(ki == 0)
    def _():
        m_sc[...] = jnp.full(m_sc.shape, NEG, F32)
        acc_sc[...] = jnp.zeros(acc_sc.shape, F32)

    def step(causal):
        half = tq // 2
        if causal:
            tpos = qi * tq + lax.broadcasted_iota(jnp.int32, (tq, 1), 0)
            kpos = ki * tk + lax.broadcasted_iota(jnp.int32, (1, tk), 1)
            mask_top = kpos[:, 0:half] <= tpos[0:half]
            mask_bot = kpos <= tpos[half:tq]
        for h in range(N_HEADS):
            vaug = v_ref[:, h * LANES:(h + 1) * LANES] + one_ref[...]
            items = []
            for idx in (2 * h, 2 * h + 1):
                cs = slice(idx * DIFF_SUB, (idx + 1) * DIFF_SUB)
                if not causal:
                    items.append((_dot_nt(q_ref[:, cs], k_ref[:, cs]), vaug, idx))
                    continue
                s_top = _dot_nt(q_ref[0:half, cs], k_ref[0:half, cs])
                items.append((jnp.where(mask_top, s_top, NEG), vaug[0:half], idx, slice(0, half)))
                s_bot = _dot_nt(q_ref[half:tq, cs], k_ref[:, cs])
                items.append((jnp.where(mask_bot, s_bot, NEG), vaug, idx, slice(half, tq)))
            _flash_update(items, m_sc, acc_sc, tk)

    @pl.when(ki * tk + tk - 1 <= qi * tq)
    def _():
        step(False)

    @pl.when(ki * tk + tk - 1 > qi * tq)
    def _():
        step(True)

    @pl.when(ki == (qi * tq + tq - 1) // tk)
    def _():
        lm = lam_ref[...]
        lam = (jnp.exp(jnp.sum(lm[0:1] * lm[1:2], axis=-1, keepdims=True))
               - jnp.exp(jnp.sum(lm[2:3] * lm[3:4], axis=-1, keepdims=True)) + lambda_init)
        outs = []
        for h in range(N_HEADS):
            o = _flash_result(acc_sc, 2 * h) - lam * _flash_result(acc_sc, 2 * h + 1)
            o = o * lax.rsqrt(jnp.mean(o * o, axis=-1, keepdims=True) + EPS)
            outs.append(o * gain_ref[...] * (1.0 - lambda_init))
        o_ref[...] = jnp.concatenate(outs, axis=1).astype(o_ref.dtype)


def _diff_attention(qk, v, lam, norm_g, lambda_init, B, S, tq=512, tk=512):
    tq, tk = min(tq, S), min(tk, S)
    assert tq == tk, "the diagonal-tile split assumes square tiles"
    nq, nk = S // tq, S // tk
    lam_pad =jnp.zeros((4, LANES), F32).at[:, :DIFF_SUB].set(lam.astype(F32))
    qt, kt = _causal_tiles(nq, tq, tk)
    row = lambda b, j, qt, kt: (b * nq + qt[j], 0)
    const = lambda b, j, qt, kt: (0, 0)
    return pl.pallas_call(
        functools.partial(_diff_kernel, tq=tq, tk=tk, lambda_init=lambda_init),
        out_shape=jax.ShapeDtypeStruct((B * S, GROUP_WIDTH), BF16),
        grid_spec=pltpu.PrefetchScalarGridSpec(
            num_scalar_prefetch=2,
            grid=(B, qt.shape[0]),
            in_specs=[pl.BlockSpec((tq, GROUP_WIDTH), row),
                      pl.BlockSpec((tk, GROUP_WIDTH), lambda b, j, qt, kt: (b * nk + kt[j], 1)),
                      pl.BlockSpec((tk, N_HEADS * LANES), lambda b, j, qt, kt: (b * nk + kt[j], 0)),
                      pl.BlockSpec((1, LANES), const),
                      pl.BlockSpec((4, LANES), const),
                      pl.BlockSpec((1, HEAD_DIM), const)],
            out_specs=pl.BlockSpec((tq, GROUP_WIDTH), row),
            scratch_shapes=[pltpu.VMEM((2 * N_HEADS, tq, LANES), F32),
                            pltpu.VMEM((2 * N_HEADS, tq, LANES), F32)]),
        compiler_params=_params("parallel", "arbitrary"),
        name="diff_attention",
    )(qt, kt, qk, qk, v, _ones_col_row(), lam_pad, norm_g.reshape(1, HEAD_DIM).astype(F32))


def _gdn_kernel(x_ref, z_ref, sm_ref, cw_ref, alog_ref, dtb_ref, ng_ref, gs_ref, ea_ref, eb_ref, tri_ref,
                o_ref, xs_sc, st_sc, *, tt):
    kconv = cw_ref.shape[0]

    @pl.when(pl.program_id(1) == 0)
    def _():
        xs_sc[0:8, :] = jnp.zeros((8, xs_sc.shape[1]), F32)
        st_sc[...] = jnp.zeros(st_sc.shape, F32)

    xs_sc[8:8 + tt, :] = x_ref[...]
    conv = cw_ref[0:1, :] * xs_sc[pl.ds(8 - (kconv - 1), tt), :]
    for j in range(1, kconv):
        conv = conv + cw_ref[j:j + 1, :] * xs_sc[pl.ds(8 - (kconv - 1) + j, tt), :]
    xs_sc[0:8, :] = x_ref[tt - 8:tt, :]
    qkv = jax.nn.silu(conv)

    def l2n(a):
        out = []
        for c in range(GROUP_WIDTH // LANES):
            xc = a[:, c * LANES:(c + 1) * LANES]
            out.append(xc * lax.rsqrt(_dot_c2(xc * xc, gs_ref[...]) + EPS))
        return jnp.concatenate(out, axis=1)

    q = l2n(qkv[:, 0:GROUP_WIDTH]) * HEAD_DIM ** -0.5
    k = l2n(qkv[:, GROUP_WIDTH:2 * GROUP_WIDTH])
    v = qkv[:, 2 * GROUP_WIDTH:]
    sm = sm_ref[...]
    g_all = -jnp.exp(alog_ref[...]) * jax.nn.softplus(sm + dtb_ref[...])
    gx = _dot_c(g_all, ea_ref[...])
    bx = _dot_c2(jax.nn.sigmoid(sm), eb_ref[...])
    ii = lax.broadcasted_iota(jnp.int32, (CHUNK, CHUNK), 0)
    jj = lax.broadcasted_iota(jnp.int32, (CHUNK, CHUNK), 1)
    eye = jnp.where(ii == jj, 1.0, 0.0)
    nchunk = tt // CHUNK
    heads = [slice(h * HEAD_DIM, (h + 1) * HEAD_DIM) for h in range(N_HEADS)]
    pairs, dec, rhs, qd, kend, eglast, qk_nt, kbk_nt = [], [], [], [], [], [], [], []
    for c in range(nchunk):
        r = slice(c * CHUNK, (c + 1) * CHUNK)
        gcx = _c_dot(tri_ref[...], gx[r])
        gct = gcx.T
        egc = jnp.exp(gcx)
        glast = gcx[CHUNK - 1:CHUNK, :]
        kc, qc, bc = k[r].astype(BF16), q[r], bx[r]
        kb = k[r] * bc
        vb = v[r] * bc
        kbe = kb * egc
        qd.append((qc * egc).astype(BF16))
        kend.append((k[r] * jnp.exp(glast - gcx)).astype(BF16))
        eglast.append(jnp.exp(glast))
        qcb, kbb = qc.astype(BF16), kb.astype(BF16)
        for hs in heads:
            pairs.append((c, hs))
            dec.append(jnp.exp(jnp.where(ii >= jj, gcx[:, hs] - gct[hs, :], NEG)))
            rhs.append(jnp.concatenate([vb[:, hs], kbe[:, hs]], axis=1).astype(BF16))
            kbk_nt.append(_dot_nt(kbb[:, hs], kc[:, hs]))
            qk_nt.append(_dot_nt(qcb[:, hs], kc[:, hs]))
    pw = [-jnp.where(ii > jj, a * d, 0.0) for a, d in zip(kbk_nt, dec)]
    inv = [eye + p for p in pw]
    for _ in range(5):
        pw = [_bdot(p, p) for p in pw]
        inv = [x + _bdot(x, p) for x, p in zip(inv, pw)]
    sol = [_bdot(x, b) for x, b in zip(inv, rhs)]
    attn = [(a * d).astype(BF16) for a, d in zip(qk_nt, dec)]
    state = [st_sc[h] for h in range(N_HEADS)]
    outs = []
    for c in range(nchunk):
        idx = [c * N_HEADS + h for h in range(N_HEADS)]
        sb = [s.astype(BF16) for s in state]
        ws = [_dot(sol[i][:, HEAD_DIM:].astype(BF16), sb[h]) for h, i in enumerate(idx)]
        qs = [_dot(qd[c][:, hs], sb[h]) for h, hs in enumerate(heads)]
        v_new = [(sol[i][:, :HEAD_DIM] - ws[h]).astype(BF16) for h, i in enumerate(idx)]
        outs.append(jnp.concatenate([qs[h] + _dot(attn[i], v_new[h]) for h, i in enumerate(idx)], axis=1))
        state = [state[h] * eglast[c][:, hs] + _dot_tn(kend[c][:, hs], v_new[h]) for h, hs in enumerate(heads)]
    for h in range(N_HEADS):
        st_sc[h] = state[h]
    o = jnp.concatenate(outs, axis=0)
    normed = []
    for c in range(GROUP_WIDTH // LANES):
        oc = o[:, c * LANES:(c + 1) * LANES]
        normed.append(oc * lax.rsqrt(_dot_c2(oc * oc, gs_ref[...]) * (1.0 / HEAD_DIM) + EPS))
    o = jnp.concatenate(normed, axis=1) * ng_ref[...]
    o_ref[...] = (o * jax.nn.silu(z_ref[...])).astype(o_ref.dtype)


def _gdn(b_qkv, b_z, small, conv_w, a_log, dt_bias, norm_g, B, S, tt=1024):
    tt = min(tt, S)
    ns = S // tt
    row = lambda b, i: (b * ns + i, 0)
    const = lambda b, i: (0, 0)
    tri = jnp.asarray(np.tril(np.ones((CHUNK, CHUNK), np.float32)), BF16)
    return pl.pallas_call(
        functools.partial(_gdn_kernel, tt=tt),
        out_shape=jax.ShapeDtypeStruct((B * S, GROUP_WIDTH), BF16),
        grid=(B, ns),
        in_specs=[pl.BlockSpec((tt, 3 * GROUP_WIDTH), row),
                  pl.BlockSpec((tt, GROUP_WIDTH), row),
                  pl.BlockSpec((tt, LANES), row),
                  pl.BlockSpec(conv_w.shape, const),
                  pl.BlockSpec((1, LANES), const),
                  pl.BlockSpec((1, LANES), const),
                  pl.BlockSpec((1, GROUP_WIDTH), const),
                  pl.BlockSpec((LANES, LANES), const),
                  pl.BlockSpec((LANES, GROUP_WIDTH), const),
                  pl.BlockSpec((LANES, GROUP_WIDTH), const),
                  pl.BlockSpec((CHUNK, CHUNK), const)],
        out_specs=pl.BlockSpec((tt, GROUP_WIDTH), row),
        scratch_shapes=[pltpu.VMEM((tt + 8, 3 * GROUP_WIDTH), F32),
                        pltpu.VMEM((N_HEADS, HEAD_DIM, HEAD_DIM), F32)],
        compiler_params=_params("parallel", "arbitrary"),
        name="gated_deltanet",
    )(b_qkv, b_z, small, conv_w.astype(F32), _row128(a_log, _LANE_GDN_A), _row128(dt_bias, _LANE_GDN_A),
      jnp.tile(norm_g.astype(F32), N_HEADS).reshape(1, GROUP_WIDTH), _group_matrix(LANES, HEAD_DIM, False),
      _expand_matrix(_LANE_GDN_A, N_HEADS, HEAD_DIM), _expand_matrix(_LANE_GDN_B, N_HEADS, HEAD_DIM), tri)


def _mlstm_kernel(qk_ref, v_ref, op_ref, sm_ref, ib_ref, fb_ref, ng_ref, gm_ref, ei_ref, ef_ref, tri_ref,
                  o_ref, c_sc, m_sc, *, tt):
    @pl.when(pl.program_id(1) == 0)
    def _():
        c_sc[...] = jnp.zeros(c_sc.shape, F32)
        m_sc[...] = jnp.zeros(m_sc.shape, F32)

    sm = sm_ref[...]
    ig = GATE_CAP * jnp.tanh((sm + ib_ref[...]) * (1.0 / GATE_CAP))
    lf = jax.nn.log_sigmoid(GATE_CAP * jnp.tanh((sm + fb_ref[...]) * (1.0 / GATE_CAP)))
    ix = _dot_c(ig, ei_ref[...])
    fx = _dot_c(lf, ef_ref[...])
    nqk = N_HEADS * MLSTM_QK
    q = qk_ref[:, 0:nqk]
    k = qk_ref[:, nqk:2 * nqk] * MLSTM_QK ** -0.5
    v = v_ref[...]
    ii = lax.broadcasted_iota(jnp.int32, (CHUNK, CHUNK), 0)
    jj = lax.broadcasted_iota(jnp.int32, (CHUNK, CHUNK), 1)
    one_col = jnp.where(jj == 0, 1.0, 0.0)
    nchunk = tt // CHUNK
    cb, dlog, blast, mloc, qh, vaug, cloc, qk_nt = [], [], [], [], [], [], [], []
    for c in range(nchunk):
        r = slice(c * CHUNK, (c + 1) * CHUNK)
        bx = _c_dot(tri_ref[...], fx[r])
        rowv_all = bx - ix[r]
        rowv_t = rowv_all.T
        for h in range(N_HEADS):
            hl = slice(h * LANES, (h + 1) * LANES)
            cb.append(bx[:, hl])
            dlog.append(jnp.where(ii >= jj, bx[:, h * LANES:h * LANES + CHUNK] - rowv_t[h * LANES:h * LANES + CHUNK, :],
                                  NEG))
            blast.append(bx[CHUNK - 1:CHUNK, hl])
            aend = blast[-1] - rowv_all[:, hl]
            mloc.append(jnp.max(aend, axis=0, keepdims=True))
            wend = jnp.exp(aend - mloc[-1])
            qh.append(q[r, h * MLSTM_QK:(h + 1) * MLSTM_QK].astype(BF16))
            kh = k[r, h * MLSTM_QK:(h + 1) * MLSTM_QK]
            vaug.append(jnp.concatenate([v[r, h * HEAD_DIM:(h + 1) * HEAD_DIM], one_col], axis=1).astype(BF16))
            cloc.append(_dot_tn((kh * wend[:, :MLSTM_QK]).astype(BF16), vaug[-1]))
            qk_nt.append(_dot_nt(qh[-1], kh.astype(BF16)))
    c_in, m_in = [], []
    c_st = [c_sc[h] for h in range(N_HEADS)]
    m_st = [m_sc[h][0:1, :] for h in range(N_HEADS)]
    for c in range(nchunk):
        for h in range(N_HEADS):
            i = c * N_HEADS + h
            c_in.append(c_st[h])
            m_in.append(m_st[h])
            m_new = jnp.maximum(blast[i] + m_st[h], mloc[i])
            c_st[h] = jnp.exp(blast[i] + m_st[h] - m_new) * c_st[h] + jnp.exp(mloc[i] - m_new) * cloc[i]
            m_st[h] = m_new
    for h in range(N_HEADS):
        c_sc[h] = c_st[h]
        m_sc[h] = jnp.broadcast_to(m_st[h], (8, LANES))
    inter = [b + m for b, m in zip(cb, m_in)]
    mt = [jnp.maximum(x, jnp.max(d, axis=-1, keepdims=True)) for x, d in zip(inter, dlog)]
    wintra = [(jnp.exp(d - m[:, :CHUNK]) * a).astype(BF16) for d, m, a in zip(dlog, mt, qk_nt)]
    numden = [jnp.exp(x - m) * _dot(qq, cc.astype(BF16)) + _dot(w, va)
              for x, m, qq, cc, w, va in zip(inter, mt, qh, c_in, wintra, vaug)]
    hout = [nd[:, :HEAD_DIM] / jnp.maximum(jnp.abs(nd[:, HEAD_DIM:HEAD_DIM + 1]), jnp.exp(-m[:, 0:1]))
            for nd, m in zip(numden, mt)]
    o = jnp.concatenate([jnp.concatenate(hout[c * N_HEADS:(c + 1) * N_HEADS], axis=1) for c in range(nchunk)], axis=0)
    normed = [_group_rms(o[:, c * LANES:(c + 1) * LANES], gm_ref[...]) for c in range(GROUP_WIDTH // LANES)]
    o = jnp.concatenate(normed, axis=1) * ng_ref[...]
    o_ref[...] = (o * jax.nn.sigmoid(op_ref[...])).astype(o_ref.dtype)


def _mlstm(c_qk, c_v, c_o, small, i_bias, f_bias, norm_g, B, S, tt=512):
    tt = min(tt, S)
    ns = S // tt
    row = lambda b, i: (b * ns + i, 0)
    const = lambda b, i: (0, 0)
    tri = jnp.asarray(np.tril(np.ones((CHUNK, CHUNK), np.float32)), BF16)
    return pl.pallas_call(
        functools.partial(_mlstm_kernel, tt=tt),
        out_shape=jax.ShapeDtypeStruct((B * S, GROUP_WIDTH), BF16),
        grid=(B, ns),
        in_specs=[pl.BlockSpec((tt, GROUP_WIDTH), row),
                  pl.BlockSpec((tt, GROUP_WIDTH), row),
                  pl.BlockSpec((tt, GROUP_WIDTH), row),
                  pl.BlockSpec((tt, LANES), row),
                  pl.BlockSpec((1, LANES), const),
                  pl.BlockSpec((1, LANES), const),
                  pl.BlockSpec((1, GROUP_WIDTH), const),
                  pl.BlockSpec((LANES, LANES), const),
                  pl.BlockSpec((LANES, N_HEADS * LANES), const),
                  pl.BlockSpec((LANES, N_HEADS * LANES), const),
                  pl.BlockSpec((CHUNK, CHUNK), const)],
        out_specs=pl.BlockSpec((tt, GROUP_WIDTH), row),
        scratch_shapes=[pltpu.VMEM((N_HEADS, MLSTM_QK, LANES), F32),
                        pltpu.VMEM((N_HEADS, 8, LANES), F32)],
        compiler_params=_params("parallel", "arbitrary"),
        name="mlstm",
    )(c_qk, c_v, c_o, small, _row128(i_bias, _LANE_I), _row128(f_bias, _LANE_F),
      jnp.tile(norm_g.astype(F32), N_HEADS).reshape(1, GROUP_WIDTH), _group_matrix(LANES, HEAD_DIM, True),
      _expand_matrix(_LANE_I, N_HEADS, LANES), _expand_matrix(_LANE_F, N_HEADS, LANES), tri)


def _memkv_kernel(mem_ref, ln_ref, wkv_ref, gk_ref, gm_ref, k_ref, v_ref):
    x = mem_ref[0]
    xn = (x * lax.rsqrt(jnp.mean(x * x, axis=-1, keepdims=True) + EPS) * ln_ref[...]).astype(BF16)
    kv = _dot(xn, wkv_ref[...])
    for c in range(GROUP_WIDTH // LANES):
        cs = slice(c * LANES, (c + 1) * LANES)
        k_ref[0, :, cs] = (_group_rms(kv[:, cs], gm_ref[...]) * gk_ref[:, cs]).astype(BF16)
    v_ref[0] = kv[:, GROUP_WIDTH:].astype(BF16)


def _memkv(mem, ln_mem, wkv, gain_k):
    B, M, _ = mem.shape
    const = lambda b: (0, 0)
    return pl.pallas_call(
        _memkv_kernel,
        out_shape=[jax.ShapeDtypeStruct((B, M, GROUP_WIDTH), BF16)] * 2,
        grid=(B,),
        in_specs=[pl.BlockSpec((1, M, D_MODEL), lambda b: (b, 0, 0)),
                  pl.BlockSpec((1, D_MODEL), const),
                  pl.BlockSpec((D_MODEL, 2 * GROUP_WIDTH), const),
                  pl.BlockSpec((1, GROUP_WIDTH), const),
                  pl.BlockSpec((LANES, LANES), const)],
        out_specs=[pl.BlockSpec((1, M, GROUP_WIDTH), lambda b: (b, 0, 0))] * 2,
        compiler_params=_params("parallel"),
        name="memory_kv",
    )(mem, ln_mem.reshape(1, D_MODEL), wkv.astype(BF16),
      jnp.tile(gain_k.astype(F32), N_HEADS).reshape(1, GROUP_WIDTH), _group_matrix(LANES, HEAD_DIM, True))


def _out_xattn_kernel(ya_ref, yb_ref, yc_ref, yd_ref, h_ref, wout_ref, lnx_ref, wq_ref, kx_ref, vx_ref, gq_ref, gm_ref,
                      wo_ref, o_ref):
    y = jnp.concatenate([ya_ref[...], yb_ref[...], yc_ref[...], yd_ref[...]], axis=1)
    h1 = h_ref[...] + _dot(y, wout_ref[...])
    hn = (h1 * lax.rsqrt(jnp.mean(h1 * h1, axis=-1, keepdims=True) + EPS) * lnx_ref[...]).astype(BF16)
    q = _dot(hn, wq_ref[...])
    chunks = []
    for c in range(GROUP_WIDTH // LANES):
        cs = slice(c * LANES, (c + 1) * LANES)
        chunks.append((_group_rms(q[:, cs], gm_ref[...]) * gq_ref[:, cs]).astype(BF16))
    qn = jnp.concatenate(chunks, axis=1)
    kx, vx = kx_ref[0], vx_ref[0]
    outs = []
    for h in range(N_HEADS):
        hs = slice(h * HEAD_DIM, (h + 1) * HEAD_DIM)
        s = _dot_nt(qn[:, hs], kx[:, hs]) * HEAD_DIM ** -0.5
        e = jnp.exp(s - jnp.max(s, axis=-1, keepdims=True))
        p = e * (1.0 / jnp.sum(e, axis=-1, keepdims=True))
        outs.append(_dot(p.astype(BF16), vx[:, hs]))
    o = jnp.concatenate(outs, axis=1).astype(BF16)
    o_ref[...] = h1 + _dot(o, wo_ref[...])


def _out_xattn(ys, h2d, w_out, ln_x, wq, kx, vx, gain_q, wo, B, S, tm=512):
    tm = min(tm, S)
    ns = S // tm
    M = kx.shape[1]
    row = lambda b, i: (b * ns + i, 0)
    const = lambda b, i: (0, 0)
    return pl.pallas_call(
        _out_xattn_kernel,
        out_shape=jax.ShapeDtypeStruct((B * S, D_MODEL), F32),
        grid=(B, ns),
        in_specs=[pl.BlockSpec((tm, GROUP_WIDTH), row)] * 4 + [
            pl.BlockSpec((tm, D_MODEL), row),
            pl.BlockSpec((D_MODEL, D_MODEL), const),
            pl.BlockSpec((1, D_MODEL), const),
            pl.BlockSpec((D_MODEL, GROUP_WIDTH), const),
            pl.BlockSpec((1, M, GROUP_WIDTH), lambda b, i: (b, 0, 0)),
            pl.BlockSpec((1, M, GROUP_WIDTH), lambda b, i: (b, 0, 0)),
            pl.BlockSpec((1, GROUP_WIDTH), const),
            pl.BlockSpec((LANES, LANES), const),
            pl.BlockSpec((GROUP_WIDTH, D_MODEL), const)],
        out_specs=pl.BlockSpec((tm, D_MODEL), row),
        compiler_params=_params("parallel", "parallel"),
        name="outproj_xattn",
    )(*ys, h2d, w_out.astype(BF16), ln_x.reshape(1, D_MODEL), wq.astype(BF16), kx, vx,
      jnp.tile(gain_q.astype(F32), N_HEADS).reshape(1, GROUP_WIDTH), _group_matrix(LANES, HEAD_DIM, True),
      wo.astype(BF16))


def _moe_route_kernel(h_ref, ln_ref, wrh_ref, wrl_ref, br_ref, tri_ref, hn_ref, comb_ref, rkt_ref, cnt_ref):
    tm = h_ref.shape[0]
    lane = lax.broadcasted_iota(jnp.int32, (tm, LANES), 1)
    x = h_ref[...]
    hn = x * lax.rsqrt(jnp.mean(x * x, axis=-1, keepdims=True) + EPS) * ln_ref[...]
    hn_hi = hn.astype(BF16)
    hn_ref[...] = hn_hi
    hn_lo = (hn - hn_hi.astype(F32)).astype(BF16)
    logits = _dot(hn_hi, wrh_ref[...]) + _dot(hn_hi, wrl_ref[...]) + _dot(hn_lo, wrh_ref[...]) + br_ref[...]
    lanef = lane.astype(F32)
    big = 1e4
    isg = (lane >= MOE_EXPERTS) & (lane < MOE_EXPERTS + MOE_GROUPS)
    lg = jnp.where(isg, logits, NEG)
    gmax = jnp.max(lg, axis=-1, keepdims=True)
    grp_p = 1.0 / jnp.sum(jnp.exp(lg - gmax), axis=-1, keepdims=True)
    gidx = jnp.min(jnp.where(lg == gmax, lanef, big), axis=-1, keepdims=True) - MOE_EXPERTS
    ing = (lane < MOE_EXPERTS) & ((lane >> 3).astype(F32) == gidx)
    le = jnp.where(ing, logits, NEG)
    m1 = jnp.max(le, axis=-1, keepdims=True)
    z = jnp.sum(jnp.where(ing, jnp.exp(le - m1), 0.0), axis=-1, keepdims=True)
    i1 = jnp.min(jnp.where(le == m1, lanef, big), axis=-1, keepdims=True)
    oh1 = lanef == i1
    le2 = jnp.where(oh1, NEG, le)
    m2 = jnp.max(le2, axis=-1, keepdims=True)
    i2 = jnp.min(jnp.where((le2 == m2) & ing, jnp.where(oh1, big, lanef), big), axis=-1, keepdims=True)
    oh2 = lanef == i2
    p1 = 1.0 / z
    p2 = jnp.exp(m2 - m1) / z
    tot = p1 + p2
    comb_ref[...] = jnp.where(oh1, p1 / tot * grp_p, 0.0) + jnp.where(oh2, p2 / tot * grp_p, 0.0)
    member = jnp.where(lanef == gidx, 1.0, 0.0)
    rank = _dot(tri_ref[...], member.astype(BF16))
    rkt_ref[0] = jnp.where(member > 0.5, rank, -1.0).T[0:8, :]
    cnt_ref[0] = jnp.broadcast_to(jnp.sum(member, axis=0, keepdims=True), (8, LANES))


def _moe_group_kernel(cnt_ref, hn_ref, comb_ref, rkt_ref, prev_ref, w1_ref, w3_ref, w2_ref, o_ref, acc_sc,
                      *, group, tm, ch):
    cnt = cnt_ref[pl.program_id(0) * MOE_GROUPS + group]
    acc_sc[...] = jnp.zeros(acc_sc.shape, F32)
    rkg = rkt_ref[0][group:group + 1, :]
    starts = [(0, ch)] + [(s, ch // 2) for s in range(ch, tm, ch // 2)]
    for start, ch in starts:
        @pl.when(start < cnt)
        def _():
            rows = (start + lax.broadcasted_iota(jnp.int32, (ch, 1), 0)).astype(F32)
            sel = jnp.where(rkg == rows, 1.0, 0.0).astype(BF16)
            xg = _dot(sel, hn_ref[...]).astype(BF16)
            cg = _c_dot2(sel, comb_ref[...])
            yg = jnp.zeros((ch, D_MODEL), F32)
            for e in range(MOE_PER_GROUP):
                hg = _dot(xg, w1_ref[e])
                hu = _dot(xg, w3_ref[e])
                ce = cg[:, group * MOE_PER_GROUP + e:group * MOE_PER_GROUP + e + 1]
                yg = yg + _dot((jax.nn.silu(hg) * hu * ce).astype(BF16), w2_ref[e])
            acc_sc[...] += _dot_tn(sel, yg.astype(BF16))
    o_ref[...] = prev_ref[...] + acc_sc[...]


def _moe(h2d, ln, w_group, b_group, w_expert, b_expert, w1, w3, w2, tm=1024, ch=256):
    T = h2d.shape[0]
    tm = min(tm, T)
    nt = T // tm
    pad = LANES - MOE_EXPERTS - MOE_GROUPS
    wr = jnp.concatenate([w_expert, w_group, jnp.zeros((D_MODEL, pad), F32)], axis=1)
    wr_hi = wr.astype(BF16)
    br =jnp.concatenate([b_expert, b_group, jnp.zeros((pad,), F32)]).reshape(1, LANES)
    tri = jnp.asarray(np.tril(np.ones((tm, tm), np.float32), -1), BF16)
    const = lambda i: (0, 0)
    hn, comb, rkt, cnt = pl.pallas_call(
        _moe_route_kernel,
        out_shape=[jax.ShapeDtypeStruct((T, D_MODEL), BF16),
                   jax.ShapeDtypeStruct((T, LANES), F32),
                   jax.ShapeDtypeStruct((nt, 8, tm), F32),
                   jax.ShapeDtypeStruct((nt, 8, LANES), F32)],
        grid=(nt,),
        in_specs=[pl.BlockSpec((tm, D_MODEL), lambda i: (i, 0)),
                  pl.BlockSpec((1, D_MODEL), const),
                  pl.BlockSpec((D_MODEL, LANES), const),
                  pl.BlockSpec((D_MODEL, LANES), const),
                  pl.BlockSpec((1, LANES), const),
                  pl.BlockSpec((tm, tm), const)],
        out_specs=[pl.BlockSpec((tm, D_MODEL), lambda i: (i, 0)),
                   pl.BlockSpec((tm, LANES), lambda i: (i, 0)),
                   pl.BlockSpec((1, 8, tm), lambda i: (i, 0, 0)),
                   pl.BlockSpec((1, 8, LANES), lambda i: (i, 0, 0))],
        compiler_params=_params("parallel"),
        name="moe_route",
    )(h2d, ln.reshape(1, D_MODEL), wr_hi, (wr - wr_hi.astype(F32)).astype(BF16), br, tri)
    counts = cnt[:, 0, :MOE_GROUPS].astype(jnp.int32).reshape(nt * MOE_GROUPS)
    w1b, w3b, w2b = w1.astype(BF16), w3.astype(BF16), w2.astype(BF16)
    out = h2d
    for g in range(MOE_GROUPS):
        wspec = lambda shape: pl.BlockSpec((MOE_PER_GROUP,) + shape, lambda i, c, g=g: (g, 0, 0),
                                           pipeline_mode=pl.Buffered(1))
        out = pl.pallas_call(
            functools.partial(_moe_group_kernel, group=g, tm=tm, ch=ch),
            out_shape=jax.ShapeDtypeStruct((T, D_MODEL), F32),
            grid_spec=pltpu.PrefetchScalarGridSpec(
                num_scalar_prefetch=1,
                grid=(nt,),
                in_specs=[pl.BlockSpec((tm, D_MODEL), lambda i, c: (i, 0)),
                          pl.BlockSpec((tm, LANES), lambda i, c: (i, 0)),
                          pl.BlockSpec((1, 8, tm), lambda i, c: (i, 0, 0)),
                          pl.BlockSpec((tm, D_MODEL), lambda i, c: (i, 0)),
                          wspec((D_MODEL, MOE_FF)), wspec((D_MODEL, MOE_FF)), wspec((MOE_FF, D_MODEL))],
                out_specs=pl.BlockSpec((tm, D_MODEL), lambda i, c: (i, 0)),
                scratch_shapes=[pltpu.VMEM((tm, D_MODEL), F32)]),
            compiler_params=_params("parallel"),
            name="moe_group",
        )(counts, hn, comb, rkt, out, w1b, w3b, w2b)
    return out


def _nsa_mixer(a_q, a_kv, kn, small, qk_gain, cmp_pe, cmp_w1, cmp_w2, cos, sin, B, S):
    nc = S // NSA_CMP_STRIDE
    wide = NSA_CMP_STRIDE * HEAD_DIM
    cos_c, sin_c = _rope_tables(NSA_CMP_STRIDE * np.arange(nc) + NSA_CMP_LEN - 1, HEAD_DIM)
    half = HEAD_DIM // 2
    kc, vc = _nsa_compress(a_kv, cmp_pe.reshape(2, 2 * wide), cmp_w1, cmp_w2, qk_gain[1],
                           cos_c[:, :half], sin_c[:, half:2 * half], B, nc)
    gq_row = jnp.tile(qk_gain[0].astype(F32), N_HEADS).reshape(1, GROUP_WIDTH)
    qn, part, sel = _nsa1(a_q, small, kc, vc, kn, a_kv, cos, sin, gq_row, B, S)
    return _nsa2(qn, sel, kn, a_kv, part, small, B, S)


def kernel(x, mem, ln_mix, w_in, w_out, nsa_qk_gain, nsa_cmp_pe, nsa_cmp_w1, nsa_cmp_w2, gdn_conv, gdn_a_log, gdn_dt_bias, gdn_norm, mlstm_i_bias, mlstm_f_bias, mlstm_norm, diff_qk_gain, diff_lambda, diff_norm, ln_xattn, ln_mem, xattn_wq, xattn_wkv, xattn_qk_gain, xattn_wo, ln_moe, moe_w_group, moe_b_group, moe_w_expert, moe_b_expert, moe_w1, moe_w3, moe_w2):
    B, S, D = x.shape
    depth = w_in.shape[0]
    cos_a, sin_a = _rope_tables(np.arange(S), HEAD_DIM)
    cos_d, sin_d = _rope_tables(np.arange(S), DIFF_SUB)
    h = x.reshape(B * S, D)
    for l in range(depth):
        gain_k = jnp.concatenate([nsa_qk_gain[l, 2], nsa_qk_gain[l, 3]]).reshape(1, LANES).astype(F32)
        gain_d = jnp.concatenate([jnp.tile(diff_qk_gain[l, 0], 2 * N_HEADS) * DIFF_SUB ** -0.5,
                                  jnp.tile(diff_qk_gain[l, 1], 2 * N_HEADS)]).reshape(1, 2 * GROUP_WIDTH).astype(F32)
        a_q, a_kv, b_qkv, b_z, c_qk, c_v, c_o, d_v, small, dqk, kn = _inproj(
            h, ln_mix[l], w_in[l], gain_k, gain_d, cos_a, sin_a, cos_d, sin_d, S)
        y_a = _nsa_mixer(a_q, a_kv, kn, small, nsa_qk_gain[l], nsa_cmp_pe[l], nsa_cmp_w1[l], nsa_cmp_w2[l],
                         cos_a, sin_a, B, S)
        y_b = _gdn(b_qkv, b_z, small, gdn_conv[l], gdn_a_log[l], gdn_dt_bias[l], gdn_norm[l], B, S)
        y_c = _mlstm(c_qk, c_v, c_o, small, mlstm_i_bias[l], mlstm_f_bias[l], mlstm_norm[l], B, S)
        lambda_init = 0.8 - 0.6 * math.exp(-0.3 * l)
        y_d = _diff_attention(dqk, d_v, diff_lambda[l], diff_norm[l], lambda_init, B, S)
        kx, vx = _memkv(mem, ln_mem[l], xattn_wkv[l], xattn_qk_gain[l, 1])
        h = _out_xattn((y_a, y_b, y_c, y_d), h, w_out[l], ln_xattn[l], xattn_wq[l], kx, vx,
                       xattn_qk_gain[l, 0], xattn_wo[l], B, S)
        h = _moe(h, ln_moe[l], moe_w_group[l], moe_b_group[l], moe_w_expert[l], moe_b_expert[l],
                 moe_w1[l], moe_w3[l], moe_w2[l])
    return h.reshape(B, S, D)
```

```python
import functools
import math

import numpy as np
import jax
import jax.numpy as jnp
from jax import lax
from jax.experimental import pallas as pl
from jax.experimental.pallas import tpu as pltpu

F32 = jnp.float32
BF16 = jnp.bfloat16
HI = lax.Precision.HIGHEST

D_MODEL = 1024
HEAD_DIM = 64
N_HEADS = 4
GROUP_WIDTH = 256
ROPE_THETA = 10000.0
EPS = 1e-6
NEG = -1e30

NSA_CMP_LEN = 32
NSA_CMP_STRIDE = 16
NSA_SEL_LEN = 64
NSA_TOP_N = 16
NSA_WINDOW = 512
NSA_FORCE_BONUS = 1e3
CHUNK = 64
MLSTM_QK = 32
GATE_CAP = 15.0
DIFF_SUB = 32
MOE_GROUPS = 4
MOE_PER_GROUP = 8
MOE_EXPERTS = 32
MOE_FF = 256
LANES = 128
VMEM_LIMIT = 48 * 1024 * 1024

_A0, _B0, _C0, _D0 = 0, 652, 1684, 2460
_GROUPS = (
    ("a_q", ((_A0, 256),), F32),
    ("a_kv", ((_A0 + 384, 64), (_A0 + 512, 64), (_A0 + 448, 64), (None, 64), (_A0 + 576, 64), (None, 64),
              (_A0 + 256, 64), (_A0 + 320, 64)), F32),
    ("b_qkv", ((_B0, 768),), F32),
    ("b_z", ((_B0 + 776, 256),), F32),
    ("c_qk", ((_C0, 256),), F32),
    ("c_v", ((_C0 + 256, 256),), F32),
    ("c_o", ((_C0 + 520, 256),), F32),
    ("d_qk", ((_D0, 512),), F32),
    ("d_v", tuple(p for h in range(N_HEADS) for p in ((_D0 + 512 + 64 * h, 64), (None, 64))), BF16),
    ("small", ((_A0 + 640, 12), (_B0 + 768, 4), (_B0 + 772, 4), (_C0 + 512, 4), (_C0 + 516, 4), (None, 100)), F32),
)
_LANE_GDN_A, _LANE_GDN_B, _LANE_I, _LANE_F = 12, 16, 20, 24


def _dot(a, b, prec=None):
    return jnp.dot(a, b, preferred_element_type=F32, precision=prec)


def _dot_nt(a, b, prec=None):
    return lax.dot_general(a, b, (((1,), (1,)), ((), ())), preferred_element_type=F32, precision=prec)


def _dot_tn(a, b, prec=None):
    return lax.dot_general(a, b, (((0,), (0,)), ((), ())), preferred_element_type=F32, precision=prec)


def _bdot(a, b):
    return _dot(a.astype(BF16), b.astype(BF16))


def _bdot_nt(a, b):
    return _dot_nt(a.astype(BF16), b.astype(BF16))


def _bdot_tn(a, b):
    return _dot_tn(a.astype(BF16), b.astype(BF16))


def _split3(a):
    hi = a.astype(BF16)
    r = a - hi.astype(F32)
    mid = r.astype(BF16)
    return hi, mid, (r - mid.astype(F32)).astype(BF16)


def _dot_c(a, c):
    hi, mid, lo = _split3(a)
    return _dot(hi, c) + _dot(mid, c) + _dot(lo, c)


def _c_dot(c, b):
    hi, mid, lo = _split3(b)
    return _dot(c, hi) + _dot(c, mid) + _dot(c, lo)


def _dot_c2(a, c):
    hi = a.astype(BF16)
    return _dot(hi, c) + _dot((a - hi.astype(F32)).astype(BF16), c)


def _c_dot2(c, b):
    hi = b.astype(BF16)
    return _dot(c, hi) + _dot(c, (b - hi.astype(F32)).astype(BF16))


def _dot_nt_x3(a, b):
    ah, bh = a.astype(BF16), b.astype(BF16)
    al, bl = (a - ah.astype(F32)).astype(BF16), (b - bh.astype(F32)).astype(BF16)
    return _dot_nt(ah, bh) + _dot_nt(ah, bl) + _dot_nt(al, bh)


def _params(*sem):
    return pltpu.CompilerParams(dimension_semantics=sem, vmem_limit_bytes=VMEM_LIMIT)


def _group_matrix(width, gsz, mean):
    g = np.kron(np.eye(width // gsz), np.ones((gsz, gsz)))
    return jnp.asarray(g / gsz if mean else g, BF16)


def _expand_matrix(src_lane0, n, out_per, stride=1):
    e = np.zeros((LANES, n * out_per), np.float32)
    for h in range(n):
        e[src_lane0 + stride * h, h * out_per:(h + 1) * out_per] = 1.0
    return jnp.asarray(e, BF16)


def _row128(vals, lane0):
    return jnp.zeros((1, LANES), F32).at[0, lane0:lane0 + vals.shape[0]].set(vals.astype(F32))


def _ones_col_row():
    return jnp.zeros((1, LANES), BF16).at[0, HEAD_DIM].set(1.0)


def _rope_tables(pos, dim):
    inv = 1.0 / (ROPE_THETA ** (jnp.arange(0, dim, 2, dtype=F32) / dim))
    ang = jnp.asarray(pos).astype(F32)[:, None] * inv[None, :]
    cos, sin = jnp.cos(ang), jnp.sin(ang)
    cosd = jnp.concatenate([cos, cos], axis=-1)
    sind = jnp.concatenate([-sin, sin], axis=-1)
    rep = LANES // dim
    return jnp.tile(cosd, (1, rep)), jnp.tile(sind, (1, rep))


def _rope128(x, cos, sin_signed, half):
    left = pltpu.roll(x, LANES - half, 1)
    right = pltpu.roll(x, half, 1)
    lane = lax.broadcasted_iota(jnp.int32, x.shape, 1)
    first = (lane & (2 * half - 1)) < half
    return x * cos + jnp.where(first, left, right) * sin_signed


def _rep_lanes(x, width):
    return x if width == LANES else jnp.concatenate([x] * (width // LANES), axis=1)


def _softmax_rows(s, mask):
    m = jnp.max(s, axis=-1, keepdims=True)
    e = jnp.exp(s - m)
    return jnp.where(mask, e * (1.0 / jnp.sum(e, axis=-1, keepdims=True)), 0.0)


def _group_rms(x, gm):
    return x * lax.rsqrt(_dot_c2(x * x, gm) + EPS)


def _flash_update(items, m_ref, acc_ref, tk):
    items = [it if len(it) == 4 else it + (slice(None),) for it in items]
    m_prev = [m_ref[i, r] for _, _, i, r in items]
    m_new = [jnp.maximum(mp, jnp.max(s, axis=-1, keepdims=True)) for mp, (s, _, _, _) in zip(m_prev, items)]
    p = [jnp.exp((s - _rep_lanes(mn, s.shape[1])).astype(BF16)) for mn, (s, _, _, _) in zip(m_new, items)]
    pv = [_dot(pp, va) for pp, (_, va, _, _) in zip(p, items)]
    for mp, mn, x, (_, _, i, r) in zip(m_prev, m_new, pv, items):
        acc_ref[i, r] = jnp.exp(mp - mn) * acc_ref[i, r] + x
        m_ref[i, r] = mn


def _flash_result(acc_ref, idx):
    acc = acc_ref[idx]
    return acc[:, :HEAD_DIM] * (1.0 / acc[:, HEAD_DIM:HEAD_DIM + 1])


def _normrope(y, gain, gm, cos, sin, half):
    out = []
    for c in range(y.shape[1] // LANES):
        cs = slice(c * LANES, (c + 1) * LANES)
        out.append(_rope128(_group_rms(y[:, cs], gm) * gain[:, cs], cos, sin, half))
    return out[0] if len(out) == 1 else jnp.concatenate(out, axis=1)


def _inproj_kernel(x_ref, g_ref, w_ref, gk_ref, gd_ref, gm64_ref, gm32_ref, cosa_ref, sina_ref, cosd_ref, sind_ref,
                   *outs, widths):
    x = x_ref[...]
    ms = jnp.mean(x * x, axis=-1, keepdims=True)
    xn = (x * lax.rsqrt(ms + EPS) * g_ref[...]).astype(BF16)
    outs = list(outs)
    kn_ref, dqk_ref = outs.pop(), outs.pop()
    off = 0
    for (name, _, _), wd in zip(_GROUPS, widths):
        y = _dot_nt(xn, w_ref[off:off + wd, :])
        off += wd
        if name == "d_qk":
            dqk_ref[...] = _normrope(y, gd_ref[...], gm32_ref[...], cosd_ref[...], sind_ref[...],
                                     DIFF_SUB // 2).astype(BF16)
            continue
        o = outs.pop(0)
        o[...] = y.astype(o.dtype)
        if name == "a_kv":
            kn_ref[...] = _normrope(y[:, :LANES], gk_ref[...], gm64_ref[...], cosa_ref[...], sina_ref[...],
                                    HEAD_DIM // 2).astype(BF16)


def _permute_w_in(w):
    wt = w.T
    rows, widths = [], []
    for _, parts, _ in _GROUPS:
        for s, n in parts:
            rows.append(jnp.zeros((n, w.shape[0]), w.dtype) if s is None else wt[s:s + n])
        widths.append(sum(n for _, n in parts))
    return jnp.concatenate(rows, axis=0).astype(BF16), tuple(widths)


def _inproj(h2d, gain, w_in, gain_k, gain_d, cos_a, sin_a, cos_d, sin_d, S, tm=256):
    T = h2d.shape[0]
    tm = min(tm, S)
    ns = S // tm
    wp, widths = _permute_w_in(w_in)
    kept = [(wd, g) for wd, g in zip(widths, _GROUPS) if g[0] != "d_qk"]
    out_shape = [jax.ShapeDtypeStruct((T, wd), g[2]) for wd, g in kept]
    out_shape += [jax.ShapeDtypeStruct((T, 2 * GROUP_WIDTH), BF16), jax.ShapeDtypeStruct((T, LANES), BF16)]
    const = lambda i: (0, 0)
    table = pl.BlockSpec((tm, LANES), lambda i: (i % ns, 0))
    return pl.pallas_call(
        functools.partial(_inproj_kernel, widths=widths),
        out_shape=out_shape,
        grid=(T // tm,),
        in_specs=[pl.BlockSpec((tm, D_MODEL), lambda i: (i, 0)),
                  pl.BlockSpec((1, D_MODEL), const),
                  pl.BlockSpec((sum(widths), D_MODEL), const),
                  pl.BlockSpec((1, LANES), const),
                  pl.BlockSpec((1, 2 * GROUP_WIDTH), const),
                  pl.BlockSpec((LANES, LANES), const),
                  pl.BlockSpec((LANES, LANES), const),
                  table, table, table, table],
        out_specs=[pl.BlockSpec((tm, s.shape[1]), lambda i: (i, 0)) for s in out_shape],
        compiler_params=_params("parallel"),
        name="inproj",
    )(h2d, gain.reshape(1, D_MODEL), wp, gain_k, gain_d, _group_matrix(LANES, HEAD_DIM, True),
      _group_matrix(LANES, DIFF_SUB, True), cos_a, sin_a, cos_d, sin_d)


def _nsa_cmp_kernel(kv_ref, pe_ref, w1_ref, w2_ref, gain_ref, cos_ref, sin_ref, kc_ref, vc_ref, *, nc):
    half_in = NSA_CMP_STRIDE * HEAD_DIM

    a = [jnp.zeros((nc, 2 * HEAD_DIM), F32) for _ in range(2)]
    b = [jnp.zeros((nc, 2 * HEAD_DIM), F32) for _ in range(2)]
    for l in range(NSA_CMP_STRIDE):
        xl = kv_ref[pl.ds(l, nc, stride=NSA_CMP_STRIDE), :]
        for j in range(2):
            xj = xl[:, j * HEAD_DIM:(j + 1) * HEAD_DIM]
            a[j] = a[j] + _dot(xj, w1_ref[j, l * HEAD_DIM:(l + 1) * HEAD_DIM, :], HI)
            b[j] = b[j] + _dot(xj, w1_ref[j, half_in + l * HEAD_DIM:half_in + (l + 1) * HEAD_DIM, :], HI)

    def finish(j):
        pe = jnp.broadcast_to(pe_ref[j], (8, 2 * half_in))
        c = _dot(pe, w1_ref[j], HI)[0:1]
        hid = jax.nn.gelu(a[j] + pltpu.roll(b[j], nc - 1, 0) + c)
        return _dot(hid, w2_ref[j], HI)

    kc = finish(0)
    kc = kc * lax.rsqrt(jnp.mean(kc * kc, axis=-1, keepdims=True) + EPS) * gain_ref[...]
    x1, x2 = kc[:, :HEAD_DIM // 2], kc[:, HEAD_DIM // 2:]
    cos, sin = cos_ref[...], sin_ref[...]
    kc_ref[0] = jnp.concatenate([x1 * cos - x2 * sin, x2 * cos + x1 * sin], axis=1)
    vc_ref[0] = finish(1)


def _nsa_compress(a_kv, pe, w1, w2, gain, cos_c, sin_c, B, nc):
    wide = NSA_CMP_STRIDE * HEAD_DIM
    return pl.pallas_call(
        functools.partial(_nsa_cmp_kernel, nc=nc),
        out_shape=[jax.ShapeDtypeStruct((B, nc, HEAD_DIM), F32)] * 2,
        grid=(B,),
        in_specs=[pl.BlockSpec((nc * NSA_CMP_STRIDE, LANES), lambda b: (b, 3)),
                  pl.BlockSpec((2, 1, 2 * wide), lambda b: (0, 0, 0)),
                  pl.BlockSpec((2, 2 * wide, 2 * HEAD_DIM), lambda b: (0, 0, 0)),
                  pl.BlockSpec((2, 2 * HEAD_DIM, HEAD_DIM), lambda b: (0, 0, 0)),
                  pl.BlockSpec((1, HEAD_DIM), lambda b: (0, 0)),
                  pl.BlockSpec((nc, HEAD_DIM // 2), lambda b: (0, 0)),
                  pl.BlockSpec((nc, HEAD_DIM // 2), lambda b: (0, 0))],
        out_specs=[pl.BlockSpec((1, nc, HEAD_DIM), lambda b: (b, 0, 0))] * 2,
        compiler_params=_params("parallel"),
        name="nsa_compress",
    )(a_kv, pe.reshape(2, 1, 2 * wide), w1, w2, gain.reshape(1, HEAD_DIM), cos_c, sin_c)


def _nsa1_kernel(q_ref, sm_ref, kc_ref, vc_ref, kn_ref, vw_ref, cos_ref, sin_ref, gq_ref, gm_ref, ovl_ref,
                 eg0_ref, eg2_ref, qn_ref, part_ref, sel_ref, *, tq, nc, n_sel):
    t0 = pl.program_id(1) * tq
    chunks = []
    for c in range(GROUP_WIDTH // LANES):
        cs = slice(c * LANES, (c + 1) * LANES)
        xn = _group_rms(q_ref[:, cs], gm_ref[...]) * gq_ref[:, cs]
        chunks.append(_rope128(xn, cos_ref[...], sin_ref[...], HEAD_DIM // 2))
    qs = jnp.concatenate(chunks, axis=1) * HEAD_DIM ** -0.5
    qb = qs.astype(BF16)
    qn_ref[...] = qb
    sig = jax.nn.sigmoid(sm_ref[...])
    g0x = _dot_c2(sig, eg0_ref[...])
    g2x = _dot_c2(sig, eg2_ref[...])
    tpos = t0 + lax.broadcasted_iota(jnp.int32, (tq, 1), 0)

    kc = kc_ref[0]
    vc = vc_ref[0].astype(BF16)
    cidx = lax.broadcasted_iota(jnp.int32, (1, nc), 1)
    cmask = ((NSA_CMP_STRIDE * cidx + NSA_CMP_LEN - 1) <= tpos) & (cidx < nc - 1)
    psum = jnp.zeros((tq, nc), F32)
    o_cmp = []
    for h in range(N_HEADS):
        hs = slice(h * HEAD_DIM, (h + 1) * HEAD_DIM)
        p = _softmax_rows(jnp.where(cmask, _dot_nt_x3(qs[:, hs], kc), NEG), cmask)
        o_cmp.append(_dot(p.astype(BF16), vc))
        psum = psum + p

    imp = _dot_c(psum, ovl_ref[...])
    j = lax.broadcasted_iota(jnp.int32, (tq, LANES), 1)
    cur = tpos >> 6
    valid = j <= cur
    forced = (j == 0) | (j == cur) | (j == cur - 1)
    score = jnp.where(valid, imp + jnp.where(forced, NSA_FORCE_BONUS, 0.0), NEG)
    nrow = -(-n_sel // 8) * 8
    st = score.T[0:nrow, :]
    blocks = [st[8 * g:8 * g + 8, :] for g in range(nrow // 8)]
    cnts = [jnp.zeros((8, tq), F32) for _ in blocks]
    j8 = lax.broadcasted_iota(jnp.int32, (8, 1), 0)
    for i in range(n_sel):
        row = st[i:i + 1, :]
        for g, blk in enumerate(blocks):
            if 8 * g > i:
                won = jnp.where(row >= blk, 1.0, 0.0)
            elif 8 * g + 7 < i:
                won = jnp.where(row > blk, 1.0, 0.0)
            else:
                tie = jnp.where(j8 + 8 * g > i, 1.0, 0.0)
                won = jnp.where(row > blk, 1.0, jnp.where(row == blk, tie, 0.0))
            cnts[g] = cnts[g] + won
    cnt = jnp.concatenate(cnts, axis=0)
    sel_t = jnp.where((cnt < min(NSA_TOP_N, n_sel)) & (st > 0.5 * NEG), 1.0, 0.0)
    if nrow < LANES:
        sel_t = jnp.concatenate([sel_t, jnp.zeros((LANES - nrow, tq), F32)], axis=0)
    sel_ref[...] = sel_t.T.astype(BF16)

    band = tq + NSA_WINDOW
    start = pl.multiple_of(jnp.maximum(t0 - NSA_WINDOW, 0), LANES)
    kw = kn_ref[pl.ds(start, band), HEAD_DIM:2 * HEAD_DIM]
    one_row = jnp.where(lax.broadcasted_iota(jnp.int32, (1, LANES), 1) == HEAD_DIM, 1.0, 0.0).astype(BF16)
    vaug = vw_ref[pl.ds(start, band), :].astype(BF16) + one_row
    dist = tpos - (start + lax.broadcasted_iota(jnp.int32, (1, band), 1))
    wmask = (dist >= 0) & (dist < NSA_WINDOW)
    s_win = [jnp.where(wmask, _dot_nt(qb[:, h * HEAD_DIM:(h + 1) * HEAD_DIM], kw), NEG) for h in range(N_HEADS)]
    p_win = [jnp.exp((s - jnp.max(s, axis=-1, keepdims=True)).astype(BF16)) for s in s_win]
    pv = [_dot(p, vaug) for p in p_win]
    o_win = [x[:, :HEAD_DIM] * (1.0 / x[:, HEAD_DIM:HEAD_DIM + 1]) for x in pv]
    part_ref[...] = g0x * jnp.concatenate(o_cmp, axis=1) + g2x * jnp.concatenate(o_win, axis=1)


def _nsa_overlap(nc, n_sel):
    c0 = NSA_CMP_STRIDE * np.arange(nc)[:, None]
    s0 = NSA_SEL_LEN * np.arange(n_sel)[None, :]
    ov = np.clip(np.minimum(c0 + NSA_CMP_LEN, s0 + NSA_SEL_LEN) - np.maximum(c0, s0), 0, None) / NSA_CMP_STRIDE
    ov[nc - 1:] = 0.0
    out = np.zeros((nc, LANES), np.float32)
    out[:, :n_sel] = ov
    return jnp.asarray(out, BF16)


def _nsa1(a_q, small, kc, vc, kn, a_kv, cos, sin, gq_row, B, S, tq=256):
    tq = min(tq, S)
    nq = S // tq
    nc = S // NSA_CMP_STRIDE
    n_sel = S // NSA_SEL_LEN
    row = lambda b, i: (b * nq + i, 0)
    return pl.pallas_call(
        functools.partial(_nsa1_kernel, tq=tq, nc=nc, n_sel=n_sel),
        out_shape=[jax.ShapeDtypeStruct((B * S, GROUP_WIDTH), BF16),
                   jax.ShapeDtypeStruct((B * S, GROUP_WIDTH), F32),
                   jax.ShapeDtypeStruct((B * S, LANES), BF16)],
        grid=(B, nq),
        in_specs=[pl.BlockSpec((tq, GROUP_WIDTH), row),
                  pl.BlockSpec((tq, LANES), row),
                  pl.BlockSpec((1, nc, HEAD_DIM), lambda b, i: (b, 0, 0)),
                  pl.BlockSpec((1, nc, HEAD_DIM), lambda b, i: (b, 0, 0)),
                  pl.BlockSpec((S, LANES), lambda b, i: (b, 0)),
                  pl.BlockSpec((S, LANES), lambda b, i: (b, 2)),
                  pl.BlockSpec((tq, LANES), lambda b, i: (i, 0)),
                  pl.BlockSpec((tq, LANES), lambda b, i: (i, 0)),
                  pl.BlockSpec((1, GROUP_WIDTH), lambda b, i: (0, 0)),
                  pl.BlockSpec((LANES, LANES), lambda b, i: (0, 0)),
                  pl.BlockSpec((nc, LANES), lambda b, i: (0, 0)),
                  pl.BlockSpec((LANES, GROUP_WIDTH), lambda b, i: (0, 0)),
                  pl.BlockSpec((LANES, GROUP_WIDTH), lambda b, i: (0, 0))],
        out_specs=[pl.BlockSpec((tq, GROUP_WIDTH), row),
                   pl.BlockSpec((tq, GROUP_WIDTH), row),
                   pl.BlockSpec((tq, LANES), row)],
        compiler_params=_params("parallel", "parallel"),
        name="nsa_cmp_win_select",
    )(a_q, small, kc, vc, kn, a_kv, cos, sin, gq_row, _group_matrix(LANES, HEAD_DIM, True),
      _nsa_overlap(nc, n_sel), _expand_matrix(0, N_HEADS, HEAD_DIM, 3), _expand_matrix(2, N_HEADS, HEAD_DIM, 3))


def _causal_tiles(nq, tq, tk):
    pairs = [(i, k) for i in range(nq) for k in range((i * tq + tq - 1) // tk + 1)]
    return (jnp.asarray([p[0] for p in pairs], jnp.int32), jnp.asarray([p[1] for p in pairs], jnp.int32))


def _nsa2_kernel(qt_ref, kt_ref, qn_ref, sel_ref, e_ref, kn_ref, vs_ref, one_ref, part_ref, sm_ref, eg1_ref, o_ref,
                 m_sc, acc_sc, *, tq, tk):
    qi = qt_ref[pl.program_id(1)]
    ki = kt_ref[pl.program_id(1)]

    @pl.when(ki == 0)
    def _():
        m_sc[...] = jnp.full(m_sc.shape, NEG, F32)
        acc_sc[...] = jnp.zeros(acc_sc.shape, F32)

    def step(causal):
        ks = kn_ref[:, 0:HEAD_DIM]
        vaug = vs_ref[...].astype(BF16) + one_ref[...]
        half = tq // 2
        if causal:
            tpos = qi * tq + lax.broadcasted_iota(jnp.int32, (tq, 1), 0)
            kpos = ki * tk + lax.broadcasted_iota(jnp.int32, (1, tk), 1)
            mask_top = (_dot(sel_ref[0:half, :], e_ref[:, 0:half]) > 0.5) & (kpos[:, 0:half] <= tpos[0:half])
            mask_bot = (_dot(sel_ref[half:tq, :], e_ref[...]) > 0.5) & (kpos <= tpos[half:tq])
        else:
            mask = _dot(sel_ref[...], e_ref[...]) > 0.5
        for h0 in range(0, N_HEADS, 2):
            items = []
            for h in (h0, h0 + 1):
                hs = slice(h * HEAD_DIM, (h + 1) * HEAD_DIM)
                if not causal:
                    items.append((jnp.where(mask, _dot_nt(qn_ref[:, hs], ks), NEG), vaug, h))
                    continue
                s_top = _dot_nt(qn_ref[0:half, hs], ks[0:half])
                items.append((jnp.where(mask_top, s_top, NEG), vaug[0:half], h, slice(0, half)))
                s_bot = _dot_nt(qn_ref[half:tq, hs], ks)
                items.append((jnp.where(mask_bot, s_bot, NEG), vaug, h, slice(half, tq)))
            _flash_update(items, m_sc, acc_sc, tk)

    @pl.when(ki * tk + tk - 1 <= qi * tq)
    def _():
        step(False)

    @pl.when(ki * tk + tk - 1 > qi * tq)
    def _():
        step(True)

    @pl.when(ki == (qi * tq + tq - 1) // tk)
    def _():
        g1x = _dot_c2(jax.nn.sigmoid(sm_ref[...]), eg1_ref[...])
        o = jnp.concatenate([_flash_result(acc_sc, h) for h in range(N_HEADS)], axis=1)
        o_ref[...] = (part_ref[...] + g1x * o).astype(o_ref.dtype)


def _nsa2(qn, sel, kn, a_kv, part, small, B, S, tq=512, tk=512):
    tq, tk = min(tq, S), min(tk, S)
    assert tq == tk, "the diagonal-tile split assumes square tiles"
    nq, nk = S // tq, S // tk
    e = np.zeros((LANES, S), np.float32)
    e[np.arange(S) // NSA_SEL_LEN, np.arange(S)] = 1.0
    qt, kt = _causal_tiles(nq, tq, tk)
    row = lambda b, j, qt, kt: (b * nq + qt[j], 0)
    const = lambda b, j, qt, kt: (0, 0)
    return pl.pallas_call(
        functools.partial(_nsa2_kernel, tq=tq, tk=tk),
        out_shape=jax.ShapeDtypeStruct((B * S, GROUP_WIDTH), BF16),
        grid_spec=pltpu.PrefetchScalarGridSpec(
            num_scalar_prefetch=2,
            grid=(B, qt.shape[0]),
            in_specs=[pl.BlockSpec((tq, GROUP_WIDTH), row),
                      pl.BlockSpec((tq, LANES), row),
                      pl.BlockSpec((LANES, tk), lambda b, j, qt, kt: (0, kt[j])),
                      pl.BlockSpec((tk, LANES), lambda b, j, qt, kt: (b * nk + kt[j], 0)),
                      pl.BlockSpec((tk, LANES), lambda b, j, qt, kt: (b * nk + kt[j], 1)),
                      pl.BlockSpec((1, LANES), const),
                      pl.BlockSpec((tq, GROUP_WIDTH), row),
                      pl.BlockSpec((tq, LANES), row),
                      pl.BlockSpec((LANES, GROUP_WIDTH), const)],
            out_specs=pl.BlockSpec((tq, GROUP_WIDTH), row),
            scratch_shapes=[pltpu.VMEM((N_HEADS, tq, LANES), F32),
                            pltpu.VMEM((N_HEADS, tq, LANES), F32)]),
        compiler_params=_params("parallel", "arbitrary"),
        name="nsa_selected",
    )(qt, kt, qn, sel, jnp.asarray(e, BF16), kn, a_kv, _ones_col_row(), part, small,
      _expand_matrix(1, N_HEADS, HEAD_DIM, 3))


def _diff_kernel(qt_ref, kt_ref, q_ref, k_ref, v_ref, one_ref, lam_ref, gain_ref, o_ref, m_sc, acc_sc,
                 *, tq, tk, lambda_init):
    qi = qt_ref[pl.program_id(1)]
    ki = kt_ref[pl.program_id(1)]

    @pl.when(ki == 0)
    def _():
        m_sc[...] = jnp.full(m_sc.shape, NEG, F32)
        acc_sc[...] = jnp.zeros(acc_sc.shape, F32)

    def step(causal):
        half = tq // 2
        if causal:
            tpos = qi * tq + lax.broadcasted_iota(jnp.int32, (tq, 1), 0)
            kpos = ki * tk + lax.broadcasted_iota(jnp.int32, (1, tk), 1)
            mask_top = kpos[:, 0:half] <= tpos[0:half]
            mask_bot = kpos <= tpos[half:tq]
        for h in range(N_HEADS):
            vaug = v_ref[:, h * LANES:(h + 1) * LANES] + one_ref[...]
            items = []
            for idx in (2 * h, 2 * h + 1):
                cs = slice(idx * DIFF_SUB, (idx + 1) * DIFF_SUB)
                if not causal:
                    items.append((_dot_nt(q_ref[:, cs], k_ref[:, cs]), vaug, idx))
                    continue
                s_top = _dot_nt(q_ref[0:half, cs], k_ref[0:half, cs])
                items.append((jnp.where(mask_top, s_top, NEG), vaug[0:half], idx, slice(0, half)))
                s_bot = _dot_nt(q_ref[half:tq, cs], k_ref[:, cs])
                items.append((jnp.where(mask_bot, s_bot, NEG), vaug, idx, slice(half, tq)))
            _flash_update(items, m_sc, acc_sc, tk)

    @pl.when(ki * tk + tk - 1 <= qi * tq)
    def _():
        step(False)

    @pl.when(ki * tk + tk - 1 > qi * tq)
    def _():
        step(True)

    @pl.when(ki == (qi * tq + tq - 1) // tk)
    def _():
        lm = lam_ref[...]
        lam = (jnp.exp(jnp.sum(lm[0:1] * lm[1:2], axis=-1, keepdims=True))
               - jnp.exp(jnp.sum(lm[2:3] * lm[3:4], axis=-1, keepdims=True)) + lambda_init)
        outs = []
        for h in range(N_HEADS):
            o = _flash_result(acc_sc, 2 * h) - lam * _flash_result(acc_sc, 2 * h + 1)
            o = o * lax.rsqrt(jnp.mean(o * o, axis=-1, keepdims=True) + EPS)
            outs.append(o * gain_ref[...] * (1.0 - lambda_init))
        o_ref[...] = jnp.concatenate(outs, axis=1).astype(o_ref.dtype)


def _diff_attention(qk, v, lam, norm_g, lambda_init, B, S, tq=512, tk=512):
    tq, tk = min(tq, S), min(tk, S)
    assert tq == tk, "the diagonal-tile split assumes square tiles"
    nq, nk = S // tq, S // tk
    lam_pad =jnp.zeros((4, LANES), F32).at[:, :DIFF_SUB].set(lam.astype(F32))
    qt, kt = _causal_tiles(nq, tq, tk)
    row = lambda b, j, qt, kt: (b * nq + qt[j], 0)
    const = lambda b, j, qt, kt: (0, 0)
    return pl.pallas_call(
        functools.partial(_diff_kernel, tq=tq, tk=tk, lambda_init=lambda_init),
        out_shape=jax.ShapeDtypeStruct((B * S, GROUP_WIDTH), BF16),
        grid_spec=pltpu.PrefetchScalarGridSpec(
            num_scalar_prefetch=2,
            grid=(B, qt.shape[0]),
            in_specs=[pl.BlockSpec((tq, GROUP_WIDTH), row),
                      pl.BlockSpec((tk, GROUP_WIDTH), lambda b, j, qt, kt: (b * nk + kt[j], 1)),
                      pl.BlockSpec((tk, N_HEADS * LANES), lambda b, j, qt, kt: (b * nk + kt[j], 0)),
                      pl.BlockSpec((1, LANES), const),
                      pl.BlockSpec((4, LANES), const),
                      pl.BlockSpec((1, HEAD_DIM), const)],
            out_specs=pl.BlockSpec((tq, GROUP_WIDTH), row),
            scratch_shapes=[pltpu.VMEM((2 * N_HEADS, tq, LANES), F32),
                            pltpu.VMEM((2 * N_HEADS, tq, LANES), F32)]),
        compiler_params=_params("parallel", "arbitrary"),
        name="diff_attention",
    )(qt, kt, qk, qk, v, _ones_col_row(), lam_pad, norm_g.reshape(1, HEAD_DIM).astype(F32))


def _gdn_kernel(x_ref, z_ref, sm_ref, cw_ref, alog_ref, dtb_ref, ng_ref, gs_ref, ea_ref, eb_ref, tri_ref,
                o_ref, xs_sc, st_sc, *, tt):
    kconv = cw_ref.shape[0]

    @pl.when(pl.program_id(1) == 0)
    def _():
        xs_sc[0:8, :] = jnp.zeros((8, xs_sc.shape[1]), F32)
        st_sc[...] = jnp.zeros(st_sc.shape, F32)

    xs_sc[8:8 + tt, :] = x_ref[...]
    conv = cw_ref[0:1, :] * xs_sc[pl.ds(8 - (kconv - 1), tt), :]
    for j in range(1, kconv):
        conv = conv + cw_ref[j:j + 1, :] * xs_sc[pl.ds(8 - (kconv - 1) + j, tt), :]
    xs_sc[0:8, :] = x_ref[tt - 8:tt, :]
    qkv = jax.nn.silu(conv)

    def l2n(a):
        out = []
        for c in range(GROUP_WIDTH // LANES):
            xc = a[:, c * LANES:(c + 1) * LANES]
            out.append(xc * lax.rsqrt(_dot_c2(xc * xc, gs_ref[...]) + EPS))
        return jnp.concatenate(out, axis=1)

    q = l2n(qkv[:, 0:GROUP_WIDTH]) * HEAD_DIM ** -0.5
    k = l2n(qkv[:, GROUP_WIDTH:2 * GROUP_WIDTH])
    v = qkv[:, 2 * GROUP_WIDTH:]
    sm = sm_ref[...]
    g_all = -jnp.exp(alog_ref[...]) * jax.nn.softplus(sm + dtb_ref[...])
    gx = _dot_c(g_all, ea_ref[...])
    bx = _dot_c2(jax.nn.sigmoid(sm), eb_ref[...])
    ii = lax.broadcasted_iota(jnp.int32, (CHUNK, CHUNK), 0)
    jj = lax.broadcasted_iota(jnp.int32, (CHUNK, CHUNK), 1)
    eye = jnp.where(ii == jj, 1.0, 0.0)
    nchunk = tt // CHUNK
    heads = [slice(h * HEAD_DIM, (h + 1) * HEAD_DIM) for h in range(N_HEADS)]
    pairs, dec, rhs, qd, kend, eglast, qk_nt, kbk_nt = [], [], [], [], [], [], [], []
    for c in range(nchunk):
        r = slice(c * CHUNK, (c + 1) * CHUNK)
        gcx = _c_dot(tri_ref[...], gx[r])
        gct = gcx.T
        egc = jnp.exp(gcx)
        glast = gcx[CHUNK - 1:CHUNK, :]
        kc, qc, bc = k[r].astype(BF16), q[r], bx[r]
        kb = k[r] * bc
        vb = v[r] * bc
        kbe = kb * egc
        qd.append((qc * egc).astype(BF16))
        kend.append((k[r] * jnp.exp(glast - gcx)).astype(BF16))
        eglast.append(jnp.exp(glast))
        qcb, kbb = qc.astype(BF16), kb.astype(BF16)
        for hs in heads:
            pairs.append((c, hs))
            dec.append(jnp.exp(jnp.where(ii >= jj, gcx[:, hs] - gct[hs, :], NEG)))
            rhs.append(jnp.concatenate([vb[:, hs], kbe[:, hs]], axis=1).astype(BF16))
            kbk_nt.append(_dot_nt(kbb[:, hs], kc[:, hs]))
            qk_nt.append(_dot_nt(qcb[:, hs], kc[:, hs]))
    pw = [-jnp.where(ii > jj, a * d, 0.0) for a, d in zip(kbk_nt, dec)]
    inv = [eye + p for p in pw]
    for _ in range(5):
        pw = [_bdot(p, p) for p in pw]
        inv = [x + _bdot(x, p) for x, p in zip(inv, pw)]
    sol = [_bdot(x, b) for x, b in zip(inv, rhs)]
    attn = [(a * d).astype(BF16) for a, d in zip(qk_nt, dec)]
    state = [st_sc[h] for h in range(N_HEADS)]
    outs = []
    for c in range(nchunk):
        idx = [c * N_HEADS + h for h in range(N_HEADS)]
        sb = [s.astype(BF16) for s in state]
        ws = [_dot(sol[i][:, HEAD_DIM:].astype(BF16), sb[h]) for h, i in enumerate(idx)]
        qs = [_dot(qd[c][:, hs], sb[h]) for h, hs in enumerate(heads)]
        v_new = [(sol[i][:, :HEAD_DIM] - ws[h]).astype(BF16) for h, i in enumerate(idx)]
        outs.append(jnp.concatenate([qs[h] + _dot(attn[i], v_new[h]) for h, i in enumerate(idx)], axis=1))
        state = [state[h] * eglast[c][:, hs] + _dot_tn(kend[c][:, hs], v_new[h]) for h, hs in enumerate(heads)]
    for h in range(N_HEADS):
        st_sc[h] = state[h]
    o = jnp.concatenate(outs, axis=0)
    normed = []
    for c in range(GROUP_WIDTH // LANES):
        oc = o[:, c * LANES:(c + 1) * LANES]
        normed.append(oc * lax.rsqrt(_dot_c2(oc * oc, gs_ref[...]) * (1.0 / HEAD_DIM) + EPS))
    o = jnp.concatenate(normed, axis=1) * ng_ref[...]
    o_ref[...] = (o * jax.nn.silu(z_ref[...])).astype(o_ref.dtype)


def _gdn(b_qkv, b_z, small, conv_w, a_log, dt_bias, norm_g, B, S, tt=1024):
    tt = min(tt, S)
    ns = S // tt
    row = lambda b, i: (b * ns + i, 0)
    const = lambda b, i: (0, 0)
    tri = jnp.asarray(np.tril(np.ones((CHUNK, CHUNK), np.float32)), BF16)
    return pl.pallas_call(
        functools.partial(_gdn_kernel, tt=tt),
        out_shape=jax.ShapeDtypeStruct((B * S, GROUP_WIDTH), BF16),
        grid=(B, ns),
        in_specs=[pl.BlockSpec((tt, 3 * GROUP_WIDTH), row),
                  pl.BlockSpec((tt, GROUP_WIDTH), row),
                  pl.BlockSpec((tt, LANES), row),
                  pl.BlockSpec(conv_w.shape, const),
                  pl.BlockSpec((1, LANES), const),
                  pl.BlockSpec((1, LANES), const),
                  pl.BlockSpec((1, GROUP_WIDTH), const),
                  pl.BlockSpec((LANES, LANES), const),
                  pl.BlockSpec((LANES, GROUP_WIDTH), const),
                  pl.BlockSpec((LANES, GROUP_WIDTH), const),
                  pl.BlockSpec((CHUNK, CHUNK), const)],
        out_specs=pl.BlockSpec((tt, GROUP_WIDTH), row),
        scratch_shapes=[pltpu.VMEM((tt + 8, 3 * GROUP_WIDTH), F32),
                        pltpu.VMEM((N_HEADS, HEAD_DIM, HEAD_DIM), F32)],
        compiler_params=_params("parallel", "arbitrary"),
        name="gated_deltanet",
    )(b_qkv, b_z, small, conv_w.astype(F32), _row128(a_log, _LANE_GDN_A), _row128(dt_bias, _LANE_GDN_A),
      jnp.tile(norm_g.astype(F32), N_HEADS).reshape(1, GROUP_WIDTH), _group_matrix(LANES, HEAD_DIM, False),
      _expand_matrix(_LANE_GDN_A, N_HEADS, HEAD_DIM), _expand_matrix(_LANE_GDN_B, N_HEADS, HEAD_DIM), tri)


def _mlstm_kernel(qk_ref, v_ref, op_ref, sm_ref, ib_ref, fb_ref, ng_ref, gm_ref, ei_ref, ef_ref, tri_ref,
                  o_ref, c_sc, m_sc, *, tt):
    @pl.when(pl.program_id(1) == 0)
    def _():
        c_sc[...] = jnp.zeros(c_sc.shape, F32)
        m_sc[...] = jnp.zeros(m_sc.shape, F32)

    sm = sm_ref[...]
    ig = GATE_CAP * jnp.tanh((sm + ib_ref[...]) * (1.0 / GATE_CAP))
    lf = jax.nn.log_sigmoid(GATE_CAP * jnp.tanh((sm + fb_ref[...]) * (1.0 / GATE_CAP)))
    ix = _dot_c(ig, ei_ref[...])
    fx = _dot_c(lf, ef_ref[...])
    nqk = N_HEADS * MLSTM_QK
    q = qk_ref[:, 0:nqk]
    k = qk_ref[:, nqk:2 * nqk] * MLSTM_QK ** -0.5
    v = v_ref[...]
    ii = lax.broadcasted_iota(jnp.int32, (CHUNK, CHUNK), 0)
    jj = lax.broadcasted_iota(jnp.int32, (CHUNK, CHUNK), 1)
    one_col = jnp.where(jj == 0, 1.0, 0.0)
    nchunk = tt // CHUNK
    cb, dlog, blast, mloc, qh, vaug, cloc, qk_nt = [], [], [], [], [], [], [], []
    for c in range(nchunk):
        r = slice(c * CHUNK, (c + 1) * CHUNK)
        bx = _c_dot(tri_ref[...], fx[r])
        rowv_all = bx - ix[r]
        rowv_t = rowv_all.T
        for h in range(N_HEADS):
            hl = slice(h * LANES, (h + 1) * LANES)
            cb.append(bx[:, hl])
            dlog.append(jnp.where(ii >= jj, bx[:, h * LANES:h * LANES + CHUNK] - rowv_t[h * LANES:h * LANES + CHUNK, :],
                                  NEG))
            blast.append(bx[CHUNK - 1:CHUNK, hl])
            aend = blast[-1] - rowv_all[:, hl]
            mloc.append(jnp.max(aend, axis=0, keepdims=True))
            wend = jnp.exp(aend - mloc[-1])
            qh.append(q[r, h * MLSTM_QK:(h + 1) * MLSTM_QK].astype(BF16))
            kh = k[r, h * MLSTM_QK:(h + 1) * MLSTM_QK]
            vaug.append(jnp.concatenate([v[r, h * HEAD_DIM:(h + 1) * HEAD_DIM], one_col], axis=1).astype(BF16))
            cloc.append(_dot_tn((kh * wend[:, :MLSTM_QK]).astype(BF16), vaug[-1]))
            qk_nt.append(_dot_nt(qh[-1], kh.astype(BF16)))
    c_in, m_in = [], []
    c_st = [c_sc[h] for h in range(N_HEADS)]
    m_st = [m_sc[h][0:1, :] for h in range(N_HEADS)]
    for c in range(nchunk):
        for h in range(N_HEADS):
            i = c * N_HEADS + h
            c_in.append(c_st[h])
            m_in.append(m_st[h])
            m_new = jnp.maximum(blast[i] + m_st[h], mloc[i])
            c_st[h] = jnp.exp(blast[i] + m_st[h] - m_new) * c_st[h] + jnp.exp(mloc[i] - m_new) * cloc[i]
            m_st[h] = m_new
    for h in range(N_HEADS):
        c_sc[h] = c_st[h]
        m_sc[h] = jnp.broadcast_to(m_st[h], (8, LANES))
    inter = [b + m for b, m in zip(cb, m_in)]
    mt = [jnp.maximum(x, jnp.max(d, axis=-1, keepdims=True)) for x, d in zip(inter, dlog)]
    wintra = [(jnp.exp(d - m[:, :CHUNK]) * a).astype(BF16) for d, m, a in zip(dlog, mt, qk_nt)]
    numden = [jnp.exp(x - m) * _dot(qq, cc.astype(BF16)) + _dot(w, va)
              for x, m, qq, cc, w, va in zip(inter, mt, qh, c_in, wintra, vaug)]
    hout = [nd[:, :HEAD_DIM] / jnp.maximum(jnp.abs(nd[:, HEAD_DIM:HEAD_DIM + 1]), jnp.exp(-m[:, 0:1]))
            for nd, m in zip(numden, mt)]
    o = jnp.concatenate([jnp.concatenate(hout[c * N_HEADS:(c + 1) * N_HEADS], axis=1) for c in range(nchunk)], axis=0)
    normed = [_group_rms(o[:, c * LANES:(c + 1) * LANES], gm_ref[...]) for c in range(GROUP_WIDTH // LANES)]
    o = jnp.concatenate(normed, axis=1) * ng_ref[...]
    o_ref[...] = (o * jax.nn.sigmoid(op_ref[...])).astype(o_ref.dtype)


def _mlstm(c_qk, c_v, c_o, small, i_bias, f_bias, norm_g, B, S, tt=512):
    tt = min(tt, S)
    ns = S // tt
    row = lambda b, i: (b * ns + i, 0)
    const = lambda b, i: (0, 0)
    tri = jnp.asarray(np.tril(np.ones((CHUNK, CHUNK), np.float32)), BF16)
    return pl.pallas_call(
        functools.partial(_mlstm_kernel, tt=tt),
        out_shape=jax.ShapeDtypeStruct((B * S, GROUP_WIDTH), BF16),
        grid=(B, ns),
        in_specs=[pl.BlockSpec((tt, GROUP_WIDTH), row),
                  pl.BlockSpec((tt, GROUP_WIDTH), row),
                  pl.BlockSpec((tt, GROUP_WIDTH), row),
                  pl.BlockSpec((tt, LANES), row),
                  pl.BlockSpec((1, LANES), const),
                  pl.BlockSpec((1, LANES), const),
                  pl.BlockSpec((1, GROUP_WIDTH), const),
                  pl.BlockSpec((LANES, LANES), const),
                  pl.BlockSpec((LANES, N_HEADS * LANES), const),
                  pl.BlockSpec((LANES, N_HEADS * LANES), const),
                  pl.BlockSpec((CHUNK, CHUNK), const)],
        out_specs=pl.BlockSpec((tt, GROUP_WIDTH), row),
        scratch_shapes=[pltpu.VMEM((N_HEADS, MLSTM_QK, LANES), F32),
                        pltpu.VMEM((N_HEADS, 8, LANES), F32)],
        compiler_params=_params("parallel", "arbitrary"),
        name="mlstm",
    )(c_qk, c_v, c_o, small, _row128(i_bias, _LANE_I), _row128(f_bias, _LANE_F),
      jnp.tile(norm_g.astype(F32), N_HEADS).reshape(1, GROUP_WIDTH), _group_matrix(LANES, HEAD_DIM, True),
      _expand_matrix(_LANE_I, N_HEADS, LANES), _expand_matrix(_LANE_F, N_HEADS, LANES), tri)


def _memkv_kernel(mem_ref, ln_ref, wkv_ref, gk_ref, gm_ref, k_ref, v_ref):
    x = mem_ref[0]
    xn = (x * lax.rsqrt(jnp.mean(x * x, axis=-1, keepdims=True) + EPS) * ln_ref[...]).astype(BF16)
    kv = _dot(xn, wkv_ref[...])
    for c in range(GROUP_WIDTH // LANES):
        cs = slice(c * LANES, (c + 1) * LANES)
        k_ref[0, :, cs] = (_group_rms(kv[:, cs], gm_ref[...]) * gk_ref[:, cs]).astype(BF16)
    v_ref[0] = kv[:, GROUP_WIDTH:].astype(BF16)


def _memkv(mem, ln_mem, wkv, gain_k):
    B, M, _ = mem.shape
    const = lambda b: (0, 0)
    return pl.pallas_call(
        _memkv_kernel,
        out_shape=[jax.ShapeDtypeStruct((B, M, GROUP_WIDTH), BF16)] * 2,
        grid=(B,),
        in_specs=[pl.BlockSpec((1, M, D_MODEL), lambda b: (b, 0, 0)),
                  pl.BlockSpec((1, D_MODEL), const),
                  pl.BlockSpec((D_MODEL, 2 * GROUP_WIDTH), const),
                  pl.BlockSpec((1, GROUP_WIDTH), const),
                  pl.BlockSpec((LANES, LANES), const)],
        out_specs=[pl.BlockSpec((1, M, GROUP_WIDTH), lambda b: (b, 0, 0))] * 2,
        compiler_params=_params("parallel"),
        name="memory_kv",
    )(mem, ln_mem.reshape(1, D_MODEL), wkv.astype(BF16),
      jnp.tile(gain_k.astype(F32), N_HEADS).reshape(1, GROUP_WIDTH), _group_matrix(LANES, HEAD_DIM, True))


def _out_xattn_kernel(ya_ref, yb_ref, yc_ref, yd_ref, h_ref, wout_ref, lnx_ref, wq_ref, kx_ref, vx_ref, gq_ref, gm_ref,
                      wo_ref, o_ref):
    y = jnp.concatenate([ya_ref[...], yb_ref[...], yc_ref[...], yd_ref[...]], axis=1)
    h1 = h_ref[...] + _dot(y, wout_ref[...])
    hn = (h1 * lax.rsqrt(jnp.mean(h1 * h1, axis=-1, keepdims=True) + EPS) * lnx_ref[...]).astype(BF16)
    q = _dot(hn, wq_ref[...])
    chunks = []
    for c in range(GROUP_WIDTH // LANES):
        cs = slice(c * LANES, (c + 1) * LANES)
        chunks.append((_group_rms(q[:, cs], gm_ref[...]) * gq_ref[:, cs]).astype(BF16))
    qn = jnp.concatenate(chunks, axis=1)
    kx, vx = kx_ref[0], vx_ref[0]
    outs = []
    for h in range(N_HEADS):
        hs = slice(h * HEAD_DIM, (h + 1) * HEAD_DIM)
        s = _dot_nt(qn[:, hs], kx[:, hs]) * HEAD_DIM ** -0.5
        e = jnp.exp(s - jnp.max(s, axis=-1, keepdims=True))
        p = e * (1.0 / jnp.sum(e, axis=-1, keepdims=True))
        outs.append(_dot(p.astype(BF16), vx[:, hs]))
    o = jnp.concatenate(outs, axis=1).astype(BF16)
    o_ref[...] = h1 + _dot(o, wo_ref[...])


def _out_xattn(ys, h2d, w_out, ln_x, wq, kx, vx, gain_q, wo, B, S, tm=512):
    tm = min(tm, S)
    ns = S // tm
    M = kx.shape[1]
    row = lambda b, i: (b * ns + i, 0)
    const = lambda b, i: (0, 0)
    return pl.pallas_call(
        _out_xattn_kernel,
        out_shape=jax.ShapeDtypeStruct((B * S, D_MODEL), F32),
        grid=(B, ns),
        in_specs=[pl.BlockSpec((tm, GROUP_WIDTH), row)] * 4 + [
            pl.BlockSpec((tm, D_MODEL), row),
            pl.BlockSpec((D_MODEL, D_MODEL), const),
            pl.BlockSpec((1, D_MODEL), const),
            pl.BlockSpec((D_MODEL, GROUP_WIDTH), const),
            pl.BlockSpec((1, M, GROUP_WIDTH), lambda b, i: (b, 0, 0)),
            pl.BlockSpec((1, M, GROUP_WIDTH), lambda b, i: (b, 0, 0)),
            pl.BlockSpec((1, GROUP_WIDTH), const),
            pl.BlockSpec((LANES, LANES), const),
            pl.BlockSpec((GROUP_WIDTH, D_MODEL), const)],
        out_specs=pl.BlockSpec((tm, D_MODEL), row),
        compiler_params=_params("parallel", "parallel"),
        name="outproj_xattn",
    )(*ys, h2d, w_out.astype(BF16), ln_x.reshape(1, D_MODEL), wq.astype(BF16), kx, vx,
      jnp.tile(gain_q.astype(F32), N_HEADS).reshape(1, GROUP_WIDTH), _group_matrix(LANES, HEAD_DIM, True),
      wo.astype(BF16))


def _moe_route_kernel(h_ref, ln_ref, wrh_ref, wrl_ref, br_ref, tri_ref, hn_ref, comb_ref, rkt_ref, cnt_ref):
    tm = h_ref.shape[0]
    lane = lax.broadcasted_iota(jnp.int32, (tm, LANES), 1)
    x = h_ref[...]
    hn = x * lax.rsqrt(jnp.mean(x * x, axis=-1, keepdims=True) + EPS) * ln_ref[...]
    hn_hi = hn.astype(BF16)
    hn_ref[...] = hn_hi
    hn_lo = (hn - hn_hi.astype(F32)).astype(BF16)
    logits = _dot(hn_hi, wrh_ref[...]) + _dot(hn_hi, wrl_ref[...]) + _dot(hn_lo, wrh_ref[...]) + br_ref[...]
    lanef = lane.astype(F32)
    big = 1e4
    isg = (lane >= MOE_EXPERTS) & (lane < MOE_EXPERTS + MOE_GROUPS)
    lg = jnp.where(isg, logits, NEG)
    gmax = jnp.max(lg, axis=-1, keepdims=True)
    grp_p = 1.0 / jnp.sum(jnp.exp(lg - gmax), axis=-1, keepdims=True)
    gidx = jnp.min(jnp.where(lg == gmax, lanef, big), axis=-1, keepdims=True) - MOE_EXPERTS
    ing = (lane < MOE_EXPERTS) & ((lane >> 3).astype(F32) == gidx)
    le = jnp.where(ing, logits, NEG)
    m1 = jnp.max(le, axis=-1, keepdims=True)
    z = jnp.sum(jnp.where(ing, jnp.exp(le - m1), 0.0), axis=-1, keepdims=True)
    i1 = jnp.min(jnp.where(le == m1, lanef, big), axis=-1, keepdims=True)
    oh1 = lanef == i1
    le2 = jnp.where(oh1, NEG, le)
    m2 = jnp.max(le2, axis=-1, keepdims=True)
    i2 = jnp.min(jnp.where((le2 == m2) & ing, jnp.where(oh1, big, lanef), big), axis=-1, keepdims=True)
    oh2 = lanef == i2
    p1 = 1.0 / z
    p2 = jnp.exp(m2 - m1) / z
    tot = p1 + p2
    comb_ref[...] = jnp.where(oh1, p1 / tot * grp_p, 0.0) + jnp.where(oh2, p2 / tot * grp_p, 0.0)
    member = jnp.where(lanef == gidx, 1.0, 0.0)
    rank = _dot(tri_ref[...], member.astype(BF16))
    rkt_ref[0] = jnp.where(member > 0.5, rank, -1.0).T[0:8, :]
    cnt_ref[0] = jnp.broadcast_to(jnp.sum(member, axis=0, keepdims=True), (8, LANES))


def _moe_group_kernel(cnt_ref, hn_ref, comb_ref, rkt_ref, prev_ref, w1_ref, w3_ref, w2_ref, o_ref, acc_sc,
                      *, group, tm, ch):
    cnt = cnt_ref[pl.program_id(0) * MOE_GROUPS + group]
    acc_sc[...] = jnp.zeros(acc_sc.shape, F32)
    rkg = rkt_ref[0][group:group + 1, :]
    starts = [(0, ch)] + [(s, ch // 2) for s in range(ch, tm, ch // 2)]
    for start, ch in starts:
        @pl.when(start < cnt)
        def _():
            rows = (start + lax.broadcasted_iota(jnp.int32, (ch, 1), 0)).astype(F32)
            sel = jnp.where(rkg == rows, 1.0, 0.0).astype(BF16)
            xg = _dot(sel, hn_ref[...]).astype(BF16)
            cg = _c_dot2(sel, comb_ref[...])
            yg = jnp.zeros((ch, D_MODEL), F32)
            for e in range(MOE_PER_GROUP):
                hg = _dot(xg, w1_ref[e])
                hu = _dot(xg, w3_ref[e])
                ce = cg[:, group * MOE_PER_GROUP + e:group * MOE_PER_GROUP + e + 1]
                yg = yg + _dot((jax.nn.silu(hg) * hu * ce).astype(BF16), w2_ref[e])
            acc_sc[...] += _dot_tn(sel, yg.astype(BF16))
    o_ref[...] = prev_ref[...] + acc_sc[...]


def _moe(h2d, ln, w_group, b_group, w_expert, b_expert, w1, w3, w2, tm=1024, ch=256):
    T = h2d.shape[0]
    tm = min(tm, T)
    nt = T // tm
    pad = LANES - MOE_EXPERTS - MOE_GROUPS
    wr = jnp.concatenate([w_expert, w_group, jnp.zeros((D_MODEL, pad), F32)], axis=1)
    wr_hi = wr.astype(BF16)
    br =jnp.concatenate([b_expert, b_group, jnp.zeros((pad,), F32)]).reshape(1, LANES)
    tri = jnp.asarray(np.tril(np.ones((tm, tm), np.float32), -1), BF16)
    const = lambda i: (0, 0)
    hn, comb, rkt, cnt = pl.pallas_call(
        _moe_route_kernel,
        out_shape=[jax.ShapeDtypeStruct((T, D_MODEL), BF16),
                   jax.ShapeDtypeStruct((T, LANES), F32),
                   jax.ShapeDtypeStruct((nt, 8, tm), F32),
                   jax.ShapeDtypeStruct((nt, 8, LANES), F32)],
        grid=(nt,),
        in_specs=[pl.BlockSpec((tm, D_MODEL), lambda i: (i, 0)),
                  pl.BlockSpec((1, D_MODEL), const),
                  pl.BlockSpec((D_MODEL, LANES), const),
                  pl.BlockSpec((D_MODEL, LANES), const),
                  pl.BlockSpec((1, LANES), const),
                  pl.BlockSpec((tm, tm), const)],
        out_specs=[pl.BlockSpec((tm, D_MODEL), lambda i: (i, 0)),
                   pl.BlockSpec((tm, LANES), lambda i: (i, 0)),
                   pl.BlockSpec((1, 8, tm), lambda i: (i, 0, 0)),
                   pl.BlockSpec((1, 8, LANES), lambda i: (i, 0, 0))],
        compiler_params=_params("parallel"),
        name="moe_route",
    )(h2d, ln.reshape(1, D_MODEL), wr_hi, (wr - wr_hi.astype(F32)).astype(BF16), br, tri)
    counts = cnt[:, 0, :MOE_GROUPS].astype(jnp.int32).reshape(nt * MOE_GROUPS)
    w1b, w3b, w2b = w1.astype(BF16), w3.astype(BF16), w2.astype(BF16)
    out = h2d
    for g in range(MOE_GROUPS):
        wspec = lambda shape: pl.BlockSpec((MOE_PER_GROUP,) + shape, lambda i, c, g=g: (g, 0, 0),
                                           pipeline_mode=pl.Buffered(1))
        out = pl.pallas_call(
            functools.partial(_moe_group_kernel, group=g, tm=tm, ch=ch),
            out_shape=jax.ShapeDtypeStruct((T, D_MODEL), F32),
            grid_spec=pltpu.PrefetchScalarGridSpec(
                num_scalar_prefetch=1,
                grid=(nt,),
                in_specs=[pl.BlockSpec((tm, D_MODEL), lambda i, c: (i, 0)),
                          pl.BlockSpec((tm, LANES), lambda i, c: (i, 0)),
                          pl.BlockSpec((1, 8, tm), lambda i, c: (i, 0, 0)),
                          pl.BlockSpec((tm, D_MODEL), lambda i, c: (i, 0)),
                          wspec((D_MODEL, MOE_FF)), wspec((D_MODEL, MOE_FF)), wspec((MOE_FF, D_MODEL))],
                out_specs=pl.BlockSpec((tm, D_MODEL), lambda i, c: (i, 0)),
                scratch_shapes=[pltpu.VMEM((tm, D_MODEL), F32)]),
            compiler_params=_params("parallel"),
            name="moe_group",
        )(counts, hn, comb, rkt, out, w1b, w3b, w2b)
    return out


def _nsa_mixer(a_q, a_kv, kn, small, qk_gain, cmp_pe, cmp_w1, cmp_w2, cos, sin, B, S):
    nc = S // NSA_CMP_STRIDE
    wide = NSA_CMP_STRIDE * HEAD_DIM
    cos_c, sin_c = _rope_tables(NSA_CMP_STRIDE * np.arange(nc) + NSA_CMP_LEN - 1, HEAD_DIM)
    half = HEAD_DIM // 2
    kc, vc = _nsa_compress(a_kv, cmp_pe.reshape(2, 2 * wide), cmp_w1, cmp_w2, qk_gain[1],
                           cos_c[:, :half], sin_c[:, half:2 * half], B, nc)
    gq_row = jnp.tile(qk_gain[0].astype(F32), N_HEADS).reshape(1, GROUP_WIDTH)
    qn, part, sel = _nsa1(a_q, small, kc, vc, kn, a_kv, cos, sin, gq_row, B, S)
    return _nsa2(qn, sel, kn, a_kv, part, small, B, S)


def kernel(x, mem, ln_mix, w_in, w_out, nsa_qk_gain, nsa_cmp_pe, nsa_cmp_w1, nsa_cmp_w2, gdn_conv, gdn_a_log, gdn_dt_bias, gdn_norm, mlstm_i_bias, mlstm_f_bias, mlstm_norm, diff_qk_gain, diff_lambda, diff_norm, ln_xattn, ln_mem, xattn_wq, xattn_wkv, xattn_qk_gain, xattn_wo, ln_moe, moe_w_group, moe_b_group, moe_w_expert, moe_b_expert, moe_w1, moe_w3, moe_w2):
    B, S, D = x.shape
    depth = w_in.shape[0]
    cos_a, sin_a = _rope_tables(np.arange(S), HEAD_DIM)
    cos_d, sin_d = _rope_tables(np.arange(S), DIFF_SUB)
    h = x.reshape(B * S, D)
    for l in range(depth):
        gain_k = jnp.concatenate([nsa_qk_gain[l, 2], nsa_qk_gain[l, 3]]).reshape(1, LANES).astype(F32)
        gain_d = jnp.concatenate([jnp.tile(diff_qk_gain[l, 0], 2 * N_HEADS) * DIFF_SUB ** -0.5,
                                  jnp.tile(diff_qk_gain[l, 1], 2 * N_HEADS)]).reshape(1, 2 * GROUP_WIDTH).astype(F32)
        a_q, a_kv, b_qkv, b_z, c_qk, c_v, c_o, d_v, small, dqk, kn = _inproj(
            h, ln_mix[l], w_in[l], gain_k, gain_d, cos_a, sin_a, cos_d, sin_d, S)
        y_a = _nsa_mixer(a_q, a_kv, kn, small, nsa_qk_gain[l], nsa_cmp_pe[l], nsa_cmp_w1[l], nsa_cmp_w2[l],
                         cos_a, sin_a, B, S)
        y_b = _gdn(b_qkv, b_z, small, gdn_conv[l], gdn_a_log[l], gdn_dt_bias[l], gdn_norm[l], B, S)
        y_c = _mlstm(c_qk, c_v, c_o, small, mlstm_i_bias[l], mlstm_f_bias[l], mlstm_norm[l], B, S)
        lambda_init = 0.8 - 0.6 * math.exp(-0.3 * l)
        y_d = _diff_attention(dqk, d_v, diff_lambda[l], diff_norm[l], lambda_init, B, S)
        kx, vx = _memkv(mem, ln_mem[l], xattn_wkv[l], xattn_qk_gain[l, 1])
        h = _out_xattn((y_a, y_b, y_c, y_d), h, w_out[l], ln_xattn[l], xattn_wq[l], kx, vx,
                       xattn_qk_gain[l, 0], xattn_wo[l], B, S)
        h = _moe(h, ln_moe[l], moe_w_group[l], moe_b_group[l], moe_w_expert[l], moe_b_expert[l],
                 moe_w1[l], moe_w3[l], moe_w2[l])
    return h.reshape(B, S, D)
```
